```python
import math
import jax, jax.numpy as jnp
from jax import lax
import numpy as np

D_MODEL = 1024
BATCH = 8
SEQ = 2048
DEPTH = 2
DEC_BATCH = 128
DEC_SEQ = 8
PAST_LEN = 16384
PAGE_SIZE = 128

D_MIX = D_MODEL
DN_HEAD_DIM = 128
DN_WIDTH = D_MIX // 2
DN_HEADS = DN_WIDTH // DN_HEAD_DIM
DN_CONV = 4
DN_CHUNK = 64
POOL_WINDOWS = (2, 4, 8, 16)
POOL_GROUPS = len(POOL_WINDOWS)
POOL_WIDTH = D_MIX // 4
POOL_GROUP_DIM = POOL_WIDTH // POOL_GROUPS
POOL_BUF = max(POOL_WINDOWS) - 1
SC_WIDTH = D_MIX - DN_WIDTH - POOL_WIDTH
SC_CONV = 3
D_FF = 2816
FFN_CONV = 3
EPS = 1e-6
SPLITS = (3 * DN_WIDTH, DN_WIDTH, DN_HEADS, DN_HEADS, POOL_WIDTH, SC_WIDTH, SC_WIDTH, SC_WIDTH)
D_IN = sum(SPLITS)

kernel_name = "hybrid_delta_pool_shortconv_decoder_step"


def rmsnorm(x, g):
    xf = x.astype(jnp.float32)
    y = xf * lax.rsqrt(jnp.mean(xf * xf, axis=-1, keepdims=True) + EPS)
    return (y * g.astype(jnp.float32)).astype(x.dtype)


def l2norm(t):
    tf = t.astype(jnp.float32)
    return tf * lax.rsqrt(jnp.sum(tf * tf, axis=-1, keepdims=True) + EPS)


def causal_dwconv(x, w, buf):
    width = w.shape[0]
    L = x.shape[1]
    xp = jnp.concatenate([buf.astype(x.dtype), x], axis=1)
    y = xp[:, 0:L] * w[0]
    for i in range(1, width):
        y = y + xp[:, i:i + L] * w[i]
    return y, xp[:, xp.shape[1] - (width - 1):]


def gated_delta_rule(q, k, v, g, beta, s0):
    f32 = jnp.float32
    bsz, L, h, dk = q.shape
    dv = v.shape[-1]
    c = min(DN_CHUNK, L)
    n = -(-L // c)
    pad = n * c - L

    def prep(t):
        t = t.astype(f32)
        t = jnp.pad(t, [(0, 0), (0, pad)] + [(0, 0)] * (t.ndim - 2))
        t = t.reshape((bsz, n, c) + t.shape[2:])
        return jnp.moveaxis(t, 3, 1)

    q, k, v, g, beta = prep(q), prep(k), prep(v), prep(g), prep(beta)
    q = q * (dk ** -0.5)
    gc = jnp.cumsum(g, axis=-1)
    idx = jnp.arange(c)
    causal = idx[:, None] >= idx[None, :]
    strict = idx[:, None] > idx[None, :]
    decay = jnp.exp(jnp.where(causal, gc[..., :, None] - gc[..., None, :], -jnp.inf))
    kb = k * beta[..., None]
    a = jnp.einsum('bhnid,bhnjd->bhnij', kb, k) * jnp.where(strict, decay, 0.0)
    eye = jnp.eye(c, dtype=f32)
    rhs = jnp.concatenate([v * beta[..., None], kb * jnp.exp(gc)[..., None]], axis=-1)
    sol = lax.linalg.triangular_solve(eye + a, rhs, left_side=True, lower=True)
    u, w = sol[..., :dv], sol[..., dv:]
    qk = jnp.einsum('bhnid,bhnjd->bhnij', q, k) * decay
    qg = q * jnp.exp(gc)[..., None]
    g_last = gc[..., -1]
    kd = k * jnp.exp(g_last[..., None] - gc)[..., None]

    def step(s, xs):
        u_i, w_i, qk_i, qg_i, kd_i, gl_i = xs
        v_new = u_i - jnp.einsum('bhck,bhkv->bhcv', w_i, s)
        o = jnp.einsum('bhck,bhkv->bhcv', qg_i, s) + jnp.einsum('bhij,bhjv->bhiv', qk_i, v_new)
        s = s * jnp.exp(gl_i)[..., None, None] + jnp.einsum('bhck,bhcv->bhkv', kd_i, v_new)
        return s, o

    xs = tuple(jnp.moveaxis(t, 2, 0) for t in (u, w, qk, qg, kd, g_last))
    s, o = lax.scan(step, s0.astype(f32), xs)
    o = jnp.moveaxis(o, 0, 2)
    o = jnp.moveaxis(o, 1, 3).reshape(bsz, n * c, h, dv)[:, :L]
    return o, s


def delta_mixer(qkv, z, a, bg, conv_w, a_log, dt_bias, out_norm, s0, buf):
    bsz, L, _ = qkv.shape
    cv, new_buf = causal_dwconv(qkv, conv_w, buf)
    cv = jax.nn.silu(cv)
    q, k, v = jnp.split(cv, 3, axis=-1)
    q = l2norm(q.reshape(bsz, L, DN_HEADS, DN_HEAD_DIM))
    k = l2norm(k.reshape(bsz, L, DN_HEADS, DN_HEAD_DIM))
    v = v.reshape(bsz, L, DN_HEADS, DN_HEAD_DIM)
    g = -jnp.exp(a_log.astype(jnp.float32)) * jax.nn.softplus(a.astype(jnp.float32) + dt_bias.astype(jnp.float32))
    beta = jax.nn.sigmoid(bg.astype(jnp.float32))
    o, s = gated_delta_rule(q, k, v, g, beta, s0)
    zf = z.reshape(bsz, L, DN_HEADS, DN_HEAD_DIM).astype(jnp.float32)
    o = o * lax.rsqrt(jnp.mean(o * o, axis=-1, keepdims=True) + EPS) * out_norm.astype(jnp.float32) * jax.nn.silu(zf)
    return o.reshape(bsz, L, DN_WIDTH).astype(qkv.dtype), s, new_buf


def pool_mixer(xin, buf, w_grp, scale, pos0):
    bsz, L, wd = xin.shape
    xf = xin.astype(jnp.float32)
    ext = jnp.concatenate([buf.astype(jnp.float32), xf], axis=1)
    cs = jnp.concatenate([jnp.zeros((bsz, 1, wd), jnp.float32), jnp.cumsum(ext, axis=1)], axis=1)
    pos = jnp.arange(L) + pos0
    hi = cs[:, POOL_BUF + 1:POOL_BUF + 1 + L]
    outs = []
    for gi, win in enumerate(POOL_WINDOWS):
        lo_c, hi_c = gi * POOL_GROUP_DIM, (gi + 1) * POOL_GROUP_DIM
        ssum = hi[..., lo_c:hi_c] - cs[:, POOL_BUF + 1 - win:POOL_BUF + 1 - win + L, lo_c:hi_c]
        cnt = jnp.minimum(pos + 1, win).astype(jnp.float32)
        outs.append(ssum / cnt[None, :, None])
    d = (jnp.concatenate(outs, axis=-1) - xf).reshape(bsz, L, POOL_GROUPS, POOL_GROUP_DIM)
    y = jnp.einsum('blgc,gcd->blgd', d, w_grp.astype(jnp.float32)).reshape(bsz, L, wd) * scale.astype(jnp.float32)
    return y.astype(xin.dtype), ext[:, ext.shape[1] - POOL_BUF:].astype(xin.dtype)


def trunk(x, states, pos0, params):
    st_delta, st_dconv, st_pool, st_sconv, st_fconv = states
    (norm_mix, w_in, dn_conv_w, dn_a_log, dn_dt_bias, dn_out_norm, pool_w, pool_scale,
     sconv_w, w_out, norm_ffn, w_ffn_gate, ffn_conv_w, w_ffn_up, w_ffn_down, final_norm) = params
    sd, sdc, sp, ssc, sfc = [], [], [], [], []
    cuts = list(np.cumsum(SPLITS)[:-1])
    for l in range(DEPTH):
        h = rmsnorm(x, norm_mix[l])
        proj = h @ w_in[l]
        qkv, z, a, bg, p_in, sc_x, sc_b, sc_c = jnp.split(proj, cuts, axis=-1)
        o_a, s_new, dbuf = delta_mixer(qkv, z, a, bg, dn_conv_w[l], dn_a_log[l], dn_dt_bias[l],
                                       dn_out_norm[l], st_delta[l], st_dconv[l])
        o_b, pbuf = pool_mixer(p_in, st_pool[l], pool_w[l], pool_scale[l], pos0)
        cconv, cbuf = causal_dwconv(sc_c * sc_x, sconv_w[l], st_sconv[l])
        o_c = sc_b * cconv
        x = x + jnp.concatenate([o_a, o_b, o_c], axis=-1) @ w_out[l]
        h = rmsnorm(x, norm_ffn[l])
        gate, fbuf = causal_dwconv(h @ w_ffn_gate[l], ffn_conv_w[l], st_fconv[l])
        x = x + (jax.nn.silu(gate) * (h @ w_ffn_up[l])) @ w_ffn_down[l]
        sd.append(s_new); sdc.append(dbuf); sp.append(pbuf); ssc.append(cbuf); sfc.append(fbuf)
    y = rmsnorm(x, final_norm)
    return y, (jnp.stack(sd), jnp.stack(sdc), jnp.stack(sp), jnp.stack(ssc), jnp.stack(sfc))


def setup_inputs(seed: int = 0) -> dict:
    key = jax.random.key(seed)
    ks = iter(jax.random.split(key, 32))

    def nrm(shape, scale=1.0):
        return jax.random.normal(next(ks), shape, jnp.float32) * scale

    dt = jnp.exp(jax.random.uniform(next(ks), (DEPTH, DN_HEADS), jnp.float32,
                                    minval=math.log(1e-3), maxval=math.log(1e-1)))
    return {
        "x_prompt": nrm((BATCH, SEQ, D_MODEL)),
        "x_sample": nrm((DEC_BATCH, DEC_SEQ, D_MODEL)),
        "state_delta": nrm((DEPTH, DEC_BATCH, DN_HEADS, DN_HEAD_DIM, DN_HEAD_DIM), 0.1),
        "state_delta_conv": nrm((DEPTH, DEC_BATCH, DN_CONV - 1, 3 * DN_WIDTH)),
        "state_pool": nrm((DEPTH, DEC_BATCH, POOL_BUF, POOL_WIDTH)),
        "state_sconv": nrm((DEPTH, DEC_BATCH, SC_CONV - 1, SC_WIDTH)),
        "state_ffn_conv": nrm((DEPTH, DEC_BATCH, FFN_CONV - 1, D_FF)),
        "norm_mix": 1.0 + nrm((DEPTH, D_MODEL), 0.05),
        "w_in": nrm((DEPTH, D_MODEL, D_IN), D_MODEL ** -0.5),
        "dn_conv_w": nrm((DEPTH, DN_CONV, 3 * DN_WIDTH), 0.5),
        "dn_a_log": jnp.log(jax.random.uniform(next(ks), (DEPTH, DN_HEADS), jnp.float32, minval=1.0, maxval=16.0)),
        "dn_dt_bias": dt + jnp.log(-jnp.expm1(-dt)),
        "dn_out_norm": 1.0 + nrm((DEPTH, DN_HEAD_DIM), 0.05),
        "pool_w": nrm((DEPTH, POOL_GROUPS, POOL_GROUP_DIM, POOL_GROUP_DIM), POOL_GROUP_DIM ** -0.5),
        "pool_scale": 0.5 + nrm((DEPTH, POOL_WIDTH), 0.1),
        "sconv_w": nrm((DEPTH, SC_CONV, SC_WIDTH), 0.5),
        "w_out": nrm((DEPTH, D_MIX, D_MODEL), D_MIX ** -0.5),
        "norm_ffn": 1.0 + nrm((DEPTH, D_MODEL), 0.05),
        "w_ffn_gate": nrm((DEPTH, D_MODEL, D_FF), D_MODEL ** -0.5),
        "ffn_conv_w": nrm((DEPTH, FFN_CONV, D_FF), 0.5),
        "w_ffn_up": nrm((DEPTH, D_MODEL, D_FF), D_MODEL ** -0.5),
        "w_ffn_down": nrm((DEPTH, D_FF, D_MODEL), D_FF ** -0.5),
        "final_norm": 1.0 + nrm((D_MODEL,), 0.05),
    }


def reference(x_prompt, x_sample, state_delta, state_delta_conv, state_pool, state_sconv, state_ffn_conv,
              norm_mix, w_in, dn_conv_w, dn_a_log, dn_dt_bias, dn_out_norm, pool_w, pool_scale,
              sconv_w, w_out, norm_ffn, w_ffn_gate, ffn_conv_w, w_ffn_up, w_ffn_down, final_norm):
    params = (norm_mix, w_in, dn_conv_w, dn_a_log, dn_dt_bias, dn_out_norm, pool_w, pool_scale,
              sconv_w, w_out, norm_ffn, w_ffn_gate, ffn_conv_w, w_ffn_up, w_ffn_down, final_norm)
    dt = x_prompt.dtype
    zero_states = (
        jnp.zeros((DEPTH, BATCH, DN_HEADS, DN_HEAD_DIM, DN_HEAD_DIM), jnp.float32),
        jnp.zeros((DEPTH, BATCH, DN_CONV - 1, 3 * DN_WIDTH), dt),
        jnp.zeros((DEPTH, BATCH, POOL_BUF, POOL_WIDTH), dt),
        jnp.zeros((DEPTH, BATCH, SC_CONV - 1, SC_WIDTH), dt),
        jnp.zeros((DEPTH, BATCH, FFN_CONV - 1, D_FF), dt),
    )
    y_prompt, p_st = trunk(x_prompt, zero_states, 0, params)
    y_sample, s_st = trunk(x_sample, (state_delta, state_delta_conv, state_pool, state_sconv, state_ffn_conv),
                           PAST_LEN, params)
    return (y_prompt, y_sample,
            p_st[0].astype(state_delta.dtype), p_st[1].astype(state_delta_conv.dtype),
            p_st[2].astype(state_pool.dtype), p_st[3].astype(state_sconv.dtype),
            p_st[4].astype(state_ffn_conv.dtype),
            s_st[0].astype(state_delta.dtype), s_st[1].astype(state_delta_conv.dtype),
            s_st[2].astype(state_pool.dtype), s_st[3].astype(state_sconv.dtype),
            s_st[4].astype(state_ffn_conv.dtype))
```

```python
import functools
import math

import jax
import jax.numpy as jnp
from jax import lax
from jax.experimental import pallas as pl
from jax.experimental.pallas import tpu as pltpu

F32 = jnp.float32
BF16 = jnp.bfloat16

LANES = 128
SUBLANES = 8
VMEM_LIMIT_BYTES = 56 * 1024 * 1024

D_MODEL = 1024
DN_HEADS = 4
DN_HEAD_DIM = 128
DN_WIDTH = DN_HEADS * DN_HEAD_DIM
DN_CONV = 4
DN_CHUNK = 64
POOL_WINDOWS = (2, 4, 8, 16)
POOL_WIDTH = 256
POOL_GROUP_DIM = POOL_WIDTH // len(POOL_WINDOWS)
POOL_BUF = max(POOL_WINDOWS) - 1
SC_WIDTH = 256
SC_CONV = 3
D_FF = 2816
FFN_CONV = 3
EPS = 1e-6
PAST_LEN = 16384

C_QKV = 0
C_Z = 3 * DN_WIDTH
C_P = C_Z + DN_WIDTH
C_SX = C_P + POOL_WIDTH
C_SB = C_SX + SC_WIDTH
C_SC = C_SB + SC_WIDTH
C_G = C_SC + SC_WIDTH
D_IN_PAD = C_G + LANES

CONV_HIST = SUBLANES
POOL_HIST = 24


def _silu(t):
    return t / (1.0 + jnp.exp(-t))


def _softplus(t):
    return jnp.maximum(t, 0.0) + jnp.log1p(jnp.exp(-jnp.abs(t)))


def _rms_scale(t):
    return t * lax.rsqrt(jnp.mean(t * t, axis=-1, keepdims=True) + EPS)


def _dot(a, b):
    return jnp.dot(a, b, preferred_element_type=F32)


def _dot_nt(a, b):
    return lax.dot_general(a, b, (((1,), (1,)), ((), ())), preferred_element_type=F32)


def _carry_history(ext, state_ref, j, hist, nbuf, length):
    @pl.when(j == 0)
    def _():
        ext[:, 0:hist - nbuf, :] = jnp.zeros((ext.shape[0], hist - nbuf, ext.shape[2]), F32)
        ext[:, hist - nbuf:hist, :] = state_ref[...]

    @pl.when(j != 0)
    def _():
        ext[:, hist - nbuf:hist, :] = ext[:, hist + length - nbuf:hist + length, :]


def _unit_lower_inverse(a, row, col, log2c):
    n = a.shape[0]
    eye = jnp.where(row == col, 1.0, 0.0).astype(F32)
    pair = ((row >> 1) == (col >> 1)) & ((row & 1) == 1) & ((col & 1) == 0)
    t = eye - jnp.where(pair, a, 0.0)
    for lvl in range(1, log2c):
        off = (((row >> (lvl + 1)) == (col >> (lvl + 1)))
               & (((row >> lvl) & 1) == 1) & (((col >> lvl) & 1) == 0))
        a_off = jnp.where(off, a, 0.0).astype(BF16)
        t16 = t.astype(BF16)
        t = t - _dot(t16, _dot(a_off, t16).astype(BF16))
    del n
    return t


def _mixer_kernel(x_ref, sd_ref, sdc_ref, sp_ref, ssc_ref,
                  nmix_ref, win_ref, dnw_ref, alog_ref, dtb_ref, onorm_ref, pw_ref, pscale_ref,
                  scw_ref, wout_ref,
                  xo_ref, od_ref, odc_ref, op_ref, osc_ref,
                  proj_s, extd_s, extp_s, s2_s, s4_s, s8_s, exts_s,
                  u_s, w_s, qg_s, qk_s, kdt_s, egl_s, vn_s, o_s, cat_s,
                  *, nb, length, chunk, pos0):
    j = pl.program_id(1)
    rows = nb * length
    log2c = int(math.log2(chunk))
    chunks_per_row = length // chunk
    n_chunks = rows // chunk

    x = x_ref[...].reshape(rows, D_MODEL)
    h16 = (_rms_scale(x) * nmix_ref[...]).astype(BF16)
    proj_s[...] = _dot(h16, win_ref[...])

    @pl.when(j == 0)
    def _():
        od_ref[...] = sd_ref[...]

    _carry_history(extd_s, sdc_ref, j, CONV_HIST, DN_CONV - 1, length)
    extd_s[:, CONV_HIST:CONV_HIST + length, :] = proj_s[:, C_QKV:C_QKV + 3 * DN_WIDTH].reshape(
        nb, length, 3 * DN_WIDTH)
    odc_ref[...] = extd_s[:, CONV_HIST + length - (DN_CONV - 1):CONV_HIST + length, :]

    def conv_block(ci):
        cs = slice(ci * LANES, (ci + 1) * LANES)
        base = CONV_HIST - (DN_CONV - 1)
        acc = extd_s[:, base:base + length, cs] * dnw_ref[0:1, cs][None]
        for i in range(1, DN_CONV):
            acc = acc + extd_s[:, base + i:base + i + length, cs] * dnw_ref[i:i + 1, cs][None]
        return _silu(acc).reshape(rows, LANES)

    gates = proj_s[:, C_G:C_G + LANES]
    g_all = -jnp.exp(alog_ref[...]) * _softplus(gates + dtb_ref[...])
    beta_all = 1.0 / (1.0 + jnp.exp(-gates))
    rowi = lax.broadcasted_iota(jnp.int32, (rows, LANES), 0)
    pos_in_chunk = rowi & (chunk - 1)
    gc_all = g_all
    step = 1
    while step < chunk:
        gc_all = gc_all + jnp.where(pos_in_chunk >= step, pltpu.roll(gc_all, step, axis=0), 0.0)
        step *= 2
    gl_all = jnp.broadcast_to(gc_all.reshape(n_chunks, chunk, LANES)[:, chunk - 1:chunk, :],
                              (n_chunks, chunk, LANES)).reshape(rows, LANES)
    gc_t = gc_all.T

    row = lax.broadcasted_iota(jnp.int32, (rows, rows), 0)
    col = lax.broadcasted_iota(jnp.int32, (rows, rows), 1)
    same_chunk = (row >> log2c) == (col >> log2c)
    m_incl = same_chunk & (row >= col)
    m_strict = same_chunk & (row > col)

    def colb(a, lane, width=LANES):
        return jnp.broadcast_to(a[:, lane:lane + 1], (rows, width))

    for hh in range(DN_HEADS):
        beta_b = colb(beta_all, DN_HEADS + hh)
        gc_b = colb(gc_all, hh)
        gl_b = colb(gl_all, hh)
        egc_b = jnp.exp(gc_b)
        qc = conv_block(hh)
        kc = conv_block(DN_HEADS + hh)
        vc = conv_block(2 * DN_HEADS + hh)
        q = qc * (lax.rsqrt(jnp.sum(qc * qc, axis=-1, keepdims=True) + EPS) * (DN_HEAD_DIM ** -0.5))
        k = kc * lax.rsqrt(jnp.sum(kc * kc, axis=-1, keepdims=True) + EPS)
        kb = k * beta_b
        k16 = k.astype(BF16)
        diff = colb(gc_all, hh, rows) - gc_t[hh:hh + 1, :]
        decay = jnp.where(m_incl, jnp.exp(jnp.where(m_incl, diff, 0.0)), 0.0)
        a = jnp.where(m_strict, _dot_nt(kb.astype(BF16), k16) * decay, 0.0)
        qk_s[hh] = _dot_nt(q.astype(BF16), k16) * decay
        tinv = _unit_lower_inverse(a, row, col, log2c)
        rhs = jnp.concatenate([vc * beta_b, kb * egc_b], axis=1).astype(BF16)
        sol = _dot(tinv.astype(BF16), rhs)
        u_s[hh] = sol[:, :DN_HEAD_DIM]
        w_s[hh] = sol[:, DN_HEAD_DIM:]
        qg_s[hh] = q * egc_b
        kdt_s[hh] = (k * jnp.exp(gl_b - gc_b)).T
        egl_s[hh] = jnp.exp(gl_b)

    vn_s[...] = jnp.zeros(vn_s.shape, F32)
    colt = lax.broadcasted_iota(jnp.int32, (DN_HEAD_DIM, rows), 1)

    def chunk_body(n, carry):
        r0 = pl.multiple_of(n * chunk, chunk)
        b = n // chunks_per_row
        in_chunk = (colt >> log2c) == n
        for hh in range(DN_HEADS):
            s = od_ref[b, hh]
            wq = jnp.concatenate([w_s[hh, pl.ds(r0, chunk), :], qg_s[hh, pl.ds(r0, chunk), :]],
                                 axis=0).astype(BF16)
            res = _dot(wq, s.astype(BF16))
            vn_s[hh, pl.ds(r0, chunk), :] = u_s[hh, pl.ds(r0, chunk), :] - res[:chunk]
            vn16 = vn_s[hh].astype(BF16)
            o_s[pl.ds(r0, chunk), hh * DN_HEAD_DIM:(hh + 1) * DN_HEAD_DIM] = (
                res[chunk:] + _dot(qk_s[hh, pl.ds(r0, chunk), :].astype(BF16), vn16))
            kdt = jnp.where(in_chunk, kdt_s[hh], 0.0).astype(BF16)
            od_ref[b, hh] = s * egl_s[hh, pl.ds(r0, 1), :] + _dot(kdt, vn16)
        return carry

    lax.fori_loop(0, n_chunks, chunk_body, 0)

    for hh in range(DN_HEADS):
        hs = slice(hh * DN_HEAD_DIM, (hh + 1) * DN_HEAD_DIM)
        z = proj_s[:, C_Z + hh * DN_HEAD_DIM:C_Z + (hh + 1) * DN_HEAD_DIM]
        cat_s[:, hs] = (_rms_scale(o_s[:, hs]) * onorm_ref[...] * _silu(z)).astype(BF16)

    ph = POOL_HIST
    _carry_history(extp_s, sp_ref, j, ph, POOL_BUF, length)
    extp_s[:, ph:ph + length, :] = proj_s[:, C_P:C_P + POOL_WIDTH].reshape(nb, length, POOL_WIDTH)
    op_ref[...] = extp_s[:, ph + length - POOL_BUF:ph + length, :]
    zero8 = jnp.zeros((nb, SUBLANES, POOL_WIDTH), F32)
    end = ph + length
    s2_s[:, 0:8, :] = zero8
    s4_s[:, 0:8, :] = zero8
    s8_s[:, 0:8, :] = zero8
    s2_s[:, 8:end, :] = extp_s[:, 8:end, :] + extp_s[:, 7:end - 1, :]
    s4_s[:, 8:end, :] = s2_s[:, 8:end, :] + s2_s[:, 6:end - 2, :]
    s8_s[:, 8:end, :] = s4_s[:, 8:end, :] + s4_s[:, 4:end - 4, :]
    s16 = s8_s[:, ph:end, :] + s8_s[:, ph - 8:end - 8, :]
    lane3 = lax.broadcasted_iota(jnp.int32, (nb, length, POOL_WIDTH), 2)
    grp = lane3 // POOL_GROUP_DIM
    wsum = jnp.where(grp == 0, s2_s[:, ph:end, :],
                     jnp.where(grp == 1, s4_s[:, ph:end, :],
                               jnp.where(grp == 2, s8_s[:, ph:end, :], s16)))
    win = jnp.where(grp == 0, POOL_WINDOWS[0],
                    jnp.where(grp == 1, POOL_WINDOWS[1],
                              jnp.where(grp == 2, POOL_WINDOWS[2], POOL_WINDOWS[3])))
    tpos = lax.broadcasted_iota(jnp.int32, (nb, length, POOL_WIDTH), 1) + (pos0 + j * length)
    cnt = jnp.minimum(tpos + 1, win).astype(F32)
    dpool = (wsum / cnt - extp_s[:, ph:end, :]).reshape(rows, POOL_WIDTH)
    cat_s[:, DN_WIDTH:DN_WIDTH + POOL_WIDTH] = (
        _dot(dpool.astype(BF16), pw_ref[...]) * pscale_ref[...]).astype(BF16)

    _carry_history(exts_s, ssc_ref, j, CONV_HIST, SC_CONV - 1, length)
    exts_s[:, CONV_HIST:CONV_HIST + length, :] = (
        proj_s[:, C_SC:C_SC + SC_WIDTH] * proj_s[:, C_SX:C_SX + SC_WIDTH]).reshape(nb, length, SC_WIDTH)
    osc_ref[...] = exts_s[:, CONV_HIST + length - (SC_CONV - 1):CONV_HIST + length, :]
    sbase = CONV_HIST - (SC_CONV - 1)
    cconv = exts_s[:, sbase:sbase + length, :] * scw_ref[0:1, :][None]
    for i in range(1, SC_CONV):
        cconv = cconv + exts_s[:, sbase + i:sbase + i + length, :] * scw_ref[i:i + 1, :][None]
    cat_s[:, DN_WIDTH + POOL_WIDTH:] = (
        proj_s[:, C_SB:C_SB + SC_WIDTH] * cconv.reshape(rows, SC_WIDTH)).astype(BF16)

    xo_ref[...] = (x + _dot(cat_s[...], wout_ref[...])).reshape(nb, length, D_MODEL)


def _ffn_kernel(x_ref, sf_ref, nffn_ref, wg_ref, fw_ref, wu_ref, wd_ref, fnorm_ref,
                xo_ref, of_ref, extf_s, *, nb, length, final):
    j = pl.program_id(1)
    rows = nb * length
    x = x_ref[...].reshape(rows, D_MODEL)
    h16 = (_rms_scale(x) * nffn_ref[...]).astype(BF16)
    _carry_history(extf_s, sf_ref, j, CONV_HIST, FFN_CONV - 1, length)
    extf_s[:, CONV_HIST:CONV_HIST + length, :] = _dot(h16, wg_ref[...]).reshape(nb, length, D_FF)
    of_ref[...] = extf_s[:, CONV_HIST + length - (FFN_CONV - 1):CONV_HIST + length, :]
    base = CONV_HIST - (FFN_CONV - 1)
    gate = extf_s[:, base:base + length, :] * fw_ref[0:1, :][None]
    for i in range(1, FFN_CONV):
        gate = gate + extf_s[:, base + i:base + i + length, :] * fw_ref[i:i + 1, :][None]
    up = _dot(h16, wu_ref[...])
    act = (_silu(gate).reshape(rows, D_FF) * up).astype(BF16)
    y = x + _dot(act, wd_ref[...])
    if final:
        y = _rms_scale(y) * fnorm_ref[...]
    xo_ref[...] = y.reshape(nb, length, D_MODEL)


def _const_spec(shape):
    zeros = (0,) * len(shape)
    return pl.BlockSpec(shape, lambda i, j: zeros)


def _state_spec(arr, layer, nb):
    tail = arr.shape[2:]
    zeros = (0,) * len(tail)
    return pl.BlockSpec((None, nb) + tail, lambda i, j: (layer, i) + zeros)


def _state_out_spec(shape, nb):
    tail = shape[1:]
    zeros = (0,) * len(tail)
    return pl.BlockSpec((nb,) + tail, lambda i, j: (i,) + zeros)


def _mixer_call(x, states, layer, w, *, nb, length, chunk, pos0):
    batch, seq, _ = x.shape
    rows = nb * length
    sd, sdc, sp, ssc = states
    grid = (batch // nb, seq // length)
    x_spec = pl.BlockSpec((nb, length, D_MODEL), lambda i, j: (i, j, 0))
    weights = (w["nmix"], w["win"], w["dnw"], w["alog"], w["dtb"], w["onorm"], w["pw"], w["pscale"],
               w["scw"], w["wout"])
    out_shape = (
        jax.ShapeDtypeStruct(x.shape, F32),
        jax.ShapeDtypeStruct((batch,) + sd.shape[2:], F32),
        jax.ShapeDtypeStruct((batch,) + sdc.shape[2:], F32),
        jax.ShapeDtypeStruct((batch,) + sp.shape[2:], F32),
        jax.ShapeDtypeStruct((batch,) + ssc.shape[2:], F32),
    )
    scratch = [
        pltpu.VMEM((rows, D_IN_PAD), F32),
        pltpu.VMEM((nb, CONV_HIST + length, 3 * DN_WIDTH), F32),
        pltpu.VMEM((nb, POOL_HIST + length, POOL_WIDTH), F32),
        pltpu.VMEM((nb, POOL_HIST + length, POOL_WIDTH), F32),
        pltpu.VMEM((nb, POOL_HIST + length, POOL_WIDTH), F32),
        pltpu.VMEM((nb, POOL_HIST + length, POOL_WIDTH), F32),
        pltpu.VMEM((nb, CONV_HIST + length, SC_WIDTH), F32),
        pltpu.VMEM((DN_HEADS, rows, DN_HEAD_DIM), F32),
        pltpu.VMEM((DN_HEADS, rows, DN_HEAD_DIM), F32),
        pltpu.VMEM((DN_HEADS, rows, DN_HEAD_DIM), F32),
        pltpu.VMEM((DN_HEADS, rows, rows), F32),
        pltpu.VMEM((DN_HEADS, DN_HEAD_DIM, rows), F32),
        pltpu.VMEM((DN_HEADS, rows, DN_HEAD_DIM), F32),
        pltpu.VMEM((DN_HEADS, rows, DN_HEAD_DIM), F32),
        pltpu.VMEM((rows, DN_WIDTH), F32),
        pltpu.VMEM((rows, D_MODEL), BF16),
    ]
    kern = functools.partial(_mixer_kernel, nb=nb, length=length, chunk=chunk, pos0=pos0)
    return pl.pallas_call(
        kern,
        grid=grid,
        in_specs=[x_spec, _state_spec(sd, layer, nb), _state_spec(sdc, layer, nb),
                  _state_spec(sp, layer, nb), _state_spec(ssc, layer, nb)]
                 + [_const_spec(a.shape) for a in weights],
        out_specs=(x_spec,) + tuple(_state_out_spec(s.shape, nb) for s in out_shape[1:]),
        out_shape=out_shape,
        scratch_shapes=scratch,
        compiler_params=pltpu.CompilerParams(
            dimension_semantics=("parallel", "arbitrary"), vmem_limit_bytes=VMEM_LIMIT_BYTES),
        name="mixer",
    )(x, sd, sdc, sp, ssc, *weights)


def _ffn_call(x, sf, layer, w, *, nb, length, final):
    batch, seq, _ = x.shape
    grid = (batch // nb, seq // length)
    x_spec = pl.BlockSpec((nb, length, D_MODEL), lambda i, j: (i, j, 0))
    weights = (w["nffn"], w["wg"], w["fw"], w["wu"], w["wd"], w["fnorm"])
    out_shape = (jax.ShapeDtypeStruct(x.shape, F32),
                 jax.ShapeDtypeStruct((batch,) + sf.shape[2:], F32))
    kern = functools.partial(_ffn_kernel, nb=nb, length=length, final=final)
    return pl.pallas_call(
        kern,
        grid=grid,
        in_specs=[x_spec, _state_spec(sf, layer, nb)] + [_const_spec(a.shape) for a in weights],
        out_specs=(x_spec, _state_out_spec(out_shape[1].shape, nb)),
        out_shape=out_shape,
        scratch_shapes=[pltpu.VMEM((nb, CONV_HIST + length, D_FF), F32)],
        compiler_params=pltpu.CompilerParams(
            dimension_semantics=("parallel", "arbitrary"), vmem_limit_bytes=VMEM_LIMIT_BYTES),
        name="ffn",
    )(x, sf, *weights)


def _prepare_layer(l, norm_mix, w_in, dn_conv_w, dn_a_log, dn_dt_bias, dn_out_norm, pool_w, pool_scale,
                   sconv_w, w_out, norm_ffn, w_ffn_gate, ffn_conv_w, w_ffn_up, w_ffn_down, final_norm):
    n_gate = 2 * DN_HEADS
    g0 = C_Z + DN_WIDTH
    wi = w_in[l]
    win = jnp.concatenate(
        [wi[:, :g0], wi[:, g0 + n_gate:], wi[:, g0:g0 + n_gate],
         jnp.zeros((D_MODEL, LANES - n_gate), F32)], axis=1).astype(BF16)
    pad = jnp.zeros((1, LANES - DN_HEADS), F32)
    pw = jnp.zeros((POOL_WIDTH, POOL_WIDTH), F32)
    for g in range(len(POOL_WINDOWS)):
        sl = slice(g * POOL_GROUP_DIM, (g + 1) * POOL_GROUP_DIM)
        pw = pw.at[sl, sl].set(pool_w[l, g])
    return {
        "nmix": norm_mix[l][None, :],
        "win": win,
        "dnw": dn_conv_w[l],
        "alog": jnp.concatenate([dn_a_log[l][None, :], pad], axis=1),
        "dtb": jnp.concatenate([dn_dt_bias[l][None, :], pad], axis=1),
        "onorm": dn_out_norm[l][None, :],
        "pw": pw.astype(BF16),
        "pscale": pool_scale[l][None, :],
        "scw": sconv_w[l],
        "wout": w_out[l].astype(BF16),
        "nffn": norm_ffn[l][None, :],
        "wg": w_ffn_gate[l].astype(BF16),
        "fw": ffn_conv_w[l],
        "wu": w_ffn_up[l].astype(BF16),
        "wd": w_ffn_down[l].astype(BF16),
        "fnorm": final_norm[None, :],
    }


def _trunk(x, states, state_layers, weights, *, nb_mix, nb_ffn, length, chunk, pos0):
    sd, sdc, sp, ssc, sf = states
    depth = len(weights)
    outs = [[] for _ in range(5)]
    for l in range(depth):
        sl = state_layers[l]
        x, o_d, o_dc, o_p, o_sc = _mixer_call(x, (sd, sdc, sp, ssc), sl, weights[l],
                                              nb=nb_mix, length=length, chunk=chunk, pos0=pos0)
        x, o_f = _ffn_call(x, sf, sl, weights[l], nb=nb_ffn, length=length, final=(l == depth - 1))
        for acc, o in zip(outs, (o_d, o_dc, o_p, o_sc, o_f)):
            acc.append(o)
    return x, tuple(jnp.stack(o) for o in outs)


PROMPT_TILE = 256
SAMPLE_MIX_ROWS = 16
SAMPLE_FFN_ROWS = 32


def kernel(x_prompt, x_sample, state_delta, state_delta_conv, state_pool, state_sconv, state_ffn_conv,
           norm_mix, w_in, dn_conv_w, dn_a_log, dn_dt_bias, dn_out_norm, pool_w, pool_scale,
           sconv_w, w_out, norm_ffn, w_ffn_gate, ffn_conv_w, w_ffn_up, w_ffn_down, final_norm):
    depth = w_in.shape[0]
    params = (norm_mix, w_in, dn_conv_w, dn_a_log, dn_dt_bias, dn_out_norm, pool_w, pool_scale,
              sconv_w, w_out, norm_ffn, w_ffn_gate, ffn_conv_w, w_ffn_up, w_ffn_down, final_norm)
    weights = [_prepare_layer(l, *params) for l in range(depth)]

    batch, seq, _ = x_prompt.shape
    dec_batch, dec_seq, _ = x_sample.shape
    sample_states = (state_delta, state_delta_conv, state_pool, state_sconv, state_ffn_conv)
    zero_states = tuple(jnp.zeros((1, batch) + s.shape[2:], F32) for s in sample_states)

    y_prompt, p_st = _trunk(x_prompt, zero_states, (0,) * depth, weights,
                            nb_mix=1, nb_ffn=1, length=min(PROMPT_TILE, seq),
                            chunk=min(DN_CHUNK, seq), pos0=0)
    y_sample, s_st = _trunk(x_sample, sample_states, tuple(range(depth)), weights,
                            nb_mix=SAMPLE_MIX_ROWS, nb_ffn=SAMPLE_FFN_ROWS, length=dec_seq,
                            chunk=min(DN_CHUNK, dec_seq), pos0=PAST_LEN)
    return (y_prompt, y_sample) + p_st + s_st
```

```python
import functools
import math

import jax
import jax.numpy as jnp
from jax import lax
from jax.experimental import pallas as pl
from jax.experimental.pallas import tpu as pltpu

F32 = jnp.float32
BF16 = jnp.bfloat16

LANES = 128
SUBLANES = 8
VMEM_LIMIT_BYTES = 56 * 1024 * 1024

D_MODEL = 1024
DN_HEADS = 4
DN_HEAD_DIM = 128
DN_WIDTH = DN_HEADS * DN_HEAD_DIM
DN_CONV = 4
DN_CHUNK = 64
POOL_WINDOWS = (2, 4, 8, 16)
POOL_WIDTH = 256
POOL_GROUP_DIM = POOL_WIDTH // len(POOL_WINDOWS)
POOL_BUF = max(POOL_WINDOWS) - 1
SC_WIDTH = 256
SC_CONV = 3
D_FF = 2816
FFN_CONV = 3
EPS = 1e-6
PAST_LEN = 16384

C_QKV = 0
C_Z = 3 * DN_WIDTH
C_P = C_Z + DN_WIDTH
C_SX = C_P + POOL_WIDTH
C_SB = C_SX + SC_WIDTH
C_SC = C_SB + SC_WIDTH
C_G = C_SC + SC_WIDTH
D_IN_PAD = C_G + LANES

CONV_HIST = SUBLANES
POOL_HIST = 24


def _silu(t):
    return t / (1.0 + jnp.exp(-t))


def _softplus(t):
    return jnp.maximum(t, 0.0) + jnp.log1p(jnp.exp(-jnp.abs(t)))


def _rms_scale(t):
    return t * lax.rsqrt(jnp.mean(t * t, axis=-1, keepdims=True) + EPS)


def _dot(a, b):
    return jnp.dot(a, b, preferred_element_type=F32)


def _dot_nt(a, b):
    return lax.dot_general(a, b, (((1,), (1,)), ((), ())), preferred_element_type=F32)


def _carry_history(ext, state_ref, j, hist, nbuf, length):
    @pl.when(j == 0)
    def _():
        ext[:, 0:hist - nbuf, :] = jnp.zeros((ext.shape[0], hist - nbuf, ext.shape[2]), F32)
        ext[:, hist - nbuf:hist, :] = state_ref[...]

    @pl.when(j != 0)
    def _():
        ext[:, hist - nbuf:hist, :] = ext[:, hist + length - nbuf:hist + length, :]


def _unit_lower_inverses(a_list, row, col, log2c):
    n = len(a_list)
    eye = jnp.where(row == col, 1.0, 0.0).astype(F32)
    pair = ((row >> 1) == (col >> 1)) & ((row & 1) == 1) & ((col & 1) == 0)
    a16 = [a.astype(BF16) for a in a_list]
    t = [eye - jnp.where(pair, a, 0.0) for a in a_list]
    for lvl in range(1, log2c):
        off = (((row >> (lvl + 1)) == (col >> (lvl + 1)))
               & (((row >> lvl) & 1) == 1) & (((col >> lvl) & 1) == 0))
        t16 = [x.astype(BF16) for x in t]
        p16 = [_dot(a16[i], t16[i]).astype(BF16) for i in range(n)]
        t = [t[i] - jnp.where(off, _dot(t16[i], p16[i]), 0.0) for i in range(n)]
    return t


L_GC, L_BETA, L_EGC, L_KDF, L_EGL = 0, DN_HEADS, SUBLANES, 2 * SUBLANES, 3 * SUBLANES


def _mixer_kernel(x_ref, sd_ref, sdc_ref, sp_ref, ssc_ref,
                  nmix_ref, win_ref, dnw_ref, alog_ref, dtb_ref, onorm_ref, pw_ref, pscale_ref,
                  scw_ref, wout_ref,
                  xo_ref, od_ref, odc_ref, op_ref, osc_ref,
                  proj_s, extd_s, extp_s, s2_s, s4_s, s8_s, exts_s,
                  u_s, w_s, qg_s, qk_s, kdt_s, egl_s, vn_s, rhs_s, o_s, cat_s,
                  *, nb, length, chunk, pos0):
    j = pl.program_id(1)
    rows = nb * length
    log2c = int(math.log2(chunk))
    chunks_per_row = length // chunk
    n_chunks = rows // chunk

    x = x_ref[...].reshape(rows, D_MODEL)
    h16 = (_rms_scale(x) * nmix_ref[...]).astype(BF16)
    proj_s[...] = _dot(h16, win_ref[...])

    @pl.when(j == 0)
    def _():
        od_ref[...] = sd_ref[...]

    _carry_history(extd_s, sdc_ref, j, CONV_HIST, DN_CONV - 1, length)
    extd_s[:, CONV_HIST:CONV_HIST + length, :] = proj_s[:, C_QKV:C_QKV + 3 * DN_WIDTH].reshape(
        nb, length, 3 * DN_WIDTH)
    odc_ref[...] = extd_s[:, CONV_HIST + length - (DN_CONV - 1):CONV_HIST + length, :]

    ph = POOL_HIST
    _carry_history(extp_s, sp_ref, j, ph, POOL_BUF, length)
    extp_s[:, ph:ph + length, :] = proj_s[:, C_P:C_P + POOL_WIDTH].reshape(nb, length, POOL_WIDTH)
    op_ref[...] = extp_s[:, ph + length - POOL_BUF:ph + length, :]
    zero8 = jnp.zeros((nb, SUBLANES, POOL_WIDTH), F32)
    end = ph + length
    s2_s[:, 0:8, :] = zero8
    s4_s[:, 0:8, :] = zero8
    s8_s[:, 0:8, :] = zero8
    s2_s[:, 8:end, :] = extp_s[:, 8:end, :] + extp_s[:, 7:end - 1, :]
    s4_s[:, 8:end, :] = s2_s[:, 8:end, :] + s2_s[:, 6:end - 2, :]
    s8_s[:, 8:end, :] = s4_s[:, 8:end, :] + s4_s[:, 4:end - 4, :]
    s16 = s8_s[:, ph:end, :] + s8_s[:, ph - 8:end - 8, :]
    lane3 = lax.broadcasted_iota(jnp.int32, (nb, length, POOL_WIDTH), 2)
    grp = lane3 // POOL_GROUP_DIM
    wsum = jnp.where(grp == 0, s2_s[:, ph:end, :],
                     jnp.where(grp == 1, s4_s[:, ph:end, :],
                               jnp.where(grp == 2, s8_s[:, ph:end, :], s16)))
    win = jnp.where(grp == 0, POOL_WINDOWS[0],
                    jnp.where(grp == 1, POOL_WINDOWS[1],
                              jnp.where(grp == 2, POOL_WINDOWS[2], POOL_WINDOWS[3])))
    tpos = lax.broadcasted_iota(jnp.int32, (nb, length, POOL_WIDTH), 1) + (pos0 + j * length)
    cnt = jnp.minimum(tpos + 1, win).astype(F32)
    dpool = (wsum / cnt - extp_s[:, ph:end, :]).reshape(rows, POOL_WIDTH)
    cat_s[:, DN_WIDTH:DN_WIDTH + POOL_WIDTH] = (
        _dot(dpool.astype(BF16), pw_ref[...]) * pscale_ref[...]).astype(BF16)

    _carry_history(exts_s, ssc_ref, j, CONV_HIST, SC_CONV - 1, length)
    exts_s[:, CONV_HIST:CONV_HIST + length, :] = (
        proj_s[:, C_SC:C_SC + SC_WIDTH] * proj_s[:, C_SX:C_SX + SC_WIDTH]).reshape(nb, length, SC_WIDTH)
    osc_ref[...] = exts_s[:, CONV_HIST + length - (SC_CONV - 1):CONV_HIST + length, :]
    sbase = CONV_HIST - (SC_CONV - 1)
    cconv = exts_s[:, sbase:sbase + length, :] * scw_ref[0:1, :][None]
    for i in range(1, SC_CONV):
        cconv = cconv + exts_s[:, sbase + i:sbase + i + length, :] * scw_ref[i:i + 1, :][None]
    cat_s[:, DN_WIDTH + POOL_WIDTH:] = (
        proj_s[:, C_SB:C_SB + SC_WIDTH] * cconv.reshape(rows, SC_WIDTH)).astype(BF16)

    gates_t = proj_s[:, C_G:C_G + LANES].T[0:SUBLANES]
    g_t = -jnp.exp(alog_ref[...]) * _softplus(gates_t + dtb_ref[...])
    beta_t = 1.0 / (1.0 + jnp.exp(-gates_t))
    pos_in_chunk = lax.broadcasted_iota(jnp.int32, (SUBLANES, rows), 1) & (chunk - 1)
    gc_t = g_t
    sfx_t = g_t
    step = 1
    while step < chunk:
        gc_t = gc_t + jnp.where(pos_in_chunk >= step, pltpu.roll(gc_t, step, axis=1), 0.0)
        sfx_t = sfx_t + jnp.where(pos_in_chunk + step < chunk,
                                  pltpu.roll(sfx_t, rows - step, axis=1), 0.0)
        step *= 2
    rest_t = sfx_t - g_t
    sub = lax.broadcasted_iota(jnp.int32, (SUBLANES, rows), 0)
    cols = jnp.concatenate(
        [jnp.where(sub < DN_HEADS, gc_t, beta_t), jnp.exp(gc_t), jnp.exp(rest_t), jnp.exp(gc_t + rest_t),
         jnp.zeros((LANES - 4 * SUBLANES, rows), F32)], axis=0).T

    def colb(lane, width=LANES):
        return jnp.broadcast_to(cols[:, lane:lane + 1], (rows, width))

    def conv_block(ci):
        cs = slice(ci * LANES, (ci + 1) * LANES)
        base = CONV_HIST - (DN_CONV - 1)
        acc = extd_s[:, base:base + length, cs] * dnw_ref[0:1, cs][None]
        for i in range(1, DN_CONV):
            acc = acc + extd_s[:, base + i:base + i + length, cs] * dnw_ref[i:i + 1, cs][None]
        return _silu(acc).reshape(rows, LANES)

    row = lax.broadcasted_iota(jnp.int32, (rows, rows), 0)
    col = lax.broadcasted_iota(jnp.int32, (rows, rows), 1)
    same_chunk = (row >> log2c) == (col >> log2c)
    m_incl = same_chunk & (row >= col)
    m_strict = same_chunk & (row > col)

    a_list = []
    for hh in range(DN_HEADS):
        beta_b = colb(L_BETA + hh)
        egc_b = colb(L_EGC + hh)
        qc = conv_block(hh)
        kc = conv_block(DN_HEADS + hh)
        vc = conv_block(2 * DN_HEADS + hh)
        q = qc * (lax.rsqrt(jnp.sum(qc * qc, axis=-1, keepdims=True) + EPS) * (DN_HEAD_DIM ** -0.5))
        k = kc * lax.rsqrt(jnp.sum(kc * kc, axis=-1, keepdims=True) + EPS)
        kb = k * beta_b
        k16 = k.astype(BF16)
        diff = colb(L_GC + hh, rows) - gc_t[hh:hh + 1, :]
        decay = jnp.where(m_incl, jnp.exp(jnp.where(m_incl, diff, 0.0)), 0.0)
        a_list.append(jnp.where(m_strict, _dot_nt(kb.astype(BF16), k16) * decay, 0.0))
        qk_s[hh] = _dot_nt(q.astype(BF16), k16) * decay
        rhs_s[hh] = jnp.concatenate([vc * beta_b, kb * egc_b], axis=1).astype(BF16)
        qg_s[hh] = q * egc_b
        kdt_s[hh] = (k * colb(L_KDF + hh)).T
        egl_s[hh] = colb(L_EGL + hh)

    tinv = _unit_lower_inverses(a_list, row, col, log2c)
    for hh in range(DN_HEADS):
        sol = _dot(tinv[hh].astype(BF16), rhs_s[hh])
        u_s[hh] = sol[:, :DN_HEAD_DIM]
        w_s[hh] = sol[:, DN_HEAD_DIM:]

    def head_slice(hh):
        return slice(hh * DN_HEAD_DIM, (hh + 1) * DN_HEAD_DIM)

    if n_chunks == 1:
        for hh in range(DN_HEADS):
            s = od_ref[0, hh]
            wq = jnp.concatenate([w_s[hh], qg_s[hh]], axis=0).astype(BF16)
            res = _dot(wq, s.astype(BF16))
            vn16 = (u_s[hh] - res[:rows]).astype(BF16)
            o_s[:, head_slice(hh)] = res[rows:] + _dot(qk_s[hh].astype(BF16), vn16)
            od_ref[0, hh] = s * egl_s[hh, 0:1, :] + _dot(kdt_s[hh].astype(BF16), vn16)
    else:
        vn_s[...] = jnp.zeros(vn_s.shape, F32)
        colt = lax.broadcasted_iota(jnp.int32, (DN_HEAD_DIM, rows), 1)

        def chunk_body(n, carry):
            r0 = pl.multiple_of(n * chunk, chunk)
            b = n // chunks_per_row
            in_chunk = (colt >> log2c) == n
            for hh in range(DN_HEADS):
                s = od_ref[b, hh]
                wq = jnp.concatenate([w_s[hh, pl.ds(r0, chunk), :], qg_s[hh, pl.ds(r0, chunk), :]],
                                     axis=0).astype(BF16)
                res = _dot(wq, s.astype(BF16))
                vn_s[hh, pl.ds(r0, chunk), :] = u_s[hh, pl.ds(r0, chunk), :] - res[:chunk]
                vn16 = vn_s[hh].astype(BF16)
                o_s[pl.ds(r0, chunk), head_slice(hh)] = (
                    res[chunk:] + _dot(qk_s[hh, pl.ds(r0, chunk), :].astype(BF16), vn16))
                kdt = jnp.where(in_chunk, kdt_s[hh], 0.0).astype(BF16)
                od_ref[b, hh] = s * egl_s[hh, pl.ds(r0, 1), :] + _dot(kdt, vn16)
            return carry

        lax.fori_loop(0, n_chunks, chunk_body, 0)

    for hh in range(DN_HEADS):
        hs = head_slice(hh)
        z = proj_s[:, C_Z + hh * DN_HEAD_DIM:C_Z + (hh + 1) * DN_HEAD_DIM]
        cat_s[:, hs] = (_rms_scale(o_s[:, hs]) * onorm_ref[...] * _silu(z)).astype(BF16)

    xo_ref[...] = (x + _dot(cat_s[...], wout_ref[...])).reshape(nb, length, D_MODEL)


def _ffn_kernel(x_ref, sf_ref, nffn_ref, wg_ref, fw_ref, wu_ref, wd_ref, fnorm_ref,
                xo_ref, of_ref, extf_s, *, nb, length, final):
    j = pl.program_id(1)
    rows = nb * length
    x = x_ref[...].reshape(rows, D_MODEL)
    h16 = (_rms_scale(x) * nffn_ref[...]).astype(BF16)
    _carry_history(extf_s, sf_ref, j, CONV_HIST, FFN_CONV - 1, length)
    extf_s[:, CONV_HIST:CONV_HIST + length, :] = _dot(h16, wg_ref[...]).reshape(nb, length, D_FF)
    of_ref[...] = extf_s[:, CONV_HIST + length - (FFN_CONV - 1):CONV_HIST + length, :]
    base = CONV_HIST - (FFN_CONV - 1)
    gate = extf_s[:, base:base + length, :] * fw_ref[0:1, :][None]
    for i in range(1, FFN_CONV):
        gate = gate + extf_s[:, base + i:base + i + length, :] * fw_ref[i:i + 1, :][None]
    up = _dot(h16, wu_ref[...])
    act = (_silu(gate).reshape(rows, D_FF) * up).astype(BF16)
    y = x + _dot(act, wd_ref[...])
    if final:
        y = _rms_scale(y) * fnorm_ref[...]
    xo_ref[...] = y.reshape(nb, length, D_MODEL)


def _const_spec(shape):
    zeros = (0,) * len(shape)
    return pl.BlockSpec(shape, lambda i, j: zeros)


def _state_spec(arr, layer, nb):
    tail = arr.shape[2:]
    zeros = (0,) * len(tail)
    return pl.BlockSpec((None, nb) + tail, lambda i, j: (layer, i) + zeros)


def _state_out_spec(shape, nb):
    tail = shape[1:]
    zeros = (0,) * len(tail)
    return pl.BlockSpec((nb,) + tail, lambda i, j: (i,) + zeros)


def _mixer_call(x, states, layer, w, *, nb, length, chunk, pos0):
    batch, seq, _ = x.shape
    rows = nb * length
    sd, sdc, sp, ssc = states
    grid = (batch // nb, seq // length)
    x_spec = pl.BlockSpec((nb, length, D_MODEL), lambda i, j: (i, j, 0))
    weights = (w["nmix"], w["win"], w["dnw"], w["alog"], w["dtb"], w["onorm"], w["pw"], w["pscale"],
               w["scw"], w["wout"])
    out_shape = (
        jax.ShapeDtypeStruct(x.shape, F32),
        jax.ShapeDtypeStruct((batch,) + sd.shape[2:], F32),
        jax.ShapeDtypeStruct((batch,) + sdc.shape[2:], F32),
        jax.ShapeDtypeStruct((batch,) + sp.shape[2:], F32),
        jax.ShapeDtypeStruct((batch,) + ssc.shape[2:], F32),
    )
    scratch = [
        pltpu.VMEM((rows, D_IN_PAD), F32),
        pltpu.VMEM((nb, CONV_HIST + length, 3 * DN_WIDTH), F32),
        pltpu.VMEM((nb, POOL_HIST + length, POOL_WIDTH), F32),
        pltpu.VMEM((nb, POOL_HIST + length, POOL_WIDTH), F32),
        pltpu.VMEM((nb, POOL_HIST + length, POOL_WIDTH), F32),
        pltpu.VMEM((nb, POOL_HIST + length, POOL_WIDTH), F32),
        pltpu.VMEM((nb, CONV_HIST + length, SC_WIDTH), F32),
        pltpu.VMEM((DN_HEADS, rows, DN_HEAD_DIM), F32),
        pltpu.VMEM((DN_HEADS, rows, DN_HEAD_DIM), F32),
        pltpu.VMEM((DN_HEADS, rows, DN_HEAD_DIM), F32),
        pltpu.VMEM((DN_HEADS, rows, rows), F32),
        pltpu.VMEM((DN_HEADS, DN_HEAD_DIM, rows), F32),
        pltpu.VMEM((DN_HEADS, rows, DN_HEAD_DIM), F32),
        pltpu.VMEM((DN_HEADS, rows, DN_HEAD_DIM), F32),
        pltpu.VMEM((DN_HEADS, rows, 2 * DN_HEAD_DIM), BF16),
        pltpu.VMEM((rows, DN_WIDTH), F32),
        pltpu.VMEM((rows, D_MODEL), BF16),
    ]
    kern = functools.partial(_mixer_kernel, nb=nb, length=length, chunk=chunk, pos0=pos0)
    return pl.pallas_call(
        kern,
        grid=grid,
        in_specs=[x_spec, _state_spec(sd, layer, nb), _state_spec(sdc, layer, nb),
                  _state_spec(sp, layer, nb), _state_spec(ssc, layer, nb)]
                 + [_const_spec(a.shape) for a in weights],
        out_specs=(x_spec,) + tuple(_state_out_spec(s.shape, nb) for s in out_shape[1:]),
        out_shape=out_shape,
        scratch_shapes=scratch,
        compiler_params=pltpu.CompilerParams(
            dimension_semantics=("parallel", "arbitrary"), vmem_limit_bytes=VMEM_LIMIT_BYTES),
        name="mixer",
    )(x, sd, sdc, sp, ssc, *weights)


def _ffn_call(x, sf, layer, w, *, nb, length, final):
    batch, seq, _ = x.shape
    grid = (batch // nb, seq // length)
    x_spec = pl.BlockSpec((nb, length, D_MODEL), lambda i, j: (i, j, 0))
    weights = (w["nffn"], w["wg"], w["fw"], w["wu"], w["wd"], w["fnorm"])
    out_shape = (jax.ShapeDtypeStruct(x.shape, F32),
                 jax.ShapeDtypeStruct((batch,) + sf.shape[2:], F32))
    kern = functools.partial(_ffn_kernel, nb=nb, length=length, final=final)
    return pl.pallas_call(
        kern,
        grid=grid,
        in_specs=[x_spec, _state_spec(sf, layer, nb)] + [_const_spec(a.shape) for a in weights],
        out_specs=(x_spec, _state_out_spec(out_shape[1].shape, nb)),
        out_shape=out_shape,
        scratch_shapes=[pltpu.VMEM((nb, CONV_HIST + length, D_FF), F32)],
        compiler_params=pltpu.CompilerParams(
            dimension_semantics=("parallel", "arbitrary"), vmem_limit_bytes=VMEM_LIMIT_BYTES),
        name="ffn",
    )(x, sf, *weights)


def _prepare_layer(l, norm_mix, w_in, dn_conv_w, dn_a_log, dn_dt_bias, dn_out_norm, pool_w, pool_scale,
                   sconv_w, w_out, norm_ffn, w_ffn_gate, ffn_conv_w, w_ffn_up, w_ffn_down, final_norm):
    n_gate = 2 * DN_HEADS
    g0 = C_Z + DN_WIDTH
    wi = w_in[l]
    win = jnp.concatenate(
        [wi[:, :g0], wi[:, g0 + n_gate:], wi[:, g0:g0 + n_gate],
         jnp.zeros((D_MODEL, LANES - n_gate), F32)], axis=1).astype(BF16)
    pad = jnp.zeros((SUBLANES - DN_HEADS, 1), F32)
    pw = jnp.zeros((POOL_WIDTH, POOL_WIDTH), F32)
    for g in range(len(POOL_WINDOWS)):
        sl = slice(g * POOL_GROUP_DIM, (g + 1) * POOL_GROUP_DIM)
        pw = pw.at[sl, sl].set(pool_w[l, g])
    return {
        "nmix": norm_mix[l][None, :],
        "win": win,
        "dnw": dn_conv_w[l],
        "alog": jnp.concatenate([dn_a_log[l][:, None], pad], axis=0),
        "dtb": jnp.concatenate([dn_dt_bias[l][:, None], pad], axis=0),
        "onorm": dn_out_norm[l][None, :],
        "pw": pw.astype(BF16),
        "pscale": pool_scale[l][None, :],
        "scw": sconv_w[l],
        "wout": w_out[l].astype(BF16),
        "nffn": norm_ffn[l][None, :],
        "wg": w_ffn_gate[l].astype(BF16),
        "fw": ffn_conv_w[l],
        "wu": w_ffn_up[l].astype(BF16),
        "wd": w_ffn_down[l].astype(BF16),
        "fnorm": final_norm[None, :],
    }


def _trunk(x, states, state_layers, weights, *, nb_mix, nb_ffn, length, chunk, pos0):
    sd, sdc, sp, ssc, sf = states
    depth = len(weights)
    outs = [[] for _ in range(5)]
    for l in range(depth):
        sl = state_layers[l]
        x, o_d, o_dc, o_p, o_sc = _mixer_call(x, (sd, sdc, sp, ssc), sl, weights[l],
                                              nb=nb_mix, length=length, chunk=chunk, pos0=pos0)
        x, o_f = _ffn_call(x, sf, sl, weights[l], nb=nb_ffn, length=length, final=(l == depth - 1))
        for acc, o in zip(outs, (o_d, o_dc, o_p, o_sc, o_f)):
            acc.append(o)
    return x, tuple(jnp.stack(o) for o in outs)


PROMPT_TILE = 256
SAMPLE_MIX_ROWS = 16
SAMPLE_FFN_ROWS = 32


def kernel(x_prompt, x_sample, state_delta, state_delta_conv, state_pool, state_sconv, state_ffn_conv,
           norm_mix, w_in, dn_conv_w, dn_a_log, dn_dt_bias, dn_out_norm, pool_w, pool_scale,
           sconv_w, w_out, norm_ffn, w_ffn_gate, ffn_conv_w, w_ffn_up, w_ffn_down, final_norm):
    depth = w_in.shape[0]
    params = (norm_mix, w_in, dn_conv_w, dn_a_log, dn_dt_bias, dn_out_norm, pool_w, pool_scale,
              sconv_w, w_out, norm_ffn, w_ffn_gate, ffn_conv_w, w_ffn_up, w_ffn_down, final_norm)
    weights = [_prepare_layer(l, *params) for l in range(depth)]

    batch, seq, _ = x_prompt.shape
    dec_batch, dec_seq, _ = x_sample.shape
    sample_states = (state_delta, state_delta_conv, state_pool, state_sconv, state_ffn_conv)
    zero_states = tuple(jnp.zeros((1, batch) + s.shape[2:], F32) for s in sample_states)

    y_prompt, p_st = _trunk(x_prompt, zero_states, (0,) * depth, weights,
                            nb_mix=1, nb_ffn=1, length=min(PROMPT_TILE, seq),
                            chunk=min(PROMPT_TILE, seq), pos0=0)
    y_sample, s_st = _trunk(x_sample, sample_states, tuple(range(depth)), weights,
                            nb_mix=SAMPLE_MIX_ROWS, nb_ffn=SAMPLE_FFN_ROWS, length=dec_seq,
                            chunk=min(DN_CHUNK, dec_seq), pos0=PAST_LEN)
    return (y_prompt, y_sample) + p_st + s_st
```

```python
import functools
import math

import jax
import jax.numpy as jnp
from jax import lax
from jax.experimental import pallas as pl
from jax.experimental.pallas import tpu as pltpu

F32 = jnp.float32
BF16 = jnp.bfloat16

LANES = 128
SUBLANES = 8
VMEM_LIMIT_BYTES = 56 * 1024 * 1024

D_MODEL = 1024
DN_HEADS = 4
DN_HEAD_DIM = 128
DN_WIDTH = DN_HEADS * DN_HEAD_DIM
DN_CONV = 4
DN_CHUNK = 64
POOL_WINDOWS = (2, 4, 8, 16)
POOL_WIDTH = 256
POOL_GROUP_DIM = POOL_WIDTH // len(POOL_WINDOWS)
POOL_BUF = max(POOL_WINDOWS) - 1
SC_WIDTH = 256
SC_CONV = 3
D_FF = 2816
FFN_CONV = 3
EPS = 1e-6
PAST_LEN = 16384

HEAD_GROUP = 2
N_GATE = 2 * DN_HEADS
W_QKV = 0
QKV_GROUP = 3 * HEAD_GROUP * DN_HEAD_DIM
W_REST = W_QKV + (DN_HEADS // HEAD_GROUP) * QKV_GROUP
REST_WIDTH = POOL_WIDTH + 3 * SC_WIDTH
W_Z = W_REST + REST_WIDTH
D_IN_PAD = W_Z + DN_WIDTH

CONV_HIST = SUBLANES
POOL_HIST = 24


def _silu(t):
    return t / (1.0 + jnp.exp(-t))


def _softplus(t):
    return jnp.maximum(t, 0.0) + jnp.log1p(jnp.exp(-jnp.abs(t)))


def _rms_scale(t):
    return t * lax.rsqrt(jnp.mean(t * t, axis=-1, keepdims=True) + EPS)


def _dot(a, b):
    return jnp.dot(a, b, preferred_element_type=F32)


def _dot_nt(a, b):
    return lax.dot_general(a, b, (((1,), (1,)), ((), ())), preferred_element_type=F32)


def _carry_history(j, length, items, first_tile=None):
    @pl.when(j == 0)
    def _():
        if first_tile is not None:
            first_tile()
        for ext, state_ref, hist, nbuf in items:
            ext[:, 0:hist - nbuf, :] = jnp.zeros((ext.shape[0], hist - nbuf, ext.shape[2]), F32)
            ext[:, hist - nbuf:hist, :] = state_ref[...]

    @pl.when(j != 0)
    def _():
        for ext, state_ref, hist, nbuf in items:
            ext[:, hist - nbuf:hist, :] = ext[:, hist + length - nbuf:hist + length, :]


def _unit_lower_inverses(a_list, row, col, log2c):
    n = len(a_list)
    eye = jnp.where(row == col, 1.0, 0.0).astype(F32)
    pair = ((row >> 1) == (col >> 1)) & ((row & 1) == 1) & ((col & 1) == 0)
    a16 = [a.astype(BF16) for a in a_list]
    t = [eye - jnp.where(pair, a, 0.0) for a in a_list]
    for lvl in range(1, log2c):
        off = (((row >> (lvl + 1)) == (col >> (lvl + 1)))
               & (((row >> lvl) & 1) == 1) & (((col >> lvl) & 1) == 0))
        t16 = [x.astype(BF16) for x in t]
        p16 = [_dot(a16[i], t16[i]).astype(BF16) for i in range(n)]
        t = [t[i] - jnp.where(off, _dot(t16[i], p16[i]), 0.0) for i in range(n)]
    return t


L_GC, L_BETA, L_EGC, L_KDF, L_EGL = 0, DN_HEADS, SUBLANES, 2 * SUBLANES, 3 * SUBLANES


def _mixer_kernel(x_ref, sd_ref, sdc_ref, sp_ref, ssc_ref,
                  nmix_ref, wgt_ref, win_ref, dnw_ref, alog_ref, dtb_ref, onorm_ref, pw_ref, pscale_ref,
                  scw_ref, wout_ref,
                  xo_ref, od_ref, odc_ref, op_ref, osc_ref,
                  z_s, extd_s, extp_s, s2_s, s4_s, s8_s, exts_s,
                  u_s, w_s, qg_s, qk_s, kdt_s, egl_s, rhs_s, o_s, cat_s,
                  *, nb, length, chunk, pos0):
    j = pl.program_id(1)
    rows = nb * length
    log2c = int(math.log2(chunk))
    n_chunks = rows // chunk

    def init_state():
        od_ref[...] = sd_ref[...]

    _carry_history(j, length,
                   [(extd_s, sdc_ref, CONV_HIST, DN_CONV - 1), (extp_s, sp_ref, POOL_HIST, POOL_BUF),
                    (exts_s, ssc_ref, CONV_HIST, SC_CONV - 1)], first_tile=init_state)

    x = x_ref[...].reshape(rows, D_MODEL)
    h16 = (_rms_scale(x) * nmix_ref[...]).astype(BF16)

    gates_t = _dot_nt(wgt_ref[...], h16)
    g_t = -jnp.exp(alog_ref[...]) * _softplus(gates_t + dtb_ref[...])
    beta_t = 1.0 / (1.0 + jnp.exp(-gates_t))
    pos_in_chunk = lax.broadcasted_iota(jnp.int32, (SUBLANES, rows), 1) & (chunk - 1)
    gc_t = g_t
    sfx_t = g_t
    step = 1
    while step < chunk:
        gc_t = gc_t + jnp.where(pos_in_chunk >= step, pltpu.roll(gc_t, step, axis=1), 0.0)
        sfx_t = sfx_t + jnp.where(pos_in_chunk + step < chunk,
                                  pltpu.roll(sfx_t, rows - step, axis=1), 0.0)
        step *= 2
    rest_t = sfx_t - g_t
    sub = lax.broadcasted_iota(jnp.int32, (SUBLANES, rows), 0)
    cols = jnp.concatenate(
        [jnp.where(sub < DN_HEADS, gc_t, beta_t), jnp.exp(gc_t), jnp.exp(rest_t), jnp.exp(gc_t + rest_t),
         jnp.zeros((LANES - 4 * SUBLANES, rows), F32)], axis=0).T

    def colb(lane, width=LANES):
        return jnp.broadcast_to(cols[:, lane:lane + 1], (rows, width))

    for grp in range(DN_HEADS // HEAD_GROUP):
        pq = _dot(h16, win_ref[:, W_QKV + grp * QKV_GROUP:W_QKV + (grp + 1) * QKV_GROUP])
        for i in range(HEAD_GROUP):
            for part in range(3):
                src = (3 * i + part) * DN_HEAD_DIM
                dst = (part * DN_HEADS + grp * HEAD_GROUP + i) * DN_HEAD_DIM
                extd_s[:, CONV_HIST:CONV_HIST + length, dst:dst + DN_HEAD_DIM] = (
                    pq[:, src:src + DN_HEAD_DIM].reshape(nb, length, DN_HEAD_DIM))
    odc_ref[...] = extd_s[:, CONV_HIST + length - (DN_CONV - 1):CONV_HIST + length, :]

    def conv_block(ci):
        cs = slice(ci * LANES, (ci + 1) * LANES)
        base = CONV_HIST - (DN_CONV - 1)
        acc = extd_s[:, base:base + length, cs] * dnw_ref[0:1, cs][None]
        for i in range(1, DN_CONV):
            acc = acc + extd_s[:, base + i:base + i + length, cs] * dnw_ref[i:i + 1, cs][None]
        return _silu(acc).reshape(rows, LANES)

    row = lax.broadcasted_iota(jnp.int32, (rows, rows), 0)
    col = lax.broadcasted_iota(jnp.int32, (rows, rows), 1)
    same_chunk = (row >> log2c) == (col >> log2c)
    m_incl = same_chunk & (row >= col)
    m_strict = same_chunk & (row > col)

    a_list = []
    for hh in range(DN_HEADS):
        beta_b = colb(L_BETA + hh)
        egc_b = colb(L_EGC + hh)
        qc = conv_block(hh)
        kc = conv_block(DN_HEADS + hh)
        vc = conv_block(2 * DN_HEADS + hh)
        q = qc * (lax.rsqrt(jnp.sum(qc * qc, axis=-1, keepdims=True) + EPS) * (DN_HEAD_DIM ** -0.5))
        k = kc * lax.rsqrt(jnp.sum(kc * kc, axis=-1, keepdims=True) + EPS)
        kb = k * beta_b
        k16 = k.astype(BF16)
        diff = colb(L_GC + hh, rows) - gc_t[hh:hh + 1, :]
        decay = jnp.where(m_incl, jnp.exp(jnp.where(m_incl, diff, 0.0)), 0.0)
        a_list.append(jnp.where(m_strict, _dot_nt(kb.astype(BF16), k16) * decay, 0.0))
        qk_s[hh] = (_dot_nt(q.astype(BF16), k16) * decay).astype(BF16)
        rhs_s[hh] = jnp.concatenate([vc * beta_b, kb * egc_b], axis=1).astype(BF16)
        qg_s[hh] = q * egc_b
        kdt_s[hh] = (k * colb(L_KDF + hh)).T
        egl_s[hh] = colb(L_EGL + hh)

    prest = _dot(h16, win_ref[:, W_REST:W_REST + REST_WIDTH])
    z_s[...] = _dot(h16, win_ref[:, W_Z:W_Z + DN_WIDTH])
    p_in = prest[:, 0:POOL_WIDTH]
    sc_x = prest[:, POOL_WIDTH:POOL_WIDTH + SC_WIDTH]
    sc_b = prest[:, POOL_WIDTH + SC_WIDTH:POOL_WIDTH + 2 * SC_WIDTH]
    sc_c = prest[:, POOL_WIDTH + 2 * SC_WIDTH:]

    ph = POOL_HIST
    extp_s[:, ph:ph + length, :] = p_in.reshape(nb, length, POOL_WIDTH)
    op_ref[...] = extp_s[:, ph + length - POOL_BUF:ph + length, :]
    zero8 = jnp.zeros((nb, SUBLANES, POOL_WIDTH), F32)
    end = ph + length
    s2_s[:, 0:8, :] = zero8
    s4_s[:, 0:8, :] = zero8
    s8_s[:, 0:8, :] = zero8
    s2_s[:, 8:end, :] = extp_s[:, 8:end, :] + extp_s[:, 7:end - 1, :]
    s4_s[:, 8:end, :] = s2_s[:, 8:end, :] + s2_s[:, 6:end - 2, :]
    s8_s[:, 8:end, :] = s4_s[:, 8:end, :] + s4_s[:, 4:end - 4, :]
    s16 = s8_s[:, ph:end, :] + s8_s[:, ph - 8:end - 8, :]
    lane3 = lax.broadcasted_iota(jnp.int32, (nb, length, POOL_WIDTH), 2)
    grp = lane3 // POOL_GROUP_DIM
    wsum = jnp.where(grp == 0, s2_s[:, ph:end, :],
                     jnp.where(grp == 1, s4_s[:, ph:end, :],
                               jnp.where(grp == 2, s8_s[:, ph:end, :], s16)))
    win = jnp.where(grp == 0, POOL_WINDOWS[0],
                    jnp.where(grp == 1, POOL_WINDOWS[1],
                              jnp.where(grp == 2, POOL_WINDOWS[2], POOL_WINDOWS[3])))
    tpos = lax.broadcasted_iota(jnp.int32, (nb, length, POOL_WIDTH), 1) + (pos0 + j * length)
    cnt = jnp.minimum(tpos + 1, win).astype(F32)
    dpool = (wsum / cnt - extp_s[:, ph:end, :]).reshape(rows, POOL_WIDTH)
    cat_s[:, DN_WIDTH:DN_WIDTH + POOL_WIDTH] = (
        _dot(dpool.astype(BF16), pw_ref[...]) * pscale_ref[...]).astype(BF16)

    exts_s[:, CONV_HIST:CONV_HIST + length, :] = (sc_c * sc_x).reshape(nb, length, SC_WIDTH)
    osc_ref[...] = exts_s[:, CONV_HIST + length - (SC_CONV - 1):CONV_HIST + length, :]
    sbase = CONV_HIST - (SC_CONV - 1)
    cconv = exts_s[:, sbase:sbase + length, :] * scw_ref[0:1, :][None]
    for i in range(1, SC_CONV):
        cconv = cconv + exts_s[:, sbase + i:sbase + i + length, :] * scw_ref[i:i + 1, :][None]
    cat_s[:, DN_WIDTH + POOL_WIDTH:] = (sc_b * cconv.reshape(rows, SC_WIDTH)).astype(BF16)
    y_part = _dot(cat_s[:, DN_WIDTH:], wout_ref[DN_WIDTH:, :])

    tinv = _unit_lower_inverses(a_list, row, col, log2c)
    for hh in range(DN_HEADS):
        sol = _dot(tinv[hh].astype(BF16), rhs_s[hh])
        u_s[hh] = sol[:, :DN_HEAD_DIM]
        w_s[hh] = sol[:, DN_HEAD_DIM:]

    def head_slice(hh):
        return slice(hh * DN_HEAD_DIM, (hh + 1) * DN_HEAD_DIM)

    if n_chunks == 1:
        for hh in range(DN_HEADS):
            s = od_ref[0, hh]
            wq = jnp.concatenate([w_s[hh], qg_s[hh]], axis=0).astype(BF16)
            res = _dot(wq, s.astype(BF16))
            vn16 = (u_s[hh] - res[:rows]).astype(BF16)
            o_s[:, head_slice(hh)] = res[rows:] + _dot(qk_s[hh], vn16)
            od_ref[0, hh] = s * egl_s[hh, 0:1, :] + _dot(kdt_s[hh].astype(BF16), vn16)
    else:
        assert n_chunks == nb
        seg = lax.broadcasted_iota(jnp.int32, (nb, 1, rows), 2) >> log2c
        own = seg == lax.broadcasted_iota(jnp.int32, (nb, 1, rows), 0)
        for hh in range(DN_HEADS):
            vn_parts, qs_parts = [], []
            for b in range(nb):
                rs = slice(b * chunk, (b + 1) * chunk)
                wq = jnp.concatenate([w_s[hh, rs, :], qg_s[hh, rs, :]], axis=0).astype(BF16)
                res = _dot(wq, od_ref[b, hh].astype(BF16))
                vn_parts.append(u_s[hh, rs, :] - res[:chunk])
                qs_parts.append(res[chunk:])
            vn16 = jnp.concatenate(vn_parts, axis=0).astype(BF16)
            o_s[:, head_slice(hh)] = jnp.concatenate(qs_parts, axis=0) + _dot(qk_s[hh], vn16)
            kd_own = jnp.where(own, kdt_s[hh][None], 0.0).astype(BF16).reshape(nb * DN_HEAD_DIM, rows)
            upd = _dot(kd_own, vn16).reshape(nb, DN_HEAD_DIM, DN_HEAD_DIM)
            egl = egl_s[hh].reshape(nb, chunk, DN_HEAD_DIM)[:, 0:1, :]
            od_ref[:, hh, :, :] = od_ref[:, hh, :, :] * egl + upd

    for hh in range(DN_HEADS):
        hs = head_slice(hh)
        cat_s[:, hs] = (_rms_scale(o_s[:, hs]) * onorm_ref[...] * _silu(z_s[:, hs])).astype(BF16)
    y = x + y_part + _dot(cat_s[:, :DN_WIDTH], wout_ref[:DN_WIDTH, :])
    xo_ref[...] = y.reshape(nb, length, D_MODEL)


def _ffn_kernel(x_ref, sf_ref, nffn_ref, wg_ref, fw_ref, wu_ref, wd_ref, fnorm_ref,
                xo_ref, of_ref, extf_s, *, nb, length, final):
    j = pl.program_id(1)
    rows = nb * length
    _carry_history(j, length, [(extf_s, sf_ref, CONV_HIST, FFN_CONV - 1)])
    x = x_ref[...].reshape(rows, D_MODEL)
    h16 = (_rms_scale(x) * nffn_ref[...]).astype(BF16)
    extf_s[:, CONV_HIST:CONV_HIST + length, :] = _dot(h16, wg_ref[...]).reshape(nb, length, D_FF)
    of_ref[...] = extf_s[:, CONV_HIST + length - (FFN_CONV - 1):CONV_HIST + length, :]
    base = CONV_HIST - (FFN_CONV - 1)
    gate = extf_s[:, base:base + length, :] * fw_ref[0:1, :][None]
    for i in range(1, FFN_CONV):
        gate = gate + extf_s[:, base + i:base + i + length, :] * fw_ref[i:i + 1, :][None]
    up = _dot(h16, wu_ref[...])
    act = (_silu(gate).reshape(rows, D_FF) * up).astype(BF16)
    y = x + _dot(act, wd_ref[...])
    if final:
        y = _rms_scale(y) * fnorm_ref[...]
    xo_ref[...] = y.reshape(nb, length, D_MODEL)


def _layer_spec(arr, layer):
    tail = arr.shape[1:]
    zeros = (0,) * len(tail)
    return pl.BlockSpec((None,) + tail, lambda i, j: (layer,) + zeros, pipeline_mode=pl.Buffered(1))


def _state_spec(arr, layer, nb):
    tail = arr.shape[2:]
    zeros = (0,) * len(tail)
    return pl.BlockSpec((None, nb) + tail, lambda i, j: (layer, i) + zeros)


def _state_out_spec(shape, nb):
    tail = shape[1:]
    zeros = (0,) * len(tail)
    return pl.BlockSpec((nb,) + tail, lambda i, j: (i,) + zeros)


def _mixer_call(x, states, state_layer, layer, w, *, nb, length, chunk, pos0):
    batch, seq, _ = x.shape
    rows = nb * length
    sd, sdc, sp, ssc = states
    grid = (batch // nb, seq // length)
    x_spec = pl.BlockSpec((nb, length, D_MODEL), lambda i, j: (i, j, 0))
    weights = (w["nmix"], w["wgt"], w["win"], w["dnw"], w["alog"], w["dtb"], w["onorm"], w["pw"], w["pscale"],
               w["scw"], w["wout"])
    out_shape = (
        jax.ShapeDtypeStruct(x.shape, F32),
        jax.ShapeDtypeStruct((batch,) + sd.shape[2:], F32),
        jax.ShapeDtypeStruct((batch,) + sdc.shape[2:], F32),
        jax.ShapeDtypeStruct((batch,) + sp.shape[2:], F32),
        jax.ShapeDtypeStruct((batch,) + ssc.shape[2:], F32),
    )
    scratch = [
        pltpu.VMEM((rows, DN_WIDTH), F32),
        pltpu.VMEM((nb, CONV_HIST + length, 3 * DN_WIDTH), F32),
        pltpu.VMEM((nb, POOL_HIST + length, POOL_WIDTH), F32),
        pltpu.VMEM((nb, POOL_HIST + length, POOL_WIDTH), F32),
        pltpu.VMEM((nb, POOL_HIST + length, POOL_WIDTH), F32),
        pltpu.VMEM((nb, POOL_HIST + length, POOL_WIDTH), F32),
        pltpu.VMEM((nb, CONV_HIST + length, SC_WIDTH), F32),
        pltpu.VMEM((DN_HEADS, rows, DN_HEAD_DIM), F32),
        pltpu.VMEM((DN_HEADS, rows, DN_HEAD_DIM), F32),
        pltpu.VMEM((DN_HEADS, rows, DN_HEAD_DIM), F32),
        pltpu.VMEM((DN_HEADS, rows, rows), BF16),
        pltpu.VMEM((DN_HEADS, DN_HEAD_DIM, rows), F32),
        pltpu.VMEM((DN_HEADS, rows, DN_HEAD_DIM), F32),
        pltpu.VMEM((DN_HEADS, rows, 2 * DN_HEAD_DIM), BF16),
        pltpu.VMEM((rows, DN_WIDTH), F32),
        pltpu.VMEM((rows, D_MODEL), BF16),
    ]
    kern = functools.partial(_mixer_kernel, nb=nb, length=length, chunk=chunk, pos0=pos0)
    return pl.pallas_call(
        kern,
        grid=grid,
        in_specs=[x_spec, _state_spec(sd, state_layer, nb), _state_spec(sdc, state_layer, nb),
                  _state_spec(sp, state_layer, nb), _state_spec(ssc, state_layer, nb)]
                 + [_layer_spec(a, layer) for a in weights],
        out_specs=(x_spec,) + tuple(_state_out_spec(s.shape, nb) for s in out_shape[1:]),
        out_shape=out_shape,
        scratch_shapes=scratch,
        compiler_params=pltpu.CompilerParams(
            dimension_semantics=("parallel", "arbitrary"), vmem_limit_bytes=VMEM_LIMIT_BYTES),
        name="mixer",
    )(x, sd, sdc, sp, ssc, *weights)


def _ffn_call(x, sf, state_layer, layer, w, *, nb, length, final):
    batch, seq, _ = x.shape
    grid = (batch // nb, seq // length)
    x_spec = pl.BlockSpec((nb, length, D_MODEL), lambda i, j: (i, j, 0))
    weights = (w["nffn"], w["wg"], w["fw"], w["wu"], w["wd"])
    fnorm = w["fnorm"]
    out_shape = (jax.ShapeDtypeStruct(x.shape, F32),
                 jax.ShapeDtypeStruct((batch,) + sf.shape[2:], F32))
    kern = functools.partial(_ffn_kernel, nb=nb, length=length, final=final)
    return pl.pallas_call(
        kern,
        grid=grid,
        in_specs=[x_spec, _state_spec(sf, state_layer, nb)] + [_layer_spec(a, layer) for a in weights]
                 + [pl.BlockSpec(fnorm.shape, lambda i, j: (0, 0))],
        out_specs=(x_spec, _state_out_spec(out_shape[1].shape, nb)),
        out_shape=out_shape,
        scratch_shapes=[pltpu.VMEM((nb, CONV_HIST + length, D_FF), F32)],
        compiler_params=pltpu.CompilerParams(
            dimension_semantics=("parallel", "arbitrary"), vmem_limit_bytes=VMEM_LIMIT_BYTES),
        name="ffn",
    )(x, sf, *weights, fnorm)


def _prepare_weights(norm_mix, w_in, dn_conv_w, dn_a_log, dn_dt_bias, dn_out_norm, pool_w, pool_scale,
                     sconv_w, w_out, norm_ffn, w_ffn_gate, ffn_conv_w, w_ffn_up, w_ffn_down, final_norm):
    depth = w_in.shape[0]
    assert N_GATE == SUBLANES
    qkv0, z0 = 0, 3 * DN_WIDTH
    g0 = z0 + DN_WIDTH
    rest0 = g0 + N_GATE
    wgt = jnp.swapaxes(w_in[:, :, g0:g0 + N_GATE], 1, 2).astype(BF16)
    pieces = []
    for hh in range(DN_HEADS):
        for part in range(3):
            c0 = qkv0 + part * DN_WIDTH + hh * DN_HEAD_DIM
            pieces.append(w_in[:, :, c0:c0 + DN_HEAD_DIM])
    pieces += [w_in[:, :, rest0:rest0 + REST_WIDTH], w_in[:, :, z0:z0 + DN_WIDTH]]
    win = jnp.concatenate(pieces, axis=2).astype(BF16)
    pad = jnp.zeros((depth, SUBLANES - DN_HEADS, 1), F32)
    eye = jnp.eye(len(POOL_WINDOWS), dtype=F32)
    pw = (pool_w[:, :, :, None, :] * eye[None, :, None, :, None]).reshape(depth, POOL_WIDTH, POOL_WIDTH)
    return {
        "nmix": norm_mix[:, None, :],
        "wgt": wgt,
        "win": win,
        "dnw": dn_conv_w,
        "alog": jnp.concatenate([dn_a_log[:, :, None], pad], axis=1),
        "dtb": jnp.concatenate([dn_dt_bias[:, :, None], pad], axis=1),
        "onorm": dn_out_norm[:, None, :],
        "pw": pw.astype(BF16),
        "pscale": pool_scale[:, None, :],
        "scw": sconv_w,
        "wout": w_out.astype(BF16),
        "nffn": norm_ffn[:, None, :],
        "wg": w_ffn_gate.astype(BF16),
        "fw": ffn_conv_w,
        "wu": w_ffn_up.astype(BF16),
        "wd": w_ffn_down.astype(BF16),
        "fnorm": final_norm[None, :],
    }


def _trunk(x, states, state_layers, weights, depth, *, nb_mix, nb_ffn, length, chunk, pos0):
    sd, sdc, sp, ssc, sf = states
    outs = [[] for _ in range(5)]
    for l in range(depth):
        sl = state_layers[l]
        x, o_d, o_dc, o_p, o_sc = _mixer_call(x, (sd, sdc, sp, ssc), sl, l, weights,
                                              nb=nb_mix, length=length, chunk=chunk, pos0=pos0)
        x, o_f = _ffn_call(x, sf, sl, l, weights, nb=nb_ffn, length=length, final=(l == depth - 1))
        for acc, o in zip(outs, (o_d, o_dc, o_p, o_sc, o_f)):
            acc.append(o)
    return x, tuple(jnp.stack(o) for o in outs)


PROMPT_TILE = 256
SAMPLE_MIX_ROWS = 16
SAMPLE_FFN_ROWS = 32


def kernel(x_prompt, x_sample, state_delta, state_delta_conv, state_pool, state_sconv, state_ffn_conv,
           norm_mix, w_in, dn_conv_w, dn_a_log, dn_dt_bias, dn_out_norm, pool_w, pool_scale,
           sconv_w, w_out, norm_ffn, w_ffn_gate, ffn_conv_w, w_ffn_up, w_ffn_down, final_norm):
    depth = w_in.shape[0]
    weights = _prepare_weights(norm_mix, w_in, dn_conv_w, dn_a_log, dn_dt_bias, dn_out_norm, pool_w,
                               pool_scale, sconv_w, w_out, norm_ffn, w_ffn_gate, ffn_conv_w, w_ffn_up,
                               w_ffn_down, final_norm)

    batch, seq, _ = x_prompt.shape
    dec_batch, dec_seq, _ = x_sample.shape
    sample_states = (state_delta, state_delta_conv, state_pool, state_sconv, state_ffn_conv)
    zero_states = tuple(jnp.zeros((1, batch) + s.shape[2:], F32) for s in sample_states)

    y_prompt, p_st = _trunk(x_prompt, zero_states, (0,) * depth, weights, depth,
                            nb_mix=1, nb_ffn=1, length=min(PROMPT_TILE, seq),
                            chunk=min(PROMPT_TILE, seq), pos0=0)
    y_sample, s_st = _trunk(x_sample, sample_states, tuple(range(depth)), weights, depth,
                            nb_mix=SAMPLE_MIX_ROWS, nb_ffn=SAMPLE_FFN_ROWS, length=dec_seq,
                            chunk=min(DN_CHUNK, dec_seq), pos0=PAST_LEN)
    return (y_prompt, y_sample) + p_st + s_st
```

```python
import functools
import math
import types

import jax
import jax.numpy as jnp
from jax import lax
from jax.experimental import pallas as pl
from jax.experimental.pallas import tpu as pltpu

F32 = jnp.float32
BF16 = jnp.bfloat16

LANES = 128
SUBLANES = 8
BF16_ROWS = 16
VMEM_LIMIT_BYTES = 56 * 1024 * 1024

D_MODEL = 1024
DN_HEADS = 4
DN_HEAD_DIM = 128
DN_WIDTH = DN_HEADS * DN_HEAD_DIM
DN_CONV = 4
DN_CHUNK = 64
POOL_WINDOWS = (2, 4, 8, 16)
POOL_WIDTH = 256
POOL_GROUP_DIM = POOL_WIDTH // len(POOL_WINDOWS)
POOL_BUF = max(POOL_WINDOWS) - 1
SC_WIDTH = 256
SC_CONV = 3
D_FF = 2816
FFN_CONV = 3
EPS = 1e-6
PAST_LEN = 16384

HEAD_GROUP = 2
N_GATE = 2 * DN_HEADS
W_QKV = 0
QKV_GROUP = 3 * HEAD_GROUP * DN_HEAD_DIM
W_REST = W_QKV + (DN_HEADS // HEAD_GROUP) * QKV_GROUP
REST_WIDTH = POOL_WIDTH + 3 * SC_WIDTH
W_Z = W_REST + REST_WIDTH
D_IN_PAD = W_Z + DN_WIDTH
D_REST = POOL_WIDTH + SC_WIDTH

CONV_HIST = SUBLANES
POOL_HIST = 24

L_GC, L_BETA, L_EGC, L_KDF, L_EGL = 0, DN_HEADS, SUBLANES, 2 * SUBLANES, 3 * SUBLANES


def _silu(t):
    return t / (1.0 + jnp.exp(-t))


def _softplus(t):
    return jnp.maximum(t, 0.0) + jnp.log1p(jnp.exp(-jnp.abs(t)))


def _rms_scale(t):
    return t * lax.rsqrt(jnp.mean(t * t, axis=-1, keepdims=True) + EPS)


def _dot(a, b):
    return jnp.dot(a, b, preferred_element_type=F32)


def _dot_nt(a, b):
    return lax.dot_general(a, b, (((1,), (1,)), ((), ())), preferred_element_type=F32)


def _head_slice(hh):
    return slice(hh * DN_HEAD_DIM, (hh + 1) * DN_HEAD_DIM)


def _interleave(*stages):
    clock = [0.0] * len(stages)
    live = list(range(len(stages)))
    while live:
        i = min(live, key=lambda s: clock[s])
        try:
            clock[i] += next(stages[i])
        except StopIteration:
            live.remove(i)


def _carry_history(j, length, items, first_tile=None):
    @pl.when(j == 0)
    def _():
        if first_tile is not None:
            first_tile()
        for ext, state_ref, hist, nbuf in items:
            ext[:, 0:hist - nbuf, :] = jnp.zeros((ext.shape[0], hist - nbuf, ext.shape[2]), F32)
            ext[:, hist - nbuf:hist, :] = state_ref[...]

    @pl.when(j != 0)
    def _():
        for ext, state_ref, hist, nbuf in items:
            ext[:, hist - nbuf:hist, :] = ext[:, hist + length - nbuf:hist + length, :]


def _unit_lower_inverses(na16, t16, log2c, out):
    heads = range(len(na16))
    n = na16[0].shape[0]
    row = lax.broadcasted_iota(jnp.int32, (n, n), 0)
    col = lax.broadcasted_iota(jnp.int32, (n, n), 1)
    for lvl in range(1, log2c):
        blk = 1 << lvl
        if blk < BF16_ROWS:
            off = (((row >> (lvl + 1)) == (col >> (lvl + 1)))
                   & (((row >> lvl) & 1) == 1) & (((col >> lvl) & 1) == 0))
            p16 = [_dot(na16[i], t16[i]).astype(BF16) for i in heads]
            t16 = [t16[i] + jnp.where(off, _dot(t16[i], p16[i]), 0.0).astype(BF16) for i in heads]
        else:
            pairs = n // (2 * blk)
            rowh = lax.broadcasted_iota(jnp.int32, (n // 2, n), 0)
            colh = lax.broadcasted_iota(jnp.int32, (n // 2, n), 1)
            offh = ((colh >> (lvl + 1)) == (rowh >> lvl)) & (((colh >> lvl) & 1) == 0)
            t4 = [t16[i].reshape(pairs, 2, blk, n) for i in heads]
            na_odd = [na16[i].reshape(pairs, 2, blk, n)[:, 1].reshape(n // 2, n) for i in heads]
            t_odd = [t4[i][:, 1].reshape(n // 2, n) for i in heads]
            p = [_dot(na_odd[i], t16[i]) for i in heads]
            p16 = [jnp.broadcast_to(p[i].reshape(pairs, 1, blk, n), (pairs, 2, blk, n))
                   .reshape(n, n).astype(BF16) for i in heads]
            x_odd = [jnp.where(offh, _dot(t_odd[i], p16[i]), 0.0).astype(BF16) for i in heads]
            t16 = [jnp.concatenate([t4[i][:, 0:1], (t_odd[i] + x_odd[i]).reshape(pairs, 1, blk, n)],
                                   axis=1).reshape(n, n) for i in heads]
        yield 0.6
    out.extend(t16)


def _stage_a(j, r, st, *, nb, length, chunk, pos0):
    rows = nb * length
    log2c = int(math.log2(chunk))

    x = r.x[...].reshape(rows, D_MODEL)
    h16 = (_rms_scale(x) * r.nmix[...]).astype(BF16)

    gates_t = _dot_nt(r.wgt[...], h16)
    g_t = -jnp.exp(r.alog[...]) * _softplus(gates_t + r.dtb[...])
    beta_t = 1.0 / (1.0 + jnp.exp(-gates_t))
    pos_in_chunk = lax.broadcasted_iota(jnp.int32, (SUBLANES, rows), 1) & (chunk - 1)
    gc_t = g_t
    sfx_t = g_t
    step = 1
    while step < chunk:
        gc_t = gc_t + jnp.where(pos_in_chunk >= step, pltpu.roll(gc_t, step, axis=1), 0.0)
        sfx_t = sfx_t + jnp.where(pos_in_chunk + step < chunk,
                                  pltpu.roll(sfx_t, rows - step, axis=1), 0.0)
        step *= 2
    rest_t = sfx_t - g_t
    sub = lax.broadcasted_iota(jnp.int32, (SUBLANES, rows), 0)
    cols = jnp.concatenate(
        [jnp.where(sub < DN_HEADS, gc_t, -beta_t), jnp.exp(gc_t), jnp.exp(rest_t), jnp.exp(gc_t + rest_t),
         jnp.zeros((LANES - 4 * SUBLANES, rows), F32)], axis=0).T

    def colb(lane, width=LANES):
        return jnp.broadcast_to(cols[:, lane:lane + 1], (rows, width))

    yield 1.0
    for grp in range(DN_HEADS // HEAD_GROUP):
        pq = _dot(h16, r.win[:, W_QKV + grp * QKV_GROUP:W_QKV + (grp + 1) * QKV_GROUP])
        for i in range(HEAD_GROUP):
            for part in range(3):
                src = (3 * i + part) * DN_HEAD_DIM
                dst = (part * DN_HEADS + grp * HEAD_GROUP + i) * DN_HEAD_DIM
                r.extd[:, CONV_HIST:CONV_HIST + length, dst:dst + DN_HEAD_DIM] = (
                    pq[:, src:src + DN_HEAD_DIM].reshape(nb, length, DN_HEAD_DIM))
        yield 0.45
    r.odc[...] = r.extd[:, CONV_HIST + length - (DN_CONV - 1):CONV_HIST + length, :]

    def conv_block(ci):
        cs = slice(ci * LANES, (ci + 1) * LANES)
        base = CONV_HIST - (DN_CONV - 1)
        acc = r.extd[:, base:base + length, cs] * r.dnw[0:1, cs][None]
        for i in range(1, DN_CONV):
            acc = acc + r.extd[:, base + i:base + i + length, cs] * r.dnw[i:i + 1, cs][None]
        return _silu(acc).reshape(rows, LANES)

    row = lax.broadcasted_iota(jnp.int32, (rows, rows), 0)
    col = lax.broadcasted_iota(jnp.int32, (rows, rows), 1)
    same_chunk = (row >> log2c) == (col >> log2c)
    m_incl = same_chunk & (row >= col)
    m_strict = same_chunk & (row > col)
    pair = ((row >> 1) == (col >> 1)) & ((row & 1) == 1) & ((col & 1) == 0)
    eye = jnp.where(row == col, 1.0, 0.0).astype(F32)

    for hh in range(DN_HEADS):
        nbeta_b = colb(L_BETA + hh)
        egc_b = colb(L_EGC + hh)
        qc = conv_block(hh)
        kc = conv_block(DN_HEADS + hh)
        vc = conv_block(2 * DN_HEADS + hh)
        q = qc * (lax.rsqrt(jnp.sum(qc * qc, axis=-1, keepdims=True) + EPS) * (DN_HEAD_DIM ** -0.5))
        k = kc * lax.rsqrt(jnp.sum(kc * kc, axis=-1, keepdims=True) + EPS)
        nkb = k * nbeta_b
        k16 = k.astype(BF16)
        yield 0.6
        diff = colb(L_GC + hh, rows) - gc_t[hh:hh + 1, :]
        decay = jnp.where(m_incl, jnp.exp(jnp.where(m_incl, diff, 0.0)), 0.0)
        na = jnp.where(m_strict, _dot_nt(nkb.astype(BF16), k16) * decay, 0.0)
        st.na[hh] = na.astype(BF16)
        st.t0[hh] = jnp.where(pair, na, eye).astype(BF16)
        st.qk[hh] = (_dot_nt(q.astype(BF16), k16) * decay).astype(BF16)
        st.rhs[hh] = jnp.concatenate([vc * nbeta_b, nkb * egc_b], axis=1).astype(BF16)
        st.qg[hh] = q * egc_b
        st.kdt[hh] = (k * colb(L_KDF + hh)).T
        st.egl[hh] = colb(L_EGL + hh)
        yield 0.5

    prest = _dot(h16, r.win[:, W_REST:W_REST + REST_WIDTH])
    st.z[...] = _dot(h16, r.win[:, W_Z:W_Z + DN_WIDTH])
    p_in = prest[:, 0:POOL_WIDTH]
    sc_x = prest[:, POOL_WIDTH:POOL_WIDTH + SC_WIDTH]
    sc_b = prest[:, POOL_WIDTH + SC_WIDTH:POOL_WIDTH + 2 * SC_WIDTH]
    sc_c = prest[:, POOL_WIDTH + 2 * SC_WIDTH:]
    yield 0.5

    ph = POOL_HIST
    r.extp[:, ph:ph + length, :] = p_in.reshape(nb, length, POOL_WIDTH)
    r.op[...] = r.extp[:, ph + length - POOL_BUF:ph + length, :]
    zero8 = jnp.zeros((nb, SUBLANES, POOL_WIDTH), F32)
    end = ph + length
    r.s2[:, 0:8, :] = zero8
    r.s4[:, 0:8, :] = zero8
    r.s8[:, 0:8, :] = zero8
    r.s2[:, 8:end, :] = r.extp[:, 8:end, :] + r.extp[:, 7:end - 1, :]
    r.s4[:, 8:end, :] = r.s2[:, 8:end, :] + r.s2[:, 6:end - 2, :]
    r.s8[:, 8:end, :] = r.s4[:, 8:end, :] + r.s4[:, 4:end - 4, :]
    s16 = r.s8[:, ph:end, :] + r.s8[:, ph - 8:end - 8, :]
    lane3 = lax.broadcasted_iota(jnp.int32, (nb, length, POOL_WIDTH), 2)
    grp = lane3 // POOL_GROUP_DIM
    wsum = jnp.where(grp == 0, r.s2[:, ph:end, :],
                     jnp.where(grp == 1, r.s4[:, ph:end, :],
                               jnp.where(grp == 2, r.s8[:, ph:end, :], s16)))
    win = jnp.where(grp == 0, POOL_WINDOWS[0],
                    jnp.where(grp == 1, POOL_WINDOWS[1],
                              jnp.where(grp == 2, POOL_WINDOWS[2], POOL_WINDOWS[3])))
    tpos = lax.broadcasted_iota(jnp.int32, (nb, length, POOL_WIDTH), 1) + (pos0 + j * length)
    cnt = jnp.minimum(tpos + 1, win).astype(F32)
    dpool = (wsum / cnt - r.extp[:, ph:end, :]).reshape(rows, POOL_WIDTH)
    r.cat_a[:, 0:POOL_WIDTH] = (_dot(dpool.astype(BF16), r.pw[...]) * r.pscale[...]).astype(BF16)
    yield 0.4

    r.exts[:, CONV_HIST:CONV_HIST + length, :] = (sc_c * sc_x).reshape(nb, length, SC_WIDTH)
    r.osc[...] = r.exts[:, CONV_HIST + length - (SC_CONV - 1):CONV_HIST + length, :]
    sbase = CONV_HIST - (SC_CONV - 1)
    cconv = r.exts[:, sbase:sbase + length, :] * r.scw[0:1, :][None]
    for i in range(1, SC_CONV):
        cconv = cconv + r.exts[:, sbase + i:sbase + i + length, :] * r.scw[i:i + 1, :][None]
    r.cat_a[:, POOL_WIDTH:] = (sc_b * cconv.reshape(rows, SC_WIDTH)).astype(BF16)

    st.xy[...] = x + _dot(r.cat_a[...], r.wout[DN_WIDTH:, :])


def _stage_b(r, st, *, nb, length, chunk):
    rows = nb * length
    log2c = int(math.log2(chunk))
    n_chunks = rows // chunk

    tinv = []
    yield from _unit_lower_inverses([st.na[hh] for hh in range(DN_HEADS)],
                                    [st.t0[hh] for hh in range(DN_HEADS)], log2c, tinv)
    for hh in range(DN_HEADS):
        nsol = _dot(tinv[hh], st.rhs[hh])
        r.nu[hh] = nsol[:, :DN_HEAD_DIM]
        r.nw[hh] = nsol[:, DN_HEAD_DIM:]
    yield 0.5

    if n_chunks == 1:
        for hh in range(DN_HEADS):
            s = r.od[0, hh]
            nwq = jnp.concatenate([r.nw[hh], st.qg[hh]], axis=0).astype(BF16)
            res = _dot(nwq, s.astype(BF16))
            vn16 = (res[:rows] - r.nu[hh]).astype(BF16)
            r.o[:, _head_slice(hh)] = res[rows:] + _dot(st.qk[hh], vn16)
            r.od[0, hh] = s * st.egl[hh, 0:1, :] + _dot(st.kdt[hh].astype(BF16), vn16)
            yield 0.3
    else:
        assert n_chunks == nb
        seg = lax.broadcasted_iota(jnp.int32, (nb, 1, rows), 2) >> log2c
        own = seg == lax.broadcasted_iota(jnp.int32, (nb, 1, rows), 0)
        for hh in range(DN_HEADS):
            vn_parts, qs_parts = [], []
            for b in range(nb):
                rs = slice(b * chunk, (b + 1) * chunk)
                nwq = jnp.concatenate([r.nw[hh, rs, :], st.qg[hh, rs, :]], axis=0).astype(BF16)
                res = _dot(nwq, r.od[b, hh].astype(BF16))
                vn_parts.append(res[:chunk] - r.nu[hh, rs, :])
                qs_parts.append(res[chunk:])
            vn16 = jnp.concatenate(vn_parts, axis=0).astype(BF16)
            r.o[:, _head_slice(hh)] = jnp.concatenate(qs_parts, axis=0) + _dot(st.qk[hh], vn16)
            kd_own = jnp.where(own, st.kdt[hh][None], 0.0).astype(BF16).reshape(nb * DN_HEAD_DIM, rows)
            upd = _dot(kd_own, vn16).reshape(nb, DN_HEAD_DIM, DN_HEAD_DIM)
            egl = st.egl[hh].reshape(nb, chunk, DN_HEAD_DIM)[:, 0:1, :]
            r.od[:, hh, :, :] = r.od[:, hh, :, :] * egl + upd
            yield 0.3

    for hh in range(DN_HEADS):
        hs = _head_slice(hh)
        r.cat_b[:, hs] = (_rms_scale(r.o[:, hs]) * r.onorm[...] * _silu(st.z[:, hs])).astype(BF16)
    y = st.xy[...] + _dot(r.cat_b[...], r.wout[:DN_WIDTH, :])
    r.xo[...] = y.reshape(nb, length, D_MODEL)


_MIXER_IN = ("x", "sd", "sdc", "sp", "ssc", "nmix", "wgt", "win", "dnw", "alog", "dtb", "onorm", "pw",
             "pscale", "scw", "wout")
_MIXER_OUT = ("xo", "od", "odc", "op", "osc")
_MIXER_PRIVATE = ("extd", "extp", "s2", "s4", "s8", "exts", "cat_a", "nu", "nw", "o", "cat_b")
_MIXER_SET = ("na", "t0", "qk", "rhs", "qg", "kdt", "egl", "z", "xy")


def _mixer_scratch(nb, length, n_sets):
    rows = nb * length
    private = [
        pltpu.VMEM((nb, CONV_HIST + length, 3 * DN_WIDTH), F32),
        pltpu.VMEM((nb, POOL_HIST + length, POOL_WIDTH), F32),
        pltpu.VMEM((nb, POOL_HIST + length, POOL_WIDTH), F32),
        pltpu.VMEM((nb, POOL_HIST + length, POOL_WIDTH), F32),
        pltpu.VMEM((nb, POOL_HIST + length, POOL_WIDTH), F32),
        pltpu.VMEM((nb, CONV_HIST + length, SC_WIDTH), F32),
        pltpu.VMEM((rows, D_REST), BF16),
        pltpu.VMEM((DN_HEADS, rows, DN_HEAD_DIM), F32),
        pltpu.VMEM((DN_HEADS, rows, DN_HEAD_DIM), F32),
        pltpu.VMEM((rows, DN_WIDTH), F32),
        pltpu.VMEM((rows, DN_WIDTH), BF16),
    ]
    one_set = [
        pltpu.VMEM((DN_HEADS, rows, rows), BF16),
        pltpu.VMEM((DN_HEADS, rows, rows), BF16),
        pltpu.VMEM((DN_HEADS, rows, rows), BF16),
        pltpu.VMEM((DN_HEADS, rows, 2 * DN_HEAD_DIM), BF16),
        pltpu.VMEM((DN_HEADS, rows, DN_HEAD_DIM), F32),
        pltpu.VMEM((DN_HEADS, DN_HEAD_DIM, rows), F32),
        pltpu.VMEM((DN_HEADS, rows, DN_HEAD_DIM), F32),
        pltpu.VMEM((rows, DN_WIDTH), F32),
        pltpu.VMEM((rows, D_MODEL), F32),
    ]
    return private + one_set * n_sets


def _mixer_kernel(*refs, nb, length, chunk, pos0, n_tiles):
    n_in, n_out, n_priv, n_set = len(_MIXER_IN), len(_MIXER_OUT), len(_MIXER_PRIVATE), len(_MIXER_SET)
    names = _MIXER_IN + _MIXER_OUT + _MIXER_PRIVATE
    r = types.SimpleNamespace(**dict(zip(names, refs[:len(names)])))
    set_refs = refs[len(names):]
    sets = [types.SimpleNamespace(**dict(zip(_MIXER_SET, set_refs[i * n_set:(i + 1) * n_set])))
            for i in range(len(set_refs) // n_set)]
    del n_in, n_out, n_priv
    j = pl.program_id(1)

    def init_state():
        r.od[...] = r.sd[...]

    _carry_history(j, length,
                   [(r.extd, r.sdc, CONV_HIST, DN_CONV - 1), (r.extp, r.sp, POOL_HIST, POOL_BUF),
                    (r.exts, r.ssc, CONV_HIST, SC_CONV - 1)], first_tile=init_state)

    stage_a = functools.partial(_stage_a, j, r, nb=nb, length=length, chunk=chunk, pos0=pos0)
    stage_b = functools.partial(_stage_b, r, nb=nb, length=length, chunk=chunk)

    if len(sets) == 1:
        _interleave(stage_a(sets[0]))
        _interleave(stage_b(sets[0]))
        return

    middle = (j > 0) & (j < n_tiles)

    @pl.when(j == 0)
    def _():
        _interleave(stage_a(sets[0]))

    @pl.when(middle & (j % 2 == 1))
    def _():
        _interleave(stage_b(sets[0]), stage_a(sets[1]))

    @pl.when(middle & (j % 2 == 0))
    def _():
        _interleave(stage_b(sets[1]), stage_a(sets[0]))

    @pl.when(j == n_tiles)
    def _():
        _interleave(stage_b(sets[(n_tiles - 1) % 2]))


def _ffn_kernel(x_ref, sf_ref, nffn_ref, wg_ref, fw_ref, wu_ref, wd_ref, fnorm_ref,
                xo_ref, of_ref, extf_s, *, nb, length, final):
    j = pl.program_id(1)
    rows = nb * length
    _carry_history(j, length, [(extf_s, sf_ref, CONV_HIST, FFN_CONV - 1)])
    x = x_ref[...].reshape(rows, D_MODEL)
    h16 = (_rms_scale(x) * nffn_ref[...]).astype(BF16)
    extf_s[:, CONV_HIST:CONV_HIST + length, :] = _dot(h16, wg_ref[...]).reshape(nb, length, D_FF)
    of_ref[...] = extf_s[:, CONV_HIST + length - (FFN_CONV - 1):CONV_HIST + length, :]
    base = CONV_HIST - (FFN_CONV - 1)
    gate = extf_s[:, base:base + length, :] * fw_ref[0:1, :][None]
    for i in range(1, FFN_CONV):
        gate = gate + extf_s[:, base + i:base + i + length, :] * fw_ref[i:i + 1, :][None]
    up = _dot(h16, wu_ref[...])
    act = (_silu(gate).reshape(rows, D_FF) * up).astype(BF16)
    y = x + _dot(act, wd_ref[...])
    if final:
        y = _rms_scale(y) * fnorm_ref[...]
    xo_ref[...] = y.reshape(nb, length, D_MODEL)


def _layer_spec(arr, layer):
    tail = arr.shape[1:]
    zeros = (0,) * len(tail)
    return pl.BlockSpec((None,) + tail, lambda i, j: (layer,) + zeros, pipeline_mode=pl.Buffered(1))


def _state_spec(arr, layer, nb):
    tail = arr.shape[2:]
    zeros = (0,) * len(tail)
    return pl.BlockSpec((None, nb) + tail, lambda i, j: (layer, i) + zeros)


def _state_out_spec(shape, nb):
    tail = shape[1:]
    zeros = (0,) * len(tail)
    return pl.BlockSpec((nb,) + tail, lambda i, j: (i,) + zeros)


def _mixer_call(x, states, state_layer, layer, w, *, nb, length, chunk, pos0):
    batch, seq, _ = x.shape
    sd, sdc, sp, ssc = states
    n_tiles = seq // length
    pipelined = n_tiles > 1
    if pipelined:
        last = n_tiles - 1
        grid = (batch // nb, n_tiles + 1)
        x_in_spec = pl.BlockSpec((nb, length, D_MODEL), lambda i, j: (i, jnp.minimum(j, last), 0))
        x_out_spec = pl.BlockSpec((nb, length, D_MODEL), lambda i, j: (i, jnp.maximum(j - 1, 0), 0))
    else:
        grid = (batch // nb, n_tiles)
        x_in_spec = x_out_spec = pl.BlockSpec((nb, length, D_MODEL), lambda i, j: (i, j, 0))
    weights = tuple(w[name] for name in _MIXER_IN[5:])
    out_shape = (
        jax.ShapeDtypeStruct(x.shape, F32),
        jax.ShapeDtypeStruct((batch,) + sd.shape[2:], F32),
        jax.ShapeDtypeStruct((batch,) + sdc.shape[2:], F32),
        jax.ShapeDtypeStruct((batch,) + sp.shape[2:], F32),
        jax.ShapeDtypeStruct((batch,) + ssc.shape[2:], F32),
    )
    kern = functools.partial(_mixer_kernel, nb=nb, length=length, chunk=chunk, pos0=pos0, n_tiles=n_tiles)
    return pl.pallas_call(
        kern,
        grid=grid,
        in_specs=[x_in_spec, _state_spec(sd, state_layer, nb), _state_spec(sdc, state_layer, nb),
                  _state_spec(sp, state_layer, nb), _state_spec(ssc, state_layer, nb)]
                 + [_layer_spec(a, layer) for a in weights],
        out_specs=(x_out_spec,) + tuple(_state_out_spec(s.shape, nb) for s in out_shape[1:]),
        out_shape=out_shape,
        scratch_shapes=_mixer_scratch(nb, length, 2 if pipelined else 1),
        compiler_params=pltpu.CompilerParams(
            dimension_semantics=("parallel", "arbitrary"), vmem_limit_bytes=VMEM_LIMIT_BYTES),
        name="mixer",
    )(x, sd, sdc, sp, ssc, *weights)


def _ffn_call(x, sf, state_layer, layer, w, *, nb, length, final):
    batch, seq, _ = x.shape
    grid = (batch // nb, seq // length)
    x_spec = pl.BlockSpec((nb, length, D_MODEL), lambda i, j: (i, j, 0))
    weights = (w["nffn"], w["wg"], w["fw"], w["wu"], w["wd"])
    fnorm = w["fnorm"]
    out_shape = (jax.ShapeDtypeStruct(x.shape, F32),
                 jax.ShapeDtypeStruct((batch,) + sf.shape[2:], F32))
    kern = functools.partial(_ffn_kernel, nb=nb, length=length, final=final)
    return pl.pallas_call(
        kern,
        grid=grid,
        in_specs=[x_spec, _state_spec(sf, state_layer, nb)] + [_layer_spec(a, layer) for a in weights]
                 + [pl.BlockSpec(fnorm.shape, lambda i, j: (0, 0))],
        out_specs=(x_spec, _state_out_spec(out_shape[1].shape, nb)),
        out_shape=out_shape,
        scratch_shapes=[pltpu.VMEM((nb, CONV_HIST + length, D_FF), F32)],
        compiler_params=pltpu.CompilerParams(
            dimension_semantics=("parallel", "arbitrary"), vmem_limit_bytes=VMEM_LIMIT_BYTES),
        name="ffn",
    )(x, sf, *weights, fnorm)


def _prepare_weights(norm_mix, w_in, dn_conv_w, dn_a_log, dn_dt_bias, dn_out_norm, pool_w, pool_scale,
                     sconv_w, w_out, norm_ffn, w_ffn_gate, ffn_conv_w, w_ffn_up, w_ffn_down, final_norm):
    depth = w_in.shape[0]
    assert N_GATE == SUBLANES
    qkv0, z0 = 0, 3 * DN_WIDTH
    g0 = z0 + DN_WIDTH
    rest0 = g0 + N_GATE
    wgt = jnp.swapaxes(w_in[:, :, g0:g0 + N_GATE], 1, 2).astype(BF16)
    pieces = []
    for hh in range(DN_HEADS):
        for part in range(3):
            c0 = qkv0 + part * DN_WIDTH + hh * DN_HEAD_DIM
            pieces.append(w_in[:, :, c0:c0 + DN_HEAD_DIM])
    pieces += [w_in[:, :, rest0:rest0 + REST_WIDTH], w_in[:, :, z0:z0 + DN_WIDTH]]
    win = jnp.concatenate(pieces, axis=2).astype(BF16)
    pad = jnp.zeros((depth, SUBLANES - DN_HEADS, 1), F32)
    eye = jnp.eye(len(POOL_WINDOWS), dtype=F32)
    pw = (pool_w[:, :, :, None, :] * eye[None, :, None, :, None]).reshape(depth, POOL_WIDTH, POOL_WIDTH)
    return {
        "nmix": norm_mix[:, None, :],
        "wgt": wgt,
        "win": win,
        "dnw": dn_conv_w,
        "alog": jnp.concatenate([dn_a_log[:, :, None], pad], axis=1),
        "dtb": jnp.concatenate([dn_dt_bias[:, :, None], pad], axis=1),
        "onorm": dn_out_norm[:, None, :],
        "pw": pw.astype(BF16),
        "pscale": pool_scale[:, None, :],
        "scw": sconv_w,
        "wout": w_out.astype(BF16),
        "nffn": norm_ffn[:, None, :],
        "wg": w_ffn_gate.astype(BF16),
        "fw": ffn_conv_w,
        "wu": w_ffn_up.astype(BF16),
        "wd": w_ffn_down.astype(BF16),
        "fnorm": final_norm[None, :],
    }


def _trunk(x, states, state_layers, weights, depth, *, nb_mix, nb_ffn, length, chunk, pos0):
    sd, sdc, sp, ssc, sf = states
    outs = [[] for _ in range(5)]
    for l in range(depth):
        sl = state_layers[l]
        x, o_d, o_dc, o_p, o_sc = _mixer_call(x, (sd, sdc, sp, ssc), sl, l, weights,
                                              nb=nb_mix, length=length, chunk=chunk, pos0=pos0)
        x, o_f = _ffn_call(x, sf, sl, l, weights, nb=nb_ffn, length=length, final=(l == depth - 1))
        for acc, o in zip(outs, (o_d, o_dc, o_p, o_sc, o_f)):
            acc.append(o)
    return x, tuple(jnp.stack(o) for o in outs)


PROMPT_TILE = 256
SAMPLE_MIX_ROWS = 16
SAMPLE_FFN_ROWS = 32


def kernel(x_prompt, x_sample, state_delta, state_delta_conv, state_pool, state_sconv, state_ffn_conv,
           norm_mix, w_in, dn_conv_w, dn_a_log, dn_dt_bias, dn_out_norm, pool_w, pool_scale,
           sconv_w, w_out, norm_ffn, w_ffn_gate, ffn_conv_w, w_ffn_up, w_ffn_down, final_norm):
    depth = w_in.shape[0]
    weights = _prepare_weights(norm_mix, w_in, dn_conv_w, dn_a_log, dn_dt_bias, dn_out_norm, pool_w,
                               pool_scale, sconv_w, w_out, norm_ffn, w_ffn_gate, ffn_conv_w, w_ffn_up,
                               w_ffn_down, final_norm)

    batch, seq, _ = x_prompt.shape
    dec_batch, dec_seq, _ = x_sample.shape
    sample_states = (state_delta, state_delta_conv, state_pool, state_sconv, state_ffn_conv)
    zero_states = tuple(jnp.zeros((1, batch) + s.shape[2:], F32) for s in sample_states)

    y_prompt, p_st = _trunk(x_prompt, zero_states, (0,) * depth, weights, depth,
                            nb_mix=1, nb_ffn=1, length=min(PROMPT_TILE, seq),
                            chunk=min(PROMPT_TILE, seq), pos0=0)
    y_sample, s_st = _trunk(x_sample, sample_states, tuple(range(depth)), weights, depth,
                            nb_mix=SAMPLE_MIX_ROWS, nb_ffn=SAMPLE_FFN_ROWS, length=dec_seq,
                            chunk=min(DN_CHUNK, dec_seq), pos0=PAST_LEN)
    return (y_prompt, y_sample) + p_st + s_st
```

```python
import functools
import math
import types

import jax
import jax.numpy as jnp
from jax import lax
from jax.experimental import pallas as pl
from jax.experimental.pallas import tpu as pltpu

F32 = jnp.float32
BF16 = jnp.bfloat16

LANES = 128
SUBLANES = 8
BF16_ROWS = 16
VMEM_LIMIT_BYTES = 56 * 1024 * 1024

D_MODEL = 1024
DN_HEADS = 4
DN_HEAD_DIM = 128
DN_WIDTH = DN_HEADS * DN_HEAD_DIM
DN_CONV = 4
DN_CHUNK = 64
POOL_WINDOWS = (2, 4, 8, 16)
POOL_WIDTH = 256
POOL_GROUP_DIM = POOL_WIDTH // len(POOL_WINDOWS)
POOL_BUF = max(POOL_WINDOWS) - 1
SC_WIDTH = 256
SC_CONV = 3
D_FF = 2816
FFN_CONV = 3
EPS = 1e-6
PAST_LEN = 16384

HEAD_GROUP = 2
N_GATE = 2 * DN_HEADS
W_QKV = 0
QKV_GROUP = 3 * HEAD_GROUP * DN_HEAD_DIM
W_REST = W_QKV + (DN_HEADS // HEAD_GROUP) * QKV_GROUP
REST_WIDTH = POOL_WIDTH + 3 * SC_WIDTH
W_Z = W_REST + REST_WIDTH
D_IN_PAD = W_Z + DN_WIDTH
D_REST = POOL_WIDTH + SC_WIDTH

CONV_HIST = SUBLANES
POOL_HIST = 24

L_GC, L_BETA, L_EGC, L_KDF, L_EGL = 0, DN_HEADS, SUBLANES, 2 * SUBLANES, 3 * SUBLANES


def _silu(t):
    return t / (1.0 + jnp.exp(-t))


def _softplus(t):
    return jnp.maximum(t, 0.0) + jnp.log1p(jnp.exp(-jnp.abs(t)))


def _rms_scale(t):
    return t * lax.rsqrt(jnp.mean(t * t, axis=-1, keepdims=True) + EPS)


def _dot(a, b):
    return jnp.dot(a, b, preferred_element_type=F32)


def _dot_nt(a, b):
    return lax.dot_general(a, b, (((1,), (1,)), ((), ())), preferred_element_type=F32)


def _head_slice(hh):
    return slice(hh * DN_HEAD_DIM, (hh + 1) * DN_HEAD_DIM)


B_AHEAD = (1.5, 1.0)


def _interleave(*stages, speeds=None):
    speeds = speeds or [1.0] * len(stages)
    clock = [0.0] * len(stages)
    live = list(range(len(stages)))
    while live:
        i = min(live, key=lambda s: clock[s])
        try:
            clock[i] += next(stages[i]) / speeds[i]
        except StopIteration:
            live.remove(i)


def _carry_history(j, length, items, first_tile=None):
    @pl.when(j == 0)
    def _():
        if first_tile is not None:
            first_tile()
        for ext, state_ref, hist, nbuf in items:
            ext[:, 0:hist - nbuf, :] = jnp.zeros((ext.shape[0], hist - nbuf, ext.shape[2]), F32)
            ext[:, hist - nbuf:hist, :] = state_ref[...]

    @pl.when(j != 0)
    def _():
        for ext, state_ref, hist, nbuf in items:
            ext[:, hist - nbuf:hist, :] = ext[:, hist + length - nbuf:hist + length, :]


def _unit_lower_inverses(na16, t16, log2c, out):
    heads = range(len(na16))
    n = na16[0].shape[0]
    row = lax.broadcasted_iota(jnp.int32, (n, n), 0)
    col = lax.broadcasted_iota(jnp.int32, (n, n), 1)
    for lvl in range(1, log2c):
        blk = 1 << lvl
        if blk < BF16_ROWS:
            off = (((row >> (lvl + 1)) == (col >> (lvl + 1)))
                   & (((row >> lvl) & 1) == 1) & (((col >> lvl) & 1) == 0))
            p16 = [_dot(na16[i], t16[i]).astype(BF16) for i in heads]
            t16 = [t16[i] + jnp.where(off, _dot(t16[i], p16[i]), 0.0).astype(BF16) for i in heads]
        else:
            pairs = n // (2 * blk)
            rowh = lax.broadcasted_iota(jnp.int32, (n // 2, n), 0)
            colh = lax.broadcasted_iota(jnp.int32, (n // 2, n), 1)
            offh = ((colh >> (lvl + 1)) == (rowh >> lvl)) & (((colh >> lvl) & 1) == 0)
            t4 = [t16[i].reshape(pairs, 2, blk, n) for i in heads]
            na_odd = [na16[i].reshape(pairs, 2, blk, n)[:, 1].reshape(n // 2, n) for i in heads]
            t_odd = [t4[i][:, 1].reshape(n // 2, n) for i in heads]
            p = [_dot(na_odd[i], t16[i]) for i in heads]
            p16 = [jnp.broadcast_to(p[i].reshape(pairs, 1, blk, n), (pairs, 2, blk, n))
                   .reshape(n, n).astype(BF16) for i in heads]
            x_odd = [jnp.where(offh, _dot(t_odd[i], p16[i]), 0.0).astype(BF16) for i in heads]
            t16 = [jnp.concatenate([t4[i][:, 0:1], (t_odd[i] + x_odd[i]).reshape(pairs, 1, blk, n)],
                                   axis=1).reshape(n, n) for i in heads]
        yield 0.6
    out.extend(t16)


def _stage_a(j, r, st, *, nb, length, chunk, pos0):
    rows = nb * length
    log2c = int(math.log2(chunk))

    x = r.x[...].reshape(rows, D_MODEL)
    h16 = (_rms_scale(x) * r.nmix[...]).astype(BF16)

    gates_t = _dot_nt(r.wgt[...], h16)
    g_t = -jnp.exp(r.alog[...]) * _softplus(gates_t + r.dtb[...])
    beta_t = 1.0 / (1.0 + jnp.exp(-gates_t))
    pos_in_chunk = lax.broadcasted_iota(jnp.int32, (SUBLANES, rows), 1) & (chunk - 1)
    gc_t = g_t
    sfx_t = g_t
    step = 1
    while step < chunk:
        gc_t = gc_t + jnp.where(pos_in_chunk >= step, pltpu.roll(gc_t, step, axis=1), 0.0)
        sfx_t = sfx_t + jnp.where(pos_in_chunk + step < chunk,
                                  pltpu.roll(sfx_t, rows - step, axis=1), 0.0)
        step *= 2
    rest_t = sfx_t - g_t
    sub = lax.broadcasted_iota(jnp.int32, (SUBLANES, rows), 0)
    cols = jnp.concatenate(
        [jnp.where(sub < DN_HEADS, gc_t, -beta_t), jnp.exp(gc_t), jnp.exp(rest_t), jnp.exp(gc_t + rest_t),
         jnp.zeros((LANES - 4 * SUBLANES, rows), F32)], axis=0).T

    def colb(lane, width=LANES):
        return jnp.broadcast_to(cols[:, lane:lane + 1], (rows, width))

    yield 1.0
    for grp in range(DN_HEADS // HEAD_GROUP):
        pq = _dot(h16, r.win[:, W_QKV + grp * QKV_GROUP:W_QKV + (grp + 1) * QKV_GROUP])
        for i in range(HEAD_GROUP):
            for part in range(3):
                src = (3 * i + part) * DN_HEAD_DIM
                dst = (part * DN_HEADS + grp * HEAD_GROUP + i) * DN_HEAD_DIM
                r.extd[:, CONV_HIST:CONV_HIST + length, dst:dst + DN_HEAD_DIM] = (
                    pq[:, src:src + DN_HEAD_DIM].reshape(nb, length, DN_HEAD_DIM))
        yield 0.45
    r.odc[...] = r.extd[:, CONV_HIST + length - (DN_CONV - 1):CONV_HIST + length, :]

    def conv_block(ci):
        cs = slice(ci * LANES, (ci + 1) * LANES)
        base = CONV_HIST - (DN_CONV - 1)
        acc = r.extd[:, base:base + length, cs] * r.dnw[0:1, cs][None]
        for i in range(1, DN_CONV):
            acc = acc + r.extd[:, base + i:base + i + length, cs] * r.dnw[i:i + 1, cs][None]
        return _silu(acc).reshape(rows, LANES)

    row = lax.broadcasted_iota(jnp.int32, (rows, rows), 0)
    col = lax.broadcasted_iota(jnp.int32, (rows, rows), 1)
    same_chunk = (row >> log2c) == (col >> log2c)
    m_incl = same_chunk & (row >= col)
    m_strict = same_chunk & (row > col)
    pair = ((row >> 1) == (col >> 1)) & ((row & 1) == 1) & ((col & 1) == 0)
    eye = jnp.where(row == col, 1.0, 0.0).astype(F32)

    for hh in range(DN_HEADS):
        nbeta_b = colb(L_BETA + hh)
        egc_b = colb(L_EGC + hh)
        qc = conv_block(hh)
        kc = conv_block(DN_HEADS + hh)
        vc = conv_block(2 * DN_HEADS + hh)
        q = qc * (lax.rsqrt(jnp.sum(qc * qc, axis=-1, keepdims=True) + EPS) * (DN_HEAD_DIM ** -0.5))
        k = kc * lax.rsqrt(jnp.sum(kc * kc, axis=-1, keepdims=True) + EPS)
        nkb = k * nbeta_b
        k16 = k.astype(BF16)
        yield 0.6
        diff = colb(L_GC + hh, rows) - gc_t[hh:hh + 1, :]
        decay = jnp.where(m_incl, jnp.exp(jnp.where(m_incl, diff, 0.0)), 0.0)
        na = jnp.where(m_strict, _dot_nt(nkb.astype(BF16), k16) * decay, 0.0)
        st.na[hh] = na.astype(BF16)
        st.t0[hh] = jnp.where(pair, na, eye).astype(BF16)
        st.qk[hh] = (_dot_nt(q.astype(BF16), k16) * decay).astype(BF16)
        st.rhs[hh] = jnp.concatenate([vc * nbeta_b, nkb * egc_b], axis=1).astype(BF16)
        st.qg[hh] = q * egc_b
        st.kdt[hh] = (k * colb(L_KDF + hh)).T
        st.egl[hh] = colb(L_EGL + hh)
        yield 0.5

    prest = _dot(h16, r.win[:, W_REST:W_REST + REST_WIDTH])
    st.z[...] = _dot(h16, r.win[:, W_Z:W_Z + DN_WIDTH])
    p_in = prest[:, 0:POOL_WIDTH]
    sc_x = prest[:, POOL_WIDTH:POOL_WIDTH + SC_WIDTH]
    sc_b = prest[:, POOL_WIDTH + SC_WIDTH:POOL_WIDTH + 2 * SC_WIDTH]
    sc_c = prest[:, POOL_WIDTH + 2 * SC_WIDTH:]
    yield 0.5

    ph = POOL_HIST
    r.extp[:, ph:ph + length, :] = p_in.reshape(nb, length, POOL_WIDTH)
    r.op[...] = r.extp[:, ph + length - POOL_BUF:ph + length, :]
    zero8 = jnp.zeros((nb, SUBLANES, POOL_WIDTH), F32)
    end = ph + length
    r.s2[:, 0:8, :] = zero8
    r.s4[:, 0:8, :] = zero8
    r.s8[:, 0:8, :] = zero8
    r.s2[:, 8:end, :] = r.extp[:, 8:end, :] + r.extp[:, 7:end - 1, :]
    r.s4[:, 8:end, :] = r.s2[:, 8:end, :] + r.s2[:, 6:end - 2, :]
    r.s8[:, 8:end, :] = r.s4[:, 8:end, :] + r.s4[:, 4:end - 4, :]
    s16 = r.s8[:, ph:end, :] + r.s8[:, ph - 8:end - 8, :]
    lane3 = lax.broadcasted_iota(jnp.int32, (nb, length, POOL_WIDTH), 2)
    grp = lane3 // POOL_GROUP_DIM
    wsum = jnp.where(grp == 0, r.s2[:, ph:end, :],
                     jnp.where(grp == 1, r.s4[:, ph:end, :],
                               jnp.where(grp == 2, r.s8[:, ph:end, :], s16)))
    win = jnp.where(grp == 0, POOL_WINDOWS[0],
                    jnp.where(grp == 1, POOL_WINDOWS[1],
                              jnp.where(grp == 2, POOL_WINDOWS[2], POOL_WINDOWS[3])))
    tpos = lax.broadcasted_iota(jnp.int32, (nb, length, POOL_WIDTH), 1) + (pos0 + j * length)
    cnt = jnp.minimum(tpos + 1, win).astype(F32)
    dpool = (wsum / cnt - r.extp[:, ph:end, :]).reshape(rows, POOL_WIDTH)
    r.cat_a[:, 0:POOL_WIDTH] = (_dot(dpool.astype(BF16), r.pw[...]) * r.pscale[...]).astype(BF16)
    yield 0.4

    r.exts[:, CONV_HIST:CONV_HIST + length, :] = (sc_c * sc_x).reshape(nb, length, SC_WIDTH)
    r.osc[...] = r.exts[:, CONV_HIST + length - (SC_CONV - 1):CONV_HIST + length, :]
    sbase = CONV_HIST - (SC_CONV - 1)
    cconv = r.exts[:, sbase:sbase + length, :] * r.scw[0:1, :][None]
    for i in range(1, SC_CONV):
        cconv = cconv + r.exts[:, sbase + i:sbase + i + length, :] * r.scw[i:i + 1, :][None]
    r.cat_a[:, POOL_WIDTH:] = (sc_b * cconv.reshape(rows, SC_WIDTH)).astype(BF16)

    st.xy[...] = x + _dot(r.cat_a[...], r.wout[DN_WIDTH:, :])


def _stage_b(r, st, *, nb, length, chunk):
    rows = nb * length
    log2c = int(math.log2(chunk))
    n_chunks = rows // chunk

    tinv = []
    yield from _unit_lower_inverses([st.na[hh] for hh in range(DN_HEADS)],
                                    [st.t0[hh] for hh in range(DN_HEADS)], log2c, tinv)
    for hh in range(DN_HEADS):
        nsol = _dot(tinv[hh], st.rhs[hh])
        r.nu[hh] = nsol[:, :DN_HEAD_DIM]
        r.nw[hh] = nsol[:, DN_HEAD_DIM:]
    yield 0.5

    if n_chunks == 1:
        for hh in range(DN_HEADS):
            s = r.od[0, hh]
            nwq = jnp.concatenate([r.nw[hh], st.qg[hh]], axis=0).astype(BF16)
            res = _dot(nwq, s.astype(BF16))
            vn16 = (res[:rows] - r.nu[hh]).astype(BF16)
            r.o[:, _head_slice(hh)] = res[rows:] + _dot(st.qk[hh], vn16)
            r.od[0, hh] = s * st.egl[hh, 0:1, :] + _dot(st.kdt[hh].astype(BF16), vn16)
            yield 0.3
    else:
        assert n_chunks == nb
        seg = lax.broadcasted_iota(jnp.int32, (nb, 1, rows), 2) >> log2c
        own = seg == lax.broadcasted_iota(jnp.int32, (nb, 1, rows), 0)
        for hh in range(DN_HEADS):
            vn_parts, qs_parts = [], []
            for b in range(nb):
                rs = slice(b * chunk, (b + 1) * chunk)
                nwq = jnp.concatenate([r.nw[hh, rs, :], st.qg[hh, rs, :]], axis=0).astype(BF16)
                res = _dot(nwq, r.od[b, hh].astype(BF16))
                vn_parts.append(res[:chunk] - r.nu[hh, rs, :])
                qs_parts.append(res[chunk:])
            vn16 = jnp.concatenate(vn_parts, axis=0).astype(BF16)
            r.o[:, _head_slice(hh)] = jnp.concatenate(qs_parts, axis=0) + _dot(st.qk[hh], vn16)
            kd_own = jnp.where(own, st.kdt[hh][None], 0.0).astype(BF16).reshape(nb * DN_HEAD_DIM, rows)
            upd = _dot(kd_own, vn16).reshape(nb, DN_HEAD_DIM, DN_HEAD_DIM)
            egl = st.egl[hh].reshape(nb, chunk, DN_HEAD_DIM)[:, 0:1, :]
            r.od[:, hh, :, :] = r.od[:, hh, :, :] * egl + upd
            yield 0.3

    for hh in range(DN_HEADS):
        hs = _head_slice(hh)
        r.cat_b[:, hs] = (_rms_scale(r.o[:, hs]) * r.onorm[...] * _silu(st.z[:, hs])).astype(BF16)
    y = st.xy[...] + _dot(r.cat_b[...], r.wout[:DN_WIDTH, :])
    r.xo[...] = y.reshape(nb, length, D_MODEL)


_MIXER_IN = ("x", "sd", "sdc", "sp", "ssc", "nmix", "wgt", "win", "dnw", "alog", "dtb", "onorm", "pw",
             "pscale", "scw", "wout")
_MIXER_OUT = ("xo", "od", "odc", "op", "osc")
_MIXER_PRIVATE = ("extd", "extp", "s2", "s4", "s8", "exts", "cat_a", "nu", "nw", "o", "cat_b")
_MIXER_SET = ("na", "t0", "qk", "rhs", "qg", "kdt", "egl", "z", "xy")


def _mixer_scratch(nb, length, n_sets):
    rows = nb * length
    private = [
        pltpu.VMEM((nb, CONV_HIST + length, 3 * DN_WIDTH), F32),
        pltpu.VMEM((nb, POOL_HIST + length, POOL_WIDTH), F32),
        pltpu.VMEM((nb, POOL_HIST + length, POOL_WIDTH), F32),
        pltpu.VMEM((nb, POOL_HIST + length, POOL_WIDTH), F32),
        pltpu.VMEM((nb, POOL_HIST + length, POOL_WIDTH), F32),
        pltpu.VMEM((nb, CONV_HIST + length, SC_WIDTH), F32),
        pltpu.VMEM((rows, D_REST), BF16),
        pltpu.VMEM((DN_HEADS, rows, DN_HEAD_DIM), F32),
        pltpu.VMEM((DN_HEADS, rows, DN_HEAD_DIM), F32),
        pltpu.VMEM((rows, DN_WIDTH), F32),
        pltpu.VMEM((rows, DN_WIDTH), BF16),
    ]
    one_set = [
        pltpu.VMEM((DN_HEADS, rows, rows), BF16),
        pltpu.VMEM((DN_HEADS, rows, rows), BF16),
        pltpu.VMEM((DN_HEADS, rows, rows), BF16),
        pltpu.VMEM((DN_HEADS, rows, 2 * DN_HEAD_DIM), BF16),
        pltpu.VMEM((DN_HEADS, rows, DN_HEAD_DIM), F32),
        pltpu.VMEM((DN_HEADS, DN_HEAD_DIM, rows), F32),
        pltpu.VMEM((DN_HEADS, rows, DN_HEAD_DIM), F32),
        pltpu.VMEM((rows, DN_WIDTH), F32),
        pltpu.VMEM((rows, D_MODEL), F32),
    ]
    return private + one_set * n_sets


def _mixer_kernel(*refs, nb, length, chunk, pos0, n_tiles, n_alias):
    n_in, n_set = len(_MIXER_IN), len(_MIXER_SET)
    refs = refs[:n_in] + refs[n_in + n_alias:]
    names = _MIXER_IN + _MIXER_OUT + _MIXER_PRIVATE
    r = types.SimpleNamespace(**dict(zip(names, refs[:len(names)])))
    set_refs = refs[len(names):]
    sets = [types.SimpleNamespace(**dict(zip(_MIXER_SET, set_refs[i * n_set:(i + 1) * n_set])))
            for i in range(len(set_refs) // n_set)]
    j = pl.program_id(1)

    def init_state():
        r.od[...] = r.sd[...]

    _carry_history(j, length,
                   [(r.extd, r.sdc, CONV_HIST, DN_CONV - 1), (r.extp, r.sp, POOL_HIST, POOL_BUF),
                    (r.exts, r.ssc, CONV_HIST, SC_CONV - 1)], first_tile=init_state)

    stage_a = functools.partial(_stage_a, j, r, nb=nb, length=length, chunk=chunk, pos0=pos0)
    stage_b = functools.partial(_stage_b, r, nb=nb, length=length, chunk=chunk)

    if len(sets) == 1:
        _interleave(stage_a(sets[0]))
        _interleave(stage_b(sets[0]))
        return

    middle = (j > 0) & (j < n_tiles)

    @pl.when(j == 0)
    def _():
        _interleave(stage_a(sets[0]))

    @pl.when(middle & (j % 2 == 1))
    def _():
        _interleave(stage_b(sets[0]), stage_a(sets[1]), speeds=B_AHEAD)

    @pl.when(middle & (j % 2 == 0))
    def _():
        _interleave(stage_b(sets[1]), stage_a(sets[0]), speeds=B_AHEAD)

    @pl.when(j == n_tiles)
    def _():
        _interleave(stage_b(sets[(n_tiles - 1) % 2]))


def _ffn_kernel(x_ref, sf_ref, nffn_ref, wg_ref, fw_ref, wu_ref, wd_ref, fnorm_ref, *rest,
                nb, length, final):
    xo_ref, of_ref, extf_s = rest[-3:]
    j = pl.program_id(1)
    rows = nb * length
    _carry_history(j, length, [(extf_s, sf_ref, CONV_HIST, FFN_CONV - 1)])
    x = x_ref[...].reshape(rows, D_MODEL)
    h16 = (_rms_scale(x) * nffn_ref[...]).astype(BF16)
    extf_s[:, CONV_HIST:CONV_HIST + length, :] = _dot(h16, wg_ref[...]).reshape(nb, length, D_FF)
    of_ref[...] = extf_s[:, CONV_HIST + length - (FFN_CONV - 1):CONV_HIST + length, :]
    base = CONV_HIST - (FFN_CONV - 1)
    gate = extf_s[:, base:base + length, :] * fw_ref[0:1, :][None]
    for i in range(1, FFN_CONV):
        gate = gate + extf_s[:, base + i:base + i + length, :] * fw_ref[i:i + 1, :][None]
    up = _dot(h16, wu_ref[...])
    act = (_silu(gate).reshape(rows, D_FF) * up).astype(BF16)
    y = x + _dot(act, wd_ref[...])
    if final:
        y = _rms_scale(y) * fnorm_ref[...]
    xo_ref[...] = y.reshape(nb, length, D_MODEL)


def _layer_spec(arr, layer):
    tail = arr.shape[1:]
    zeros = (0,) * len(tail)
    return pl.BlockSpec((None,) + tail, lambda i, j: (layer,) + zeros, pipeline_mode=pl.Buffered(1))


def _state_spec(arr, layer, nb):
    tail = arr.shape[2:]
    zeros = (0,) * len(tail)
    return pl.BlockSpec((None, nb) + tail, lambda i, j: (layer, i) + zeros)


def _state_out_spec(shape, layer, nb):
    tail = shape[2:]
    zeros = (0,) * len(tail)
    return pl.BlockSpec((None, nb) + tail, lambda i, j: (layer, i) + zeros)


def _alias_args(prev, n_in, first_out):
    if prev is None:
        return [], [], {}
    specs = [pl.BlockSpec(memory_space=pl.ANY)] * len(prev)
    return list(prev), specs, {n_in + k: first_out + k for k in range(len(prev))}


def _mixer_call(x, states, state_layer, layer, depth, w, prev, *, nb, length, chunk, pos0):
    batch, seq, _ = x.shape
    sd, sdc, sp, ssc = states
    n_tiles = seq // length
    pipelined = n_tiles > 1
    if pipelined:
        last = n_tiles - 1
        grid = (batch // nb, n_tiles + 1)
        x_in_spec = pl.BlockSpec((nb, length, D_MODEL), lambda i, j: (i, jnp.minimum(j, last), 0))
        x_out_spec = pl.BlockSpec((nb, length, D_MODEL), lambda i, j: (i, jnp.maximum(j - 1, 0), 0))
    else:
        grid = (batch // nb, n_tiles)
        x_in_spec = x_out_spec = pl.BlockSpec((nb, length, D_MODEL), lambda i, j: (i, j, 0))
    weights = tuple(w[name] for name in _MIXER_IN[5:])
    out_shape = (
        jax.ShapeDtypeStruct(x.shape, F32),
        jax.ShapeDtypeStruct((depth, batch) + sd.shape[2:], F32),
        jax.ShapeDtypeStruct((depth, batch) + sdc.shape[2:], F32),
        jax.ShapeDtypeStruct((depth, batch) + sp.shape[2:], F32),
        jax.ShapeDtypeStruct((depth, batch) + ssc.shape[2:], F32),
    )
    operands = (x, sd, sdc, sp, ssc) + weights
    alias_ops, alias_specs, aliases = _alias_args(prev, len(operands), 1)
    kern = functools.partial(_mixer_kernel, nb=nb, length=length, chunk=chunk, pos0=pos0, n_tiles=n_tiles,
                             n_alias=len(alias_ops))
    return pl.pallas_call(
        kern,
        grid=grid,
        in_specs=[x_in_spec, _state_spec(sd, state_layer, nb), _state_spec(sdc, state_layer, nb),
                  _state_spec(sp, state_layer, nb), _state_spec(ssc, state_layer, nb)]
                 + [_layer_spec(a, layer) for a in weights] + alias_specs,
        out_specs=(x_out_spec,) + tuple(_state_out_spec(s.shape, layer, nb) for s in out_shape[1:]),
        out_shape=out_shape,
        input_output_aliases=aliases,
        scratch_shapes=_mixer_scratch(nb, length, 2 if pipelined else 1),
        compiler_params=pltpu.CompilerParams(
            dimension_semantics=("parallel", "arbitrary"), vmem_limit_bytes=VMEM_LIMIT_BYTES),
        name="mixer",
    )(*operands, *alias_ops)


def _ffn_call(x, sf, state_layer, layer, depth, w, prev, *, nb, length, final):
    batch, seq, _ = x.shape
    grid = (batch // nb, seq // length)
    x_spec = pl.BlockSpec((nb, length, D_MODEL), lambda i, j: (i, j, 0))
    weights = (w["nffn"], w["wg"], w["fw"], w["wu"], w["wd"])
    fnorm = w["fnorm"]
    out_shape = (jax.ShapeDtypeStruct(x.shape, F32),
                 jax.ShapeDtypeStruct((depth, batch) + sf.shape[2:], F32))
    operands = (x, sf) + weights + (fnorm,)
    alias_ops, alias_specs, aliases = _alias_args(prev, len(operands), 1)
    kern = functools.partial(_ffn_kernel, nb=nb, length=length, final=final)
    return pl.pallas_call(
        kern,
        grid=grid,
        in_specs=[x_spec, _state_spec(sf, state_layer, nb)] + [_layer_spec(a, layer) for a in weights]
                 + [pl.BlockSpec(fnorm.shape, lambda i, j: (0, 0))] + alias_specs,
        out_specs=(x_spec, _state_out_spec(out_shape[1].shape, layer, nb)),
        out_shape=out_shape,
        input_output_aliases=aliases,
        scratch_shapes=[pltpu.VMEM((nb, CONV_HIST + length, D_FF), F32)],
        compiler_params=pltpu.CompilerParams(
            dimension_semantics=("parallel", "arbitrary"), vmem_limit_bytes=VMEM_LIMIT_BYTES),
        name="ffn",
    )(*operands, *alias_ops)


def _prepare_weights(norm_mix, w_in, dn_conv_w, dn_a_log, dn_dt_bias, dn_out_norm, pool_w, pool_scale,
                     sconv_w, w_out, norm_ffn, w_ffn_gate, ffn_conv_w, w_ffn_up, w_ffn_down, final_norm):
    depth = w_in.shape[0]
    assert N_GATE == SUBLANES
    qkv0, z0 = 0, 3 * DN_WIDTH
    g0 = z0 + DN_WIDTH
    rest0 = g0 + N_GATE
    w16 = w_in.astype(BF16)
    wgt = jnp.swapaxes(w16[:, :, g0:g0 + N_GATE], 1, 2)
    qkv = w16[:, :, qkv0:qkv0 + 3 * DN_WIDTH].reshape(depth, D_MODEL, 3, DN_HEADS, DN_HEAD_DIM)
    qkv = jnp.swapaxes(qkv, 2, 3).reshape(depth, D_MODEL, 3 * DN_WIDTH)
    win = jnp.concatenate([qkv, w16[:, :, rest0:rest0 + REST_WIDTH], w16[:, :, z0:z0 + DN_WIDTH]], axis=2)
    pad = jnp.zeros((depth, SUBLANES - DN_HEADS, 1), F32)
    eye = jnp.eye(len(POOL_WINDOWS), dtype=F32)
    pw = (pool_w[:, :, :, None, :] * eye[None, :, None, :, None]).reshape(depth, POOL_WIDTH, POOL_WIDTH)
    return {
        "nmix": norm_mix[:, None, :],
        "wgt": wgt,
        "win": win,
        "dnw": dn_conv_w,
        "alog": jnp.concatenate([dn_a_log[:, :, None], pad], axis=1),
        "dtb": jnp.concatenate([dn_dt_bias[:, :, None], pad], axis=1),
        "onorm": dn_out_norm[:, None, :],
        "pw": pw.astype(BF16),
        "pscale": pool_scale[:, None, :],
        "scw": sconv_w,
        "wout": w_out.astype(BF16),
        "nffn": norm_ffn[:, None, :],
        "wg": w_ffn_gate.astype(BF16),
        "fw": ffn_conv_w,
        "wu": w_ffn_up.astype(BF16),
        "wd": w_ffn_down.astype(BF16),
        "fnorm": final_norm[None, :],
    }


def _trunk(x, states, state_layers, weights, depth, *, nb_mix, nb_ffn, length, chunk, pos0):
    sd, sdc, sp, ssc, sf = states
    mix_out, ffn_out = None, None
    for l in range(depth):
        sl = state_layers[l]
        x, *mix_out = _mixer_call(x, (sd, sdc, sp, ssc), sl, l, depth, weights, mix_out,
                                  nb=nb_mix, length=length, chunk=chunk, pos0=pos0)
        x, *ffn_out = _ffn_call(x, sf, sl, l, depth, weights, ffn_out,
                                nb=nb_ffn, length=length, final=(l == depth - 1))
    return x, tuple(mix_out) + tuple(ffn_out)


PROMPT_TILE = 256
SAMPLE_MIX_ROWS = 16
SAMPLE_FFN_ROWS = 32


def kernel(x_prompt, x_sample, state_delta, state_delta_conv, state_pool, state_sconv, state_ffn_conv,
           norm_mix, w_in, dn_conv_w, dn_a_log, dn_dt_bias, dn_out_norm, pool_w, pool_scale,
           sconv_w, w_out, norm_ffn, w_ffn_gate, ffn_conv_w, w_ffn_up, w_ffn_down, final_norm):
    depth = w_in.shape[0]
    weights = _prepare_weights(norm_mix, w_in, dn_conv_w, dn_a_log, dn_dt_bias, dn_out_norm, pool_w,
                               pool_scale, sconv_w, w_out, norm_ffn, w_ffn_gate, ffn_conv_w, w_ffn_up,
                               w_ffn_down, final_norm)

    batch, seq, _ = x_prompt.shape
    dec_batch, dec_seq, _ = x_sample.shape
    sample_states = (state_delta, state_delta_conv, state_pool, state_sconv, state_ffn_conv)
    zero_states = tuple(jnp.zeros((1, batch) + s.shape[2:], F32) for s in sample_states)

    y_prompt, p_st = _trunk(x_prompt, zero_states, (0,) * depth, weights, depth,
                            nb_mix=1, nb_ffn=1, length=min(PROMPT_TILE, seq),
                            chunk=min(PROMPT_TILE, seq), pos0=0)
    y_sample, s_st = _trunk(x_sample, sample_states, tuple(range(depth)), weights, depth,
                            nb_mix=SAMPLE_MIX_ROWS, nb_ffn=SAMPLE_FFN_ROWS, length=dec_seq,
                            chunk=min(DN_CHUNK, dec_seq), pos0=PAST_LEN)
    return (y_prompt, y_sample) + p_st + s_st
```

```python
import functools
import math
import types

import jax
import jax.numpy as jnp
from jax import lax
from jax.experimental import pallas as pl
from jax.experimental.pallas import tpu as pltpu

F32 = jnp.float32
BF16 = jnp.bfloat16

LANES = 128
SUBLANES = 8
BF16_ROWS = 16
VMEM_LIMIT_BYTES = 56 * 1024 * 1024

D_MODEL = 1024
DN_HEADS = 4
DN_HEAD_DIM = 128
DN_WIDTH = DN_HEADS * DN_HEAD_DIM
DN_CONV = 4
DN_CHUNK = 64
POOL_WINDOWS = (2, 4, 8, 16)
POOL_WIDTH = 256
POOL_GROUP_DIM = POOL_WIDTH // len(POOL_WINDOWS)
POOL_BUF = max(POOL_WINDOWS) - 1
SC_WIDTH = 256
SC_CONV = 3
D_FF = 2816
FFN_CONV = 3
EPS = 1e-6
PAST_LEN = 16384

N_GATE = 2 * DN_HEADS
W_Z = 3 * DN_WIDTH
REST_WIDTH = POOL_WIDTH + 3 * SC_WIDTH
D_REST = POOL_WIDTH + SC_WIDTH

CONV_HIST = SUBLANES
POOL_HIST = 24

L_GC, L_BETA, L_EGC, L_KDF, L_EGL = 0, DN_HEADS, SUBLANES, 2 * SUBLANES, 3 * SUBLANES


def _silu(t):
    return t / (1.0 + jnp.exp(-t))


def _softplus(t):
    return jnp.maximum(t, 0.0) + jnp.log1p(jnp.exp(-jnp.abs(t)))


def _rms_scale(t):
    return t * lax.rsqrt(jnp.mean(t * t, axis=-1, keepdims=True) + EPS)


def _dot(a, b):
    return jnp.dot(a, b, preferred_element_type=F32)


def _dot_nt(a, b):
    return lax.dot_general(a, b, (((1,), (1,)), ((), ())), preferred_element_type=F32)


def _head_slice(hh):
    return slice(hh * DN_HEAD_DIM, (hh + 1) * DN_HEAD_DIM)


B_AHEAD = (1.5, 1.0)


def _interleave(*stages, speeds=None):
    speeds = speeds or [1.0] * len(stages)
    clock = [0.0] * len(stages)
    live = list(range(len(stages)))
    while live:
        i = min(live, key=lambda s: clock[s])
        try:
            clock[i] += next(stages[i]) / speeds[i]
        except StopIteration:
            live.remove(i)


def _carry_history(j, length, items, first_tile=None):
    @pl.when(j == 0)
    def _():
        if first_tile is not None:
            first_tile()
        for ext, state_ref, hist, nbuf in items:
            ext[:, 0:hist - nbuf, :] = jnp.zeros((ext.shape[0], hist - nbuf, ext.shape[2]), F32)
            ext[:, hist - nbuf:hist, :] = state_ref[...]

    @pl.when(j != 0)
    def _():
        for ext, state_ref, hist, nbuf in items:
            ext[:, hist - nbuf:hist, :] = ext[:, hist + length - nbuf:hist + length, :]


def _unit_lower_inverses(na16, t16, log2c, out):
    heads = range(len(na16))
    n = na16[0].shape[0]
    row = lax.broadcasted_iota(jnp.int32, (n, n), 0)
    col = lax.broadcasted_iota(jnp.int32, (n, n), 1)
    for lvl in range(1, log2c):
        blk = 1 << lvl
        if blk < BF16_ROWS:
            off = (((row >> (lvl + 1)) == (col >> (lvl + 1)))
                   & (((row >> lvl) & 1) == 1) & (((col >> lvl) & 1) == 0))
            p16 = [_dot(na16[i], t16[i]).astype(BF16) for i in heads]
            t16 = [t16[i] + jnp.where(off, _dot(t16[i], p16[i]), 0.0).astype(BF16) for i in heads]
        else:
            pairs = n // (2 * blk)
            rowh = lax.broadcasted_iota(jnp.int32, (n // 2, n), 0)
            colh = lax.broadcasted_iota(jnp.int32, (n // 2, n), 1)
            offh = ((colh >> (lvl + 1)) == (rowh >> lvl)) & (((colh >> lvl) & 1) == 0)
            t4 = [t16[i].reshape(pairs, 2, blk, n) for i in heads]
            na_odd = [na16[i].reshape(pairs, 2, blk, n)[:, 1].reshape(n // 2, n) for i in heads]
            t_odd = [t4[i][:, 1].reshape(n // 2, n) for i in heads]
            p = [_dot(na_odd[i], t16[i]) for i in heads]
            p16 = [jnp.broadcast_to(p[i].reshape(pairs, 1, blk, n), (pairs, 2, blk, n))
                   .reshape(n, n).astype(BF16) for i in heads]
            x_odd = [jnp.where(offh, _dot(t_odd[i], p16[i]), 0.0).astype(BF16) for i in heads]
            t16 = [jnp.concatenate([t4[i][:, 0:1], (t_odd[i] + x_odd[i]).reshape(pairs, 1, blk, n)],
                                   axis=1).reshape(n, n) for i in heads]
        yield 0.6
    out.extend(t16)


def _stage_a(j, r, st, *, nb, length, chunk, pos0):
    rows = nb * length
    log2c = int(math.log2(chunk))

    x = r.x[...].reshape(rows, D_MODEL)
    h16 = (_rms_scale(x) * r.nmix[...]).astype(BF16)

    gates_t = _dot_nt(r.wgt[...], h16)
    g_t = -jnp.exp(r.alog[...]) * _softplus(gates_t + r.dtb[...])
    beta_t = 1.0 / (1.0 + jnp.exp(-gates_t))
    pos_in_chunk = lax.broadcasted_iota(jnp.int32, (SUBLANES, rows), 1) & (chunk - 1)
    gc_t = g_t
    sfx_t = g_t
    step = 1
    while step < chunk:
        gc_t = gc_t + jnp.where(pos_in_chunk >= step, pltpu.roll(gc_t, step, axis=1), 0.0)
        sfx_t = sfx_t + jnp.where(pos_in_chunk + step < chunk,
                                  pltpu.roll(sfx_t, rows - step, axis=1), 0.0)
        step *= 2
    rest_t = sfx_t - g_t
    sub = lax.broadcasted_iota(jnp.int32, (SUBLANES, rows), 0)
    cols = jnp.concatenate(
        [jnp.where(sub < DN_HEADS, gc_t, -beta_t), jnp.exp(gc_t), jnp.exp(rest_t), jnp.exp(gc_t + rest_t),
         jnp.zeros((LANES - 4 * SUBLANES, rows), F32)], axis=0).T

    def colb(lane, width=LANES):
        return jnp.broadcast_to(cols[:, lane:lane + 1], (rows, width))

    yield 1.0
    for part in range(3):
        cs = slice(part * DN_WIDTH, (part + 1) * DN_WIDTH)
        r.extd[:, CONV_HIST:CONV_HIST + length, cs] = _dot(h16, r.wqkvz[:, cs]).reshape(nb, length, DN_WIDTH)
        yield 0.3
    r.odc[...] = r.extd[:, CONV_HIST + length - (DN_CONV - 1):CONV_HIST + length, :]

    def conv_block(ci):
        cs = slice(ci * LANES, (ci + 1) * LANES)
        base = CONV_HIST - (DN_CONV - 1)
        acc = r.extd[:, base:base + length, cs] * r.dnw[0:1, cs][None]
        for i in range(1, DN_CONV):
            acc = acc + r.extd[:, base + i:base + i + length, cs] * r.dnw[i:i + 1, cs][None]
        return _silu(acc).reshape(rows, LANES)

    row = lax.broadcasted_iota(jnp.int32, (rows, rows), 0)
    col = lax.broadcasted_iota(jnp.int32, (rows, rows), 1)
    same_chunk = (row >> log2c) == (col >> log2c)
    m_incl = same_chunk & (row >= col)
    m_strict = same_chunk & (row > col)
    pair = ((row >> 1) == (col >> 1)) & ((row & 1) == 1) & ((col & 1) == 0)
    eye = jnp.where(row == col, 1.0, 0.0).astype(F32)

    for hh in range(DN_HEADS):
        nbeta_b = colb(L_BETA + hh)
        egc_b = colb(L_EGC + hh)
        qc = conv_block(hh)
        kc = conv_block(DN_HEADS + hh)
        vc = conv_block(2 * DN_HEADS + hh)
        q = qc * (lax.rsqrt(jnp.sum(qc * qc, axis=-1, keepdims=True) + EPS) * (DN_HEAD_DIM ** -0.5))
        k = kc * lax.rsqrt(jnp.sum(kc * kc, axis=-1, keepdims=True) + EPS)
        nkb = k * nbeta_b
        k16 = k.astype(BF16)
        yield 0.6
        diff = colb(L_GC + hh, rows) - gc_t[hh:hh + 1, :]
        decay = jnp.where(m_incl, jnp.exp(jnp.where(m_incl, diff, 0.0)), 0.0)
        na = jnp.where(m_strict, _dot_nt(nkb.astype(BF16), k16) * decay, 0.0)
        st.na[hh] = na.astype(BF16)
        st.t0[hh] = jnp.where(pair, na, eye).astype(BF16)
        st.qk[hh] = (_dot_nt(q.astype(BF16), k16) * decay).astype(BF16)
        st.rhs[hh] = jnp.concatenate([vc * nbeta_b, nkb * egc_b], axis=1).astype(BF16)
        st.qg[hh] = q * egc_b
        st.kdt[hh] = (k * colb(L_KDF + hh)).T
        st.egl[hh] = colb(L_EGL + hh)
        yield 0.5

    prest = _dot(h16, r.wrest[...])
    st.z[...] = _dot(h16, r.wqkvz[:, W_Z:W_Z + DN_WIDTH])
    p_in = prest[:, 0:POOL_WIDTH]
    sc_x = prest[:, POOL_WIDTH:POOL_WIDTH + SC_WIDTH]
    sc_b = prest[:, POOL_WIDTH + SC_WIDTH:POOL_WIDTH + 2 * SC_WIDTH]
    sc_c = prest[:, POOL_WIDTH + 2 * SC_WIDTH:]
    yield 0.5

    ph = POOL_HIST
    r.extp[:, ph:ph + length, :] = p_in.reshape(nb, length, POOL_WIDTH)
    r.op[...] = r.extp[:, ph + length - POOL_BUF:ph + length, :]
    zero8 = jnp.zeros((nb, SUBLANES, POOL_WIDTH), F32)
    end = ph + length
    r.s2[:, 0:8, :] = zero8
    r.s4[:, 0:8, :] = zero8
    r.s8[:, 0:8, :] = zero8
    r.s2[:, 8:end, :] = r.extp[:, 8:end, :] + r.extp[:, 7:end - 1, :]
    r.s4[:, 8:end, :] = r.s2[:, 8:end, :] + r.s2[:, 6:end - 2, :]
    r.s8[:, 8:end, :] = r.s4[:, 8:end, :] + r.s4[:, 4:end - 4, :]
    s16 = r.s8[:, ph:end, :] + r.s8[:, ph - 8:end - 8, :]
    lane3 = lax.broadcasted_iota(jnp.int32, (nb, length, POOL_WIDTH), 2)
    grp = lane3 // POOL_GROUP_DIM
    wsum = jnp.where(grp == 0, r.s2[:, ph:end, :],
                     jnp.where(grp == 1, r.s4[:, ph:end, :],
                               jnp.where(grp == 2, r.s8[:, ph:end, :], s16)))
    win = jnp.where(grp == 0, POOL_WINDOWS[0],
                    jnp.where(grp == 1, POOL_WINDOWS[1],
                              jnp.where(grp == 2, POOL_WINDOWS[2], POOL_WINDOWS[3])))
    tpos = lax.broadcasted_iota(jnp.int32, (nb, length, POOL_WIDTH), 1) + (pos0 + j * length)
    cnt = jnp.minimum(tpos + 1, win).astype(F32)
    dpool = (wsum / cnt - r.extp[:, ph:end, :]).reshape(rows, POOL_WIDTH)
    r.cat_a[:, 0:POOL_WIDTH] = (_dot(dpool.astype(BF16), r.pw[...]) * r.pscale[...]).astype(BF16)
    yield 0.4

    r.exts[:, CONV_HIST:CONV_HIST + length, :] = (sc_c * sc_x).reshape(nb, length, SC_WIDTH)
    r.osc[...] = r.exts[:, CONV_HIST + length - (SC_CONV - 1):CONV_HIST + length, :]
    sbase = CONV_HIST - (SC_CONV - 1)
    cconv = r.exts[:, sbase:sbase + length, :] * r.scw[0:1, :][None]
    for i in range(1, SC_CONV):
        cconv = cconv + r.exts[:, sbase + i:sbase + i + length, :] * r.scw[i:i + 1, :][None]
    r.cat_a[:, POOL_WIDTH:] = (sc_b * cconv.reshape(rows, SC_WIDTH)).astype(BF16)

    st.xy[...] = x + _dot(r.cat_a[...], r.wout[DN_WIDTH:, :])


def _stage_b(r, st, *, nb, length, chunk):
    rows = nb * length
    log2c = int(math.log2(chunk))
    n_chunks = rows // chunk

    tinv = []
    yield from _unit_lower_inverses([st.na[hh] for hh in range(DN_HEADS)],
                                    [st.t0[hh] for hh in range(DN_HEADS)], log2c, tinv)
    for hh in range(DN_HEADS):
        nsol = _dot(tinv[hh], st.rhs[hh])
        r.nu[hh] = nsol[:, :DN_HEAD_DIM]
        r.nw[hh] = nsol[:, DN_HEAD_DIM:]
    yield 0.5

    if n_chunks == 1:
        for hh in range(DN_HEADS):
            s = r.od[0, hh]
            nwq = jnp.concatenate([r.nw[hh], st.qg[hh]], axis=0).astype(BF16)
            res = _dot(nwq, s.astype(BF16))
            vn16 = (res[:rows] - r.nu[hh]).astype(BF16)
            r.o[:, _head_slice(hh)] = res[rows:] + _dot(st.qk[hh], vn16)
            r.od[0, hh] = s * st.egl[hh, 0:1, :] + _dot(st.kdt[hh].astype(BF16), vn16)
            yield 0.3
    else:
        assert n_chunks == nb
        seg = lax.broadcasted_iota(jnp.int32, (nb, 1, rows), 2) >> log2c
        own = seg == lax.broadcasted_iota(jnp.int32, (nb, 1, rows), 0)
        for hh in range(DN_HEADS):
            vn_parts, qs_parts = [], []
            for b in range(nb):
                rs = slice(b * chunk, (b + 1) * chunk)
                nwq = jnp.concatenate([r.nw[hh, rs, :], st.qg[hh, rs, :]], axis=0).astype(BF16)
                res = _dot(nwq, r.od[b, hh].astype(BF16))
                vn_parts.append(res[:chunk] - r.nu[hh, rs, :])
                qs_parts.append(res[chunk:])
            vn16 = jnp.concatenate(vn_parts, axis=0).astype(BF16)
            r.o[:, _head_slice(hh)] = jnp.concatenate(qs_parts, axis=0) + _dot(st.qk[hh], vn16)
            kd_own = jnp.where(own, st.kdt[hh][None], 0.0).astype(BF16).reshape(nb * DN_HEAD_DIM, rows)
            upd = _dot(kd_own, vn16).reshape(nb, DN_HEAD_DIM, DN_HEAD_DIM)
            egl = st.egl[hh].reshape(nb, chunk, DN_HEAD_DIM)[:, 0:1, :]
            r.od[:, hh, :, :] = r.od[:, hh, :, :] * egl + upd
            yield 0.3

    for hh in range(DN_HEADS):
        hs = _head_slice(hh)
        r.cat_b[:, hs] = (_rms_scale(r.o[:, hs]) * r.onorm[...] * _silu(st.z[:, hs])).astype(BF16)
    y = st.xy[...] + _dot(r.cat_b[...], r.wout[:DN_WIDTH, :])
    r.xo[...] = y.reshape(nb, length, D_MODEL)


_MIXER_IN = ("x", "sd", "sdc", "sp", "ssc", "nmix", "wgt", "wqkvz", "wrest", "dnw", "alog", "dtb", "onorm", "pw",
             "pscale", "scw", "wout")
_MIXER_OUT = ("xo", "od", "odc", "op", "osc")
_MIXER_PRIVATE = ("extd", "extp", "s2", "s4", "s8", "exts", "cat_a", "nu", "nw", "o", "cat_b")
_MIXER_SET = ("na", "t0", "qk", "rhs", "qg", "kdt", "egl", "z", "xy")


def _mixer_scratch(nb, length, n_sets):
    rows = nb * length
    private = [
        pltpu.VMEM((nb, CONV_HIST + length, 3 * DN_WIDTH), F32),
        pltpu.VMEM((nb, POOL_HIST + length, POOL_WIDTH), F32),
        pltpu.VMEM((nb, POOL_HIST + length, POOL_WIDTH), F32),
        pltpu.VMEM((nb, POOL_HIST + length, POOL_WIDTH), F32),
        pltpu.VMEM((nb, POOL_HIST + length, POOL_WIDTH), F32),
        pltpu.VMEM((nb, CONV_HIST + length, SC_WIDTH), F32),
        pltpu.VMEM((rows, D_REST), BF16),
        pltpu.VMEM((DN_HEADS, rows, DN_HEAD_DIM), F32),
        pltpu.VMEM((DN_HEADS, rows, DN_HEAD_DIM), F32),
        pltpu.VMEM((rows, DN_WIDTH), F32),
        pltpu.VMEM((rows, DN_WIDTH), BF16),
    ]
    one_set = [
        pltpu.VMEM((DN_HEADS, rows, rows), BF16),
        pltpu.VMEM((DN_HEADS, rows, rows), BF16),
        pltpu.VMEM((DN_HEADS, rows, rows), BF16),
        pltpu.VMEM((DN_HEADS, rows, 2 * DN_HEAD_DIM), BF16),
        pltpu.VMEM((DN_HEADS, rows, DN_HEAD_DIM), F32),
        pltpu.VMEM((DN_HEADS, DN_HEAD_DIM, rows), F32),
        pltpu.VMEM((DN_HEADS, rows, DN_HEAD_DIM), F32),
        pltpu.VMEM((rows, DN_WIDTH), F32),
        pltpu.VMEM((rows, D_MODEL), F32),
    ]
    return private + one_set * n_sets


def _mixer_kernel(*refs, nb, length, chunk, pos0, n_tiles, n_alias):
    n_in, n_set = len(_MIXER_IN), len(_MIXER_SET)
    refs = refs[:n_in] + refs[n_in + n_alias:]
    names = _MIXER_IN + _MIXER_OUT + _MIXER_PRIVATE
    r = types.SimpleNamespace(**dict(zip(names, refs[:len(names)])))
    set_refs = refs[len(names):]
    sets = [types.SimpleNamespace(**dict(zip(_MIXER_SET, set_refs[i * n_set:(i + 1) * n_set])))
            for i in range(len(set_refs) // n_set)]
    j = pl.program_id(1)

    def init_state():
        r.od[...] = r.sd[...]

    _carry_history(j, length,
                   [(r.extd, r.sdc, CONV_HIST, DN_CONV - 1), (r.extp, r.sp, POOL_HIST, POOL_BUF),
                    (r.exts, r.ssc, CONV_HIST, SC_CONV - 1)], first_tile=init_state)

    stage_a = functools.partial(_stage_a, j, r, nb=nb, length=length, chunk=chunk, pos0=pos0)
    stage_b = functools.partial(_stage_b, r, nb=nb, length=length, chunk=chunk)

    if len(sets) == 1:
        _interleave(stage_a(sets[0]))
        _interleave(stage_b(sets[0]))
        return

    middle = (j > 0) & (j < n_tiles)

    @pl.when(j == 0)
    def _():
        _interleave(stage_a(sets[0]))

    @pl.when(middle & (j % 2 == 1))
    def _():
        _interleave(stage_b(sets[0]), stage_a(sets[1]), speeds=B_AHEAD)

    @pl.when(middle & (j % 2 == 0))
    def _():
        _interleave(stage_b(sets[1]), stage_a(sets[0]), speeds=B_AHEAD)

    @pl.when(j == n_tiles)
    def _():
        _interleave(stage_b(sets[(n_tiles - 1) % 2]))


def _ffn_kernel(x_ref, sf_ref, nffn_ref, wg_ref, fw_ref, wu_ref, wd_ref, fnorm_ref, *rest,
                nb, length, final):
    xo_ref, of_ref, extf_s = rest[-3:]
    j = pl.program_id(1)
    rows = nb * length
    _carry_history(j, length, [(extf_s, sf_ref, CONV_HIST, FFN_CONV - 1)])
    x = x_ref[...].reshape(rows, D_MODEL)
    h16 = (_rms_scale(x) * nffn_ref[...]).astype(BF16)
    extf_s[:, CONV_HIST:CONV_HIST + length, :] = _dot(h16, wg_ref[...]).reshape(nb, length, D_FF)
    of_ref[...] = extf_s[:, CONV_HIST + length - (FFN_CONV - 1):CONV_HIST + length, :]
    base = CONV_HIST - (FFN_CONV - 1)
    gate = extf_s[:, base:base + length, :] * fw_ref[0:1, :][None]
    for i in range(1, FFN_CONV):
        gate = gate + extf_s[:, base + i:base + i + length, :] * fw_ref[i:i + 1, :][None]
    up = _dot(h16, wu_ref[...])
    act = (_silu(gate).reshape(rows, D_FF) * up).astype(BF16)
    y = x + _dot(act, wd_ref[...])
    if final:
        y = _rms_scale(y) * fnorm_ref[...]
    xo_ref[...] = y.reshape(nb, length, D_MODEL)


def _layer_spec(arr, layer):
    tail = arr.shape[1:]
    zeros = (0,) * len(tail)
    return pl.BlockSpec((None,) + tail, lambda i, j: (layer,) + zeros, pipeline_mode=pl.Buffered(1))


def _state_spec(arr, layer, nb):
    tail = arr.shape[2:]
    zeros = (0,) * len(tail)
    return pl.BlockSpec((None, nb) + tail, lambda i, j: (layer, i) + zeros)


def _state_out_spec(shape, layer, nb):
    tail = shape[2:]
    zeros = (0,) * len(tail)
    return pl.BlockSpec((None, nb) + tail, lambda i, j: (layer, i) + zeros)


def _alias_args(prev, n_in, first_out):
    if prev is None:
        return [], [], {}
    specs = [pl.BlockSpec(memory_space=pl.ANY)] * len(prev)
    return list(prev), specs, {n_in + k: first_out + k for k in range(len(prev))}


def _mixer_call(x, states, state_layer, layer, depth, w, prev, *, nb, length, chunk, pos0):
    batch, seq, _ = x.shape
    sd, sdc, sp, ssc = states
    n_tiles = seq // length
    pipelined = n_tiles > 1
    if pipelined:
        last = n_tiles - 1
        grid = (batch // nb, n_tiles + 1)
        x_in_spec = pl.BlockSpec((nb, length, D_MODEL), lambda i, j: (i, jnp.minimum(j, last), 0))
        x_out_spec = pl.BlockSpec((nb, length, D_MODEL), lambda i, j: (i, jnp.maximum(j - 1, 0), 0))
    else:
        grid = (batch // nb, n_tiles)
        x_in_spec = x_out_spec = pl.BlockSpec((nb, length, D_MODEL), lambda i, j: (i, j, 0))
    weights = tuple(w[name] for name in _MIXER_IN[5:])
    out_shape = (
        jax.ShapeDtypeStruct(x.shape, F32),
        jax.ShapeDtypeStruct((depth, batch) + sd.shape[2:], F32),
        jax.ShapeDtypeStruct((depth, batch) + sdc.shape[2:], F32),
        jax.ShapeDtypeStruct((depth, batch) + sp.shape[2:], F32),
        jax.ShapeDtypeStruct((depth, batch) + ssc.shape[2:], F32),
    )
    operands = (x, sd, sdc, sp, ssc) + weights
    alias_ops, alias_specs, aliases = _alias_args(prev, len(operands), 1)
    kern = functools.partial(_mixer_kernel, nb=nb, length=length, chunk=chunk, pos0=pos0, n_tiles=n_tiles,
                             n_alias=len(alias_ops))
    return pl.pallas_call(
        kern,
        grid=grid,
        in_specs=[x_in_spec, _state_spec(sd, state_layer, nb), _state_spec(sdc, state_layer, nb),
                  _state_spec(sp, state_layer, nb), _state_spec(ssc, state_layer, nb)]
                 + [_layer_spec(a, layer) for a in weights] + alias_specs,
        out_specs=(x_out_spec,) + tuple(_state_out_spec(s.shape, layer, nb) for s in out_shape[1:]),
        out_shape=out_shape,
        input_output_aliases=aliases,
        scratch_shapes=_mixer_scratch(nb, length, 2 if pipelined else 1),
        compiler_params=pltpu.CompilerParams(
            dimension_semantics=("parallel", "arbitrary"), vmem_limit_bytes=VMEM_LIMIT_BYTES),
        name="mixer",
    )(*operands, *alias_ops)


def _ffn_call(x, sf, state_layer, layer, depth, w, prev, *, nb, length, final):
    batch, seq, _ = x.shape
    grid = (batch // nb, seq // length)
    x_spec = pl.BlockSpec((nb, length, D_MODEL), lambda i, j: (i, j, 0))
    weights = (w["nffn"], w["wg"], w["fw"], w["wu"], w["wd"])
    fnorm = w["fnorm"]
    out_shape = (jax.ShapeDtypeStruct(x.shape, F32),
                 jax.ShapeDtypeStruct((depth, batch) + sf.shape[2:], F32))
    operands = (x, sf) + weights + (fnorm,)
    alias_ops, alias_specs, aliases = _alias_args(prev, len(operands), 1)
    kern = functools.partial(_ffn_kernel, nb=nb, length=length, final=final)
    return pl.pallas_call(
        kern,
        grid=grid,
        in_specs=[x_spec, _state_spec(sf, state_layer, nb)] + [_layer_spec(a, layer) for a in weights]
                 + [pl.BlockSpec(fnorm.shape, lambda i, j: (0, 0))] + alias_specs,
        out_specs=(x_spec, _state_out_spec(out_shape[1].shape, layer, nb)),
        out_shape=out_shape,
        input_output_aliases=aliases,
        scratch_shapes=[pltpu.VMEM((nb, CONV_HIST + length, D_FF), F32)],
        compiler_params=pltpu.CompilerParams(
            dimension_semantics=("parallel", "arbitrary"), vmem_limit_bytes=VMEM_LIMIT_BYTES),
        name="ffn",
    )(*operands, *alias_ops)


def _prepare_weights(norm_mix, w_in, dn_conv_w, dn_a_log, dn_dt_bias, dn_out_norm, pool_w, pool_scale,
                     sconv_w, w_out, norm_ffn, w_ffn_gate, ffn_conv_w, w_ffn_up, w_ffn_down, final_norm):
    depth = w_in.shape[0]
    assert N_GATE == SUBLANES
    g0 = W_Z + DN_WIDTH
    rest0 = g0 + N_GATE
    wgt = jnp.swapaxes(w_in[:, :, g0:g0 + N_GATE], 1, 2).astype(BF16)
    pad = jnp.zeros((depth, SUBLANES - DN_HEADS, 1), F32)
    eye = jnp.eye(len(POOL_WINDOWS), dtype=F32)
    pw = (pool_w[:, :, :, None, :] * eye[None, :, None, :, None]).reshape(depth, POOL_WIDTH, POOL_WIDTH)
    return {
        "nmix": norm_mix[:, None, :],
        "wgt": wgt,
        "wqkvz": w_in[:, :, :g0].astype(BF16),
        "wrest": w_in[:, :, rest0:rest0 + REST_WIDTH].astype(BF16),
        "dnw": dn_conv_w,
        "alog": jnp.concatenate([dn_a_log[:, :, None], pad], axis=1),
        "dtb": jnp.concatenate([dn_dt_bias[:, :, None], pad], axis=1),
        "onorm": dn_out_norm[:, None, :],
        "pw": pw.astype(BF16),
        "pscale": pool_scale[:, None, :],
        "scw": sconv_w,
        "wout": w_out.astype(BF16),
        "nffn": norm_ffn[:, None, :],
        "wg": w_ffn_gate.astype(BF16),
        "fw": ffn_conv_w,
        "wu": w_ffn_up.astype(BF16),
        "wd": w_ffn_down.astype(BF16),
        "fnorm": final_norm[None, :],
    }


def _trunk(x, states, state_layers, weights, depth, *, nb_mix, nb_ffn, length, ffn_length, chunk, pos0):
    sd, sdc, sp, ssc, sf = states
    mix_out, ffn_out = None, None
    for l in range(depth):
        sl = state_layers[l]
        x, *mix_out = _mixer_call(x, (sd, sdc, sp, ssc), sl, l, depth, weights, mix_out,
                                  nb=nb_mix, length=length, chunk=chunk, pos0=pos0)
        x, *ffn_out = _ffn_call(x, sf, sl, l, depth, weights, ffn_out,
                                nb=nb_ffn, length=ffn_length, final=(l == depth - 1))
    return x, tuple(mix_out) + tuple(ffn_out)


PROMPT_TILE = 256
PROMPT_FFN_TILE = 512
SAMPLE_MIX_ROWS = 16
SAMPLE_FFN_ROWS = 32


def kernel(x_prompt, x_sample, state_delta, state_delta_conv, state_pool, state_sconv, state_ffn_conv,
           norm_mix, w_in, dn_conv_w, dn_a_log, dn_dt_bias, dn_out_norm, pool_w, pool_scale,
           sconv_w, w_out, norm_ffn, w_ffn_gate, ffn_conv_w, w_ffn_up, w_ffn_down, final_norm):
    depth = w_in.shape[0]
    weights = _prepare_weights(norm_mix, w_in, dn_conv_w, dn_a_log, dn_dt_bias, dn_out_norm, pool_w,
                               pool_scale, sconv_w, w_out, norm_ffn, w_ffn_gate, ffn_conv_w, w_ffn_up,
                               w_ffn_down, final_norm)

    batch, seq, _ = x_prompt.shape
    dec_batch, dec_seq, _ = x_sample.shape
    sample_states = (state_delta, state_delta_conv, state_pool, state_sconv, state_ffn_conv)
    zero_states = tuple(jnp.zeros((1, batch) + s.shape[2:], F32) for s in sample_states)

    y_prompt, p_st = _trunk(x_prompt, zero_states, (0,) * depth, weights, depth,
                            nb_mix=1, nb_ffn=1, length=min(PROMPT_TILE, seq), ffn_length=min(PROMPT_FFN_TILE, seq),
                            chunk=min(PROMPT_TILE, seq), pos0=0)
    y_sample, s_st = _trunk(x_sample, sample_states, tuple(range(depth)), weights, depth,
                            nb_mix=SAMPLE_MIX_ROWS, nb_ffn=SAMPLE_FFN_ROWS, length=dec_seq, ffn_length=dec_seq,
                            chunk=min(DN_CHUNK, dec_seq), pos0=PAST_LEN)
    return (y_prompt, y_sample) + p_st + s_st
```

```python
import functools
import math
import types

import jax
import jax.numpy as jnp
from jax import lax
from jax.experimental import pallas as pl
from jax.experimental.pallas import tpu as pltpu

F32 = jnp.float32
BF16 = jnp.bfloat16

LANES = 128
SUBLANES = 8
BF16_ROWS = 16
VMEM_LIMIT_BYTES = 56 * 1024 * 1024

D_MODEL = 1024
DN_HEADS = 4
DN_HEAD_DIM = 128
DN_WIDTH = DN_HEADS * DN_HEAD_DIM
DN_CONV = 4
DN_CHUNK = 64
POOL_WINDOWS = (2, 4, 8, 16)
POOL_WIDTH = 256
POOL_GROUP_DIM = POOL_WIDTH // len(POOL_WINDOWS)
POOL_BUF = max(POOL_WINDOWS) - 1
SC_WIDTH = 256
SC_CONV = 3
D_FF = 2816
FFN_CONV = 3
EPS = 1e-6
PAST_LEN = 16384

N_GATE = 2 * DN_HEADS
W_Z = 3 * DN_WIDTH
REST_WIDTH = POOL_WIDTH + 3 * SC_WIDTH
D_REST = POOL_WIDTH + SC_WIDTH

CONV_HIST = SUBLANES
POOL_HIST = 24

L_GC, L_BETA, L_EGC, L_KDF, L_EGL = 0, DN_HEADS, SUBLANES, 2 * SUBLANES, 3 * SUBLANES


def _silu(t):
    return t / (1.0 + jnp.exp(-t))


def _softplus(t):
    return jnp.maximum(t, 0.0) + jnp.log1p(jnp.exp(-jnp.abs(t)))


def _rms_scale(t):
    return t * lax.rsqrt(jnp.mean(t * t, axis=-1, keepdims=True) + EPS)


def _dot(a, b):
    return jnp.dot(a, b, preferred_element_type=F32)


def _dot_nt(a, b):
    return lax.dot_general(a, b, (((1,), (1,)), ((), ())), preferred_element_type=F32)


def _head_slice(hh):
    return slice(hh * DN_HEAD_DIM, (hh + 1) * DN_HEAD_DIM)


B_AHEAD = (1.5, 1.0)


def _interleave(*stages, speeds=None):
    speeds = speeds or [1.0] * len(stages)
    clock = [0.0] * len(stages)
    live = list(range(len(stages)))
    while live:
        i = min(live, key=lambda s: clock[s])
        try:
            clock[i] += next(stages[i]) / speeds[i]
        except StopIteration:
            live.remove(i)


def _carry_history(j, length, items, first_tile=None):
    @pl.when(j == 0)
    def _():
        if first_tile is not None:
            first_tile()
        for ext, state_ref, hist, nbuf in items:
            ext[:, 0:hist - nbuf, :] = jnp.zeros((ext.shape[0], hist - nbuf, ext.shape[2]), F32)
            ext[:, hist - nbuf:hist, :] = state_ref[...]

    @pl.when(j != 0)
    def _():
        for ext, state_ref, hist, nbuf in items:
            ext[:, hist - nbuf:hist, :] = ext[:, hist + length - nbuf:hist + length, :]


def _causal_conv(ext, w_ref, cs, length):
    nb = ext.shape[0]
    taps = w_ref.shape[0]
    tiles = length // SUBLANES
    e4 = ext[:, 0:CONV_HIST + length, cs]
    width = e4.shape[-1]
    e4 = e4.reshape(nb, tiles + 1, SUBLANES, width)
    sub4 = lax.broadcasted_iota(jnp.int32, (nb, tiles, SUBLANES, width), 2)
    acc = None
    for i in range(taps):
        s = taps - 1 - i
        if s:
            rot = pltpu.roll(e4, s, axis=2)
            term = jnp.where(sub4 >= s, rot[:, 1:], rot[:, :-1])
        else:
            term = e4[:, 1:]
        term = term * w_ref[i:i + 1, cs][None, None]
        acc = term if acc is None else acc + term
    return acc.reshape(nb, length, width)


def _unit_lower_inverses(na16, t16, log2c, out):
    heads = range(len(na16))
    n = na16[0].shape[0]
    row = lax.broadcasted_iota(jnp.int32, (n, n), 0)
    col = lax.broadcasted_iota(jnp.int32, (n, n), 1)
    for lvl in range(1, log2c):
        blk = 1 << lvl
        if blk < BF16_ROWS:
            off = (((row >> (lvl + 1)) == (col >> (lvl + 1)))
                   & (((row >> lvl) & 1) == 1) & (((col >> lvl) & 1) == 0))
            p16 = [_dot(na16[i], t16[i]).astype(BF16) for i in heads]
            t16 = [t16[i] + jnp.where(off, _dot(t16[i], p16[i]), 0.0).astype(BF16) for i in heads]
        else:
            pairs = n // (2 * blk)
            rowh = lax.broadcasted_iota(jnp.int32, (n // 2, n), 0)
            colh = lax.broadcasted_iota(jnp.int32, (n // 2, n), 1)
            offh = ((colh >> (lvl + 1)) == (rowh >> lvl)) & (((colh >> lvl) & 1) == 0)
            t4 = [t16[i].reshape(pairs, 2, blk, n) for i in heads]
            na_odd = [na16[i].reshape(pairs, 2, blk, n)[:, 1].reshape(n // 2, n) for i in heads]
            t_odd = [t4[i][:, 1].reshape(n // 2, n) for i in heads]
            p = [_dot(na_odd[i], t16[i]) for i in heads]
            p16 = [jnp.broadcast_to(p[i].reshape(pairs, 1, blk, n), (pairs, 2, blk, n))
                   .reshape(n, n).astype(BF16) for i in heads]
            x_odd = [jnp.where(offh, _dot(t_odd[i], p16[i]), 0.0).astype(BF16) for i in heads]
            t16 = [jnp.concatenate([t4[i][:, 0:1], (t_odd[i] + x_odd[i]).reshape(pairs, 1, blk, n)],
                                   axis=1).reshape(n, n) for i in heads]
        yield 0.6
    out.extend(t16)


def _stage_a(j, r, st, *, nb, length, chunk, pos0):
    rows = nb * length
    log2c = int(math.log2(chunk))

    x = r.x[...].reshape(rows, D_MODEL)
    h16 = (_rms_scale(x) * r.nmix[...]).astype(BF16)

    gates_t = _dot_nt(r.wgt[...], h16)
    g_t = -jnp.exp(r.alog[...]) * _softplus(gates_t + r.dtb[...])
    beta_t = 1.0 / (1.0 + jnp.exp(-gates_t))
    pos_in_chunk = lax.broadcasted_iota(jnp.int32, (SUBLANES, rows), 1) & (chunk - 1)
    gc_t = g_t
    sfx_t = g_t
    step = 1
    while step < chunk:
        gc_t = gc_t + jnp.where(pos_in_chunk >= step, pltpu.roll(gc_t, step, axis=1), 0.0)
        sfx_t = sfx_t + jnp.where(pos_in_chunk + step < chunk,
                                  pltpu.roll(sfx_t, rows - step, axis=1), 0.0)
        step *= 2
    rest_t = sfx_t - g_t
    sub = lax.broadcasted_iota(jnp.int32, (SUBLANES, rows), 0)
    cols = jnp.concatenate(
        [jnp.where(sub < DN_HEADS, gc_t, -beta_t), jnp.exp(gc_t), jnp.exp(rest_t), jnp.exp(gc_t + rest_t),
         jnp.zeros((LANES - 4 * SUBLANES, rows), F32)], axis=0).T

    def colb(lane, width=LANES):
        return jnp.broadcast_to(cols[:, lane:lane + 1], (rows, width))

    yield 1.0
    for part in range(3):
        cs = slice(part * DN_WIDTH, (part + 1) * DN_WIDTH)
        r.extd[:, CONV_HIST:CONV_HIST + length, cs] = _dot(h16, r.wqkvz[:, cs]).reshape(nb, length, DN_WIDTH)
        yield 0.3
    r.odc[...] = r.extd[:, CONV_HIST + length - (DN_CONV - 1):CONV_HIST + length, :]

    def conv_block(ci):
        cs = slice(ci * LANES, (ci + 1) * LANES)
        return _silu(_causal_conv(r.extd, r.dnw, cs, length)).reshape(rows, LANES)

    row = lax.broadcasted_iota(jnp.int32, (rows, rows), 0)
    col = lax.broadcasted_iota(jnp.int32, (rows, rows), 1)
    same_chunk = (row >> log2c) == (col >> log2c)
    m_incl = same_chunk & (row >= col)
    m_strict = same_chunk & (row > col)
    pair = ((row >> 1) == (col >> 1)) & ((row & 1) == 1) & ((col & 1) == 0)
    eye = jnp.where(row == col, 1.0, 0.0).astype(F32)

    for hh in range(DN_HEADS):
        nbeta_b = colb(L_BETA + hh)
        egc_b = colb(L_EGC + hh)
        qc = conv_block(hh)
        kc = conv_block(DN_HEADS + hh)
        vc = conv_block(2 * DN_HEADS + hh)
        q = qc * (lax.rsqrt(jnp.sum(qc * qc, axis=-1, keepdims=True) + EPS) * (DN_HEAD_DIM ** -0.5))
        k = kc * lax.rsqrt(jnp.sum(kc * kc, axis=-1, keepdims=True) + EPS)
        nkb = k * nbeta_b
        k16 = k.astype(BF16)
        yield 0.6
        diff = colb(L_GC + hh, rows) - gc_t[hh:hh + 1, :]
        decay = jnp.where(m_incl, jnp.exp(jnp.where(m_incl, diff, 0.0)), 0.0)
        na = jnp.where(m_strict, _dot_nt(nkb.astype(BF16), k16) * decay, 0.0)
        st.na[hh] = na.astype(BF16)
        st.t0[hh] = jnp.where(pair, na, eye).astype(BF16)
        st.qk[hh] = (_dot_nt(q.astype(BF16), k16) * decay).astype(BF16)
        st.rhs[hh] = jnp.concatenate([vc * nbeta_b, nkb * egc_b], axis=1).astype(BF16)
        st.qg[hh] = q * egc_b
        st.kdt[hh] = (k * colb(L_KDF + hh)).T
        st.egl[hh] = colb(L_EGL + hh)
        yield 0.5

    prest = _dot(h16, r.wrest[...])
    st.z[...] = _dot(h16, r.wqkvz[:, W_Z:W_Z + DN_WIDTH])
    p_in = prest[:, 0:POOL_WIDTH]
    sc_x = prest[:, POOL_WIDTH:POOL_WIDTH + SC_WIDTH]
    sc_b = prest[:, POOL_WIDTH + SC_WIDTH:POOL_WIDTH + 2 * SC_WIDTH]
    sc_c = prest[:, POOL_WIDTH + 2 * SC_WIDTH:]
    yield 0.5

    ph = POOL_HIST
    r.extp[:, ph:ph + length, :] = p_in.reshape(nb, length, POOL_WIDTH)
    r.op[...] = r.extp[:, ph + length - POOL_BUF:ph + length, :]
    zero8 = jnp.zeros((nb, SUBLANES, POOL_WIDTH), F32)
    end = ph + length
    r.s2[:, 0:8, :] = zero8
    r.s4[:, 0:8, :] = zero8
    r.s8[:, 0:8, :] = zero8
    r.s2[:, 8:end, :] = r.extp[:, 8:end, :] + r.extp[:, 7:end - 1, :]
    r.s4[:, 8:end, :] = r.s2[:, 8:end, :] + r.s2[:, 6:end - 2, :]
    r.s8[:, 8:end, :] = r.s4[:, 8:end, :] + r.s4[:, 4:end - 4, :]
    s16 = r.s8[:, ph:end, :] + r.s8[:, ph - 8:end - 8, :]
    lane3 = lax.broadcasted_iota(jnp.int32, (nb, length, POOL_WIDTH), 2)
    grp = lane3 // POOL_GROUP_DIM
    wsum = jnp.where(grp == 0, r.s2[:, ph:end, :],
                     jnp.where(grp == 1, r.s4[:, ph:end, :],
                               jnp.where(grp == 2, r.s8[:, ph:end, :], s16)))
    win = jnp.where(grp == 0, POOL_WINDOWS[0],
                    jnp.where(grp == 1, POOL_WINDOWS[1],
                              jnp.where(grp == 2, POOL_WINDOWS[2], POOL_WINDOWS[3])))
    tpos = lax.broadcasted_iota(jnp.int32, (nb, length, POOL_WIDTH), 1) + (pos0 + j * length)
    cnt = jnp.minimum(tpos + 1, win).astype(F32)
    dpool = (wsum / cnt - r.extp[:, ph:end, :]).reshape(rows, POOL_WIDTH)
    r.cat_a[:, 0:POOL_WIDTH] = (_dot(dpool.astype(BF16), r.pw[...]) * r.pscale[...]).astype(BF16)
    yield 0.4

    r.exts[:, CONV_HIST:CONV_HIST + length, :] = (sc_c * sc_x).reshape(nb, length, SC_WIDTH)
    r.osc[...] = r.exts[:, CONV_HIST + length - (SC_CONV - 1):CONV_HIST + length, :]
    cconv = _causal_conv(r.exts, r.scw, slice(0, SC_WIDTH), length)
    r.cat_a[:, POOL_WIDTH:] = (sc_b * cconv.reshape(rows, SC_WIDTH)).astype(BF16)

    st.xy[...] = x + _dot(r.cat_a[...], r.wout[DN_WIDTH:, :])


def _stage_b(r, st, *, nb, length, chunk):
    rows = nb * length
    log2c = int(math.log2(chunk))
    n_chunks = rows // chunk

    tinv = []
    yield from _unit_lower_inverses([st.na[hh] for hh in range(DN_HEADS)],
                                    [st.t0[hh] for hh in range(DN_HEADS)], log2c, tinv)
    for hh in range(DN_HEADS):
        nsol = _dot(tinv[hh], st.rhs[hh])
        r.nu[hh] = nsol[:, :DN_HEAD_DIM]
        r.nw[hh] = nsol[:, DN_HEAD_DIM:]
    yield 0.5

    if n_chunks == 1:
        for hh in range(DN_HEADS):
            s = r.od[0, hh]
            nwq = jnp.concatenate([r.nw[hh], st.qg[hh]], axis=0).astype(BF16)
            res = _dot(nwq, s.astype(BF16))
            vn16 = (res[:rows] - r.nu[hh]).astype(BF16)
            r.o[:, _head_slice(hh)] = res[rows:] + _dot(st.qk[hh], vn16)
            r.od[0, hh] = s * st.egl[hh, 0:1, :] + _dot(st.kdt[hh].astype(BF16), vn16)
            yield 0.3
    else:
        assert n_chunks == nb
        seg = lax.broadcasted_iota(jnp.int32, (nb, 1, rows), 2) >> log2c
        own = seg == lax.broadcasted_iota(jnp.int32, (nb, 1, rows), 0)
        for hh in range(DN_HEADS):
            vn_parts, qs_parts = [], []
            for b in range(nb):
                rs = slice(b * chunk, (b + 1) * chunk)
                nwq = jnp.concatenate([r.nw[hh, rs, :], st.qg[hh, rs, :]], axis=0).astype(BF16)
                res = _dot(nwq, r.od[b, hh].astype(BF16))
                vn_parts.append(res[:chunk] - r.nu[hh, rs, :])
                qs_parts.append(res[chunk:])
            vn16 = jnp.concatenate(vn_parts, axis=0).astype(BF16)
            r.o[:, _head_slice(hh)] = jnp.concatenate(qs_parts, axis=0) + _dot(st.qk[hh], vn16)
            kd_own = jnp.where(own, st.kdt[hh][None], 0.0).astype(BF16).reshape(nb * DN_HEAD_DIM, rows)
            upd = _dot(kd_own, vn16).reshape(nb, DN_HEAD_DIM, DN_HEAD_DIM)
            egl = st.egl[hh].reshape(nb, chunk, DN_HEAD_DIM)[:, 0:1, :]
            r.od[:, hh, :, :] = r.od[:, hh, :, :] * egl + upd
            yield 0.3

    for hh in range(DN_HEADS):
        hs = _head_slice(hh)
        r.cat_b[:, hs] = (_rms_scale(r.o[:, hs]) * r.onorm[...] * _silu(st.z[:, hs])).astype(BF16)
    y = st.xy[...] + _dot(r.cat_b[...], r.wout[:DN_WIDTH, :])
    r.xo[...] = y.reshape(nb, length, D_MODEL)


_MIXER_IN = ("x", "sd", "sdc", "sp", "ssc", "nmix", "wgt", "wqkvz", "wrest", "dnw", "alog", "dtb", "onorm", "pw",
             "pscale", "scw", "wout")
_MIXER_OUT = ("xo", "od", "odc", "op", "osc")
_MIXER_PRIVATE = ("extd", "extp", "s2", "s4", "s8", "exts", "cat_a", "nu", "nw", "o", "cat_b")
_MIXER_SET = ("na", "t0", "qk", "rhs", "qg", "kdt", "egl", "z", "xy")


def _mixer_scratch(nb, length, n_sets):
    rows = nb * length
    private = [
        pltpu.VMEM((nb, CONV_HIST + length, 3 * DN_WIDTH), F32),
        pltpu.VMEM((nb, POOL_HIST + length, POOL_WIDTH), F32),
        pltpu.VMEM((nb, POOL_HIST + length, POOL_WIDTH), F32),
        pltpu.VMEM((nb, POOL_HIST + length, POOL_WIDTH), F32),
        pltpu.VMEM((nb, POOL_HIST + length, POOL_WIDTH), F32),
        pltpu.VMEM((nb, CONV_HIST + length, SC_WIDTH), F32),
        pltpu.VMEM((rows, D_REST), BF16),
        pltpu.VMEM((DN_HEADS, rows, DN_HEAD_DIM), F32),
        pltpu.VMEM((DN_HEADS, rows, DN_HEAD_DIM), F32),
        pltpu.VMEM((rows, DN_WIDTH), F32),
        pltpu.VMEM((rows, DN_WIDTH), BF16),
    ]
    one_set = [
        pltpu.VMEM((DN_HEADS, rows, rows), BF16),
        pltpu.VMEM((DN_HEADS, rows, rows), BF16),
        pltpu.VMEM((DN_HEADS, rows, rows), BF16),
        pltpu.VMEM((DN_HEADS, rows, 2 * DN_HEAD_DIM), BF16),
        pltpu.VMEM((DN_HEADS, rows, DN_HEAD_DIM), F32),
        pltpu.VMEM((DN_HEADS, DN_HEAD_DIM, rows), F32),
        pltpu.VMEM((DN_HEADS, rows, DN_HEAD_DIM), F32),
        pltpu.VMEM((rows, DN_WIDTH), F32),
        pltpu.VMEM((rows, D_MODEL), F32),
    ]
    return private + one_set * n_sets


def _zero_other_layers(ref, layer):
    for l in range(ref.shape[0]):
        if l != layer:
            ref[l] = jnp.zeros(ref.shape[1:], F32)


def _mixer_kernel(*refs, nb, length, chunk, pos0, n_tiles, n_alias, layer, whole):
    n_in, n_set = len(_MIXER_IN), len(_MIXER_SET)
    refs = refs[:n_in] + refs[n_in + n_alias:]
    names = _MIXER_IN + _MIXER_OUT + _MIXER_PRIVATE
    r = types.SimpleNamespace(**dict(zip(names, refs[:len(names)])))
    set_refs = refs[len(names):]
    sets = [types.SimpleNamespace(**dict(zip(_MIXER_SET, set_refs[i * n_set:(i + 1) * n_set])))
            for i in range(len(set_refs) // n_set)]
    j = pl.program_id(1)
    whole_refs = (r.od, r.odc, r.op, r.osc)
    if whole:
        r.od, r.odc, r.op, r.osc = (f.at[layer] for f in whole_refs)

    def init_state():
        if whole:
            for f in whole_refs:
                _zero_other_layers(f, layer)
        r.od[...] = r.sd[...]

    _carry_history(j, length,
                   [(r.extd, r.sdc, CONV_HIST, DN_CONV - 1), (r.extp, r.sp, POOL_HIST, POOL_BUF),
                    (r.exts, r.ssc, CONV_HIST, SC_CONV - 1)], first_tile=init_state)

    stage_a = functools.partial(_stage_a, j, r, nb=nb, length=length, chunk=chunk, pos0=pos0)
    stage_b = functools.partial(_stage_b, r, nb=nb, length=length, chunk=chunk)

    if len(sets) == 1:
        _interleave(stage_a(sets[0]))
        _interleave(stage_b(sets[0]))
        return

    middle = (j > 0) & (j < n_tiles)

    @pl.when(j == 0)
    def _():
        _interleave(stage_a(sets[0]))

    @pl.when(middle & (j % 2 == 1))
    def _():
        _interleave(stage_b(sets[0]), stage_a(sets[1]), speeds=B_AHEAD)

    @pl.when(middle & (j % 2 == 0))
    def _():
        _interleave(stage_b(sets[1]), stage_a(sets[0]), speeds=B_AHEAD)

    @pl.when(j == n_tiles)
    def _():
        _interleave(stage_b(sets[(n_tiles - 1) % 2]))


def _ffn_kernel(x_ref, sf_ref, nffn_ref, wg_ref, fw_ref, wu_ref, wd_ref, fnorm_ref, *rest,
                nb, length, final, layer, whole):
    xo_ref, of_full, extf_s = rest[-3:]
    of_ref = of_full.at[layer] if whole else of_full
    j = pl.program_id(1)
    rows = nb * length
    _carry_history(j, length, [(extf_s, sf_ref, CONV_HIST, FFN_CONV - 1)],
                   first_tile=functools.partial(_zero_other_layers, of_full, layer) if whole else None)
    x = x_ref[...].reshape(rows, D_MODEL)
    h16 = (_rms_scale(x) * nffn_ref[...]).astype(BF16)
    extf_s[:, CONV_HIST:CONV_HIST + length, :] = _dot(h16, wg_ref[...]).reshape(nb, length, D_FF)
    of_ref[...] = extf_s[:, CONV_HIST + length - (FFN_CONV - 1):CONV_HIST + length, :]
    gate = _causal_conv(extf_s, fw_ref, slice(0, D_FF), length)
    up = _dot(h16, wu_ref[...])
    act = (_silu(gate).reshape(rows, D_FF) * up).astype(BF16)
    y = x + _dot(act, wd_ref[...])
    if final:
        y = _rms_scale(y) * fnorm_ref[...]
    xo_ref[...] = y.reshape(nb, length, D_MODEL)


def _layer_spec(arr, layer):
    tail = arr.shape[1:]
    zeros = (0,) * len(tail)
    return pl.BlockSpec((None,) + tail, lambda i, j: (layer,) + zeros, pipeline_mode=pl.Buffered(1))


def _state_spec(arr, layer, nb):
    tail = arr.shape[2:]
    zeros = (0,) * len(tail)
    return pl.BlockSpec((None, nb) + tail, lambda i, j: (layer, i) + zeros)


def _state_out_spec(shape, layer, nb, whole):
    tail = shape[2:]
    zeros = (0,) * len(tail)
    if whole:
        return pl.BlockSpec((shape[0], nb) + tail, lambda i, j: (0, i) + zeros)
    return pl.BlockSpec((None, nb) + tail, lambda i, j: (layer, i) + zeros)


def _alias_args(prev, n_in, first_out):
    if prev is None:
        return [], [], {}
    specs = [pl.BlockSpec(memory_space=pl.ANY)] * len(prev)
    return list(prev), specs, {n_in + k: first_out + k for k in range(len(prev))}


def _mixer_call(x, states, state_layer, layer, depth, w, prev, *, nb, length, chunk, pos0):
    batch, seq, _ = x.shape
    sd, sdc, sp, ssc = states
    n_tiles = seq // length
    pipelined = n_tiles > 1
    if pipelined:
        last = n_tiles - 1
        grid = (batch // nb, n_tiles + 1)
        x_in_spec = pl.BlockSpec((nb, length, D_MODEL), lambda i, j: (i, jnp.minimum(j, last), 0))
        x_out_spec = pl.BlockSpec((nb, length, D_MODEL), lambda i, j: (i, jnp.maximum(j - 1, 0), 0))
    else:
        grid = (batch // nb, n_tiles)
        x_in_spec = x_out_spec = pl.BlockSpec((nb, length, D_MODEL), lambda i, j: (i, j, 0))
    weights = tuple(w[name] for name in _MIXER_IN[5:])
    out_shape = (
        jax.ShapeDtypeStruct(x.shape, F32),
        jax.ShapeDtypeStruct((depth, batch) + sd.shape[2:], F32),
        jax.ShapeDtypeStruct((depth, batch) + sdc.shape[2:], F32),
        jax.ShapeDtypeStruct((depth, batch) + sp.shape[2:], F32),
        jax.ShapeDtypeStruct((depth, batch) + ssc.shape[2:], F32),
    )
    operands = (x, sd, sdc, sp, ssc) + weights
    alias_ops, alias_specs, aliases = _alias_args(prev, len(operands), 1)
    whole = prev is None
    kern = functools.partial(_mixer_kernel, nb=nb, length=length, chunk=chunk, pos0=pos0, n_tiles=n_tiles,
                             n_alias=len(alias_ops), layer=layer, whole=whole)
    return pl.pallas_call(
        kern,
        grid=grid,
        in_specs=[x_in_spec, _state_spec(sd, state_layer, nb), _state_spec(sdc, state_layer, nb),
                  _state_spec(sp, state_layer, nb), _state_spec(ssc, state_layer, nb)]
                 + [_layer_spec(a, layer) for a in weights] + alias_specs,
        out_specs=(x_out_spec,) + tuple(_state_out_spec(s.shape, layer, nb, whole) for s in out_shape[1:]),
        out_shape=out_shape,
        input_output_aliases=aliases,
        scratch_shapes=_mixer_scratch(nb, length, 2 if pipelined else 1),
        compiler_params=pltpu.CompilerParams(
            dimension_semantics=("parallel", "arbitrary"), vmem_limit_bytes=VMEM_LIMIT_BYTES),
        name="mixer",
    )(*operands, *alias_ops)


def _ffn_call(x, sf, state_layer, layer, depth, w, prev, *, nb, length, final):
    batch, seq, _ = x.shape
    grid = (batch // nb, seq // length)
    x_spec = pl.BlockSpec((nb, length, D_MODEL), lambda i, j: (i, j, 0))
    weights = (w["nffn"], w["wg"], w["fw"], w["wu"], w["wd"])
    fnorm = w["fnorm"]
    out_shape = (jax.ShapeDtypeStruct(x.shape, F32),
                 jax.ShapeDtypeStruct((depth, batch) + sf.shape[2:], F32))
    operands = (x, sf) + weights + (fnorm,)
    alias_ops, alias_specs, aliases = _alias_args(prev, len(operands), 1)
    whole = prev is None
    kern = functools.partial(_ffn_kernel, nb=nb, length=length, final=final, layer=layer, whole=whole)
    return pl.pallas_call(
        kern,
        grid=grid,
        in_specs=[x_spec, _state_spec(sf, state_layer, nb)] + [_layer_spec(a, layer) for a in weights]
                 + [pl.BlockSpec(fnorm.shape, lambda i, j: (0, 0))] + alias_specs,
        out_specs=(x_spec, _state_out_spec(out_shape[1].shape, layer, nb, whole)),
        out_shape=out_shape,
        input_output_aliases=aliases,
        scratch_shapes=[pltpu.VMEM((nb, CONV_HIST + length, D_FF), F32)],
        compiler_params=pltpu.CompilerParams(
            dimension_semantics=("parallel", "arbitrary"), vmem_limit_bytes=VMEM_LIMIT_BYTES),
        name="ffn",
    )(*operands, *alias_ops)


def _prepare_weights(norm_mix, w_in, dn_conv_w, dn_a_log, dn_dt_bias, dn_out_norm, pool_w, pool_scale,
                     sconv_w, w_out, norm_ffn, w_ffn_gate, ffn_conv_w, w_ffn_up, w_ffn_down, final_norm):
    depth = w_in.shape[0]
    assert N_GATE == SUBLANES
    g0 = W_Z + DN_WIDTH
    rest0 = g0 + N_GATE
    wgt = jnp.swapaxes(w_in[:, :, g0:g0 + N_GATE], 1, 2).astype(BF16)
    pad = jnp.zeros((depth, SUBLANES - DN_HEADS, 1), F32)
    eye = jnp.eye(len(POOL_WINDOWS), dtype=F32)
    pw = (pool_w[:, :, :, None, :] * eye[None, :, None, :, None]).reshape(depth, POOL_WIDTH, POOL_WIDTH)
    return {
        "nmix": norm_mix[:, None, :],
        "wgt": wgt,
        "wqkvz": w_in[:, :, :g0].astype(BF16),
        "wrest": w_in[:, :, rest0:rest0 + REST_WIDTH].astype(BF16),
        "dnw": dn_conv_w,
        "alog": jnp.concatenate([dn_a_log[:, :, None], pad], axis=1),
        "dtb": jnp.concatenate([dn_dt_bias[:, :, None], pad], axis=1),
        "onorm": dn_out_norm[:, None, :],
        "pw": pw.astype(BF16),
        "pscale": pool_scale[:, None, :],
        "scw": sconv_w,
        "wout": w_out.astype(BF16),
        "nffn": norm_ffn[:, None, :],
        "wg": w_ffn_gate.astype(BF16),
        "fw": ffn_conv_w,
        "wu": w_ffn_up.astype(BF16),
        "wd": w_ffn_down.astype(BF16),
        "fnorm": final_norm[None, :],
    }


def _trunk(x, states, state_layers, weights, depth, *, nb_mix, nb_ffn, length, ffn_length, chunk, pos0):
    sd, sdc, sp, ssc, sf = states
    mix_out, ffn_out = None, None
    for l in range(depth):
        sl = state_layers[l]
        x, *mix_out = _mixer_call(x, (sd, sdc, sp, ssc), sl, l, depth, weights, mix_out,
                                  nb=nb_mix, length=length, chunk=chunk, pos0=pos0)
        x, *ffn_out = _ffn_call(x, sf, sl, l, depth, weights, ffn_out,
                                nb=nb_ffn, length=ffn_length, final=(l == depth - 1))
    return x, tuple(mix_out) + tuple(ffn_out)


PROMPT_TILE = 256
PROMPT_FFN_TILE = 512
SAMPLE_MIX_ROWS = 16
SAMPLE_FFN_ROWS = 32


def kernel(x_prompt, x_sample, state_delta, state_delta_conv, state_pool, state_sconv, state_ffn_conv,
           norm_mix, w_in, dn_conv_w, dn_a_log, dn_dt_bias, dn_out_norm, pool_w, pool_scale,
           sconv_w, w_out, norm_ffn, w_ffn_gate, ffn_conv_w, w_ffn_up, w_ffn_down, final_norm):
    depth = w_in.shape[0]
    weights = _prepare_weights(norm_mix, w_in, dn_conv_w, dn_a_log, dn_dt_bias, dn_out_norm, pool_w,
                               pool_scale, sconv_w, w_out, norm_ffn, w_ffn_gate, ffn_conv_w, w_ffn_up,
                               w_ffn_down, final_norm)

    batch, seq, _ = x_prompt.shape
    dec_batch, dec_seq, _ = x_sample.shape
    sample_states = (state_delta, state_delta_conv, state_pool, state_sconv, state_ffn_conv)
    zero_states = tuple(jnp.zeros((1, batch) + s.shape[2:], F32) for s in sample_states)

    y_prompt, p_st = _trunk(x_prompt, zero_states, (0,) * depth, weights, depth,
                            nb_mix=1, nb_ffn=1, length=min(PROMPT_TILE, seq), ffn_length=min(PROMPT_FFN_TILE, seq),
                            chunk=min(PROMPT_TILE, seq), pos0=0)
    y_sample, s_st = _trunk(x_sample, sample_states, tuple(range(depth)), weights, depth,
                            nb_mix=SAMPLE_MIX_ROWS, nb_ffn=SAMPLE_FFN_ROWS, length=dec_seq, ffn_length=dec_seq,
                            chunk=min(DN_CHUNK, dec_seq), pos0=PAST_LEN)
    return (y_prompt, y_sample) + p_st + s_st
```

```python
import functools
import math
import types

import jax
import jax.numpy as jnp
from jax import lax
from jax.experimental import pallas as pl
from jax.experimental.pallas import tpu as pltpu

F32 = jnp.float32
BF16 = jnp.bfloat16

LANES = 128
SUBLANES = 8
BF16_ROWS = 16
VMEM_LIMIT_BYTES = 56 * 1024 * 1024

D_MODEL = 1024
DN_HEADS = 4
DN_HEAD_DIM = 128
DN_WIDTH = DN_HEADS * DN_HEAD_DIM
DN_CONV = 4
DN_CHUNK = 64
POOL_WINDOWS = (2, 4, 8, 16)
POOL_WIDTH = 256
POOL_GROUP_DIM = POOL_WIDTH // len(POOL_WINDOWS)
POOL_BUF = max(POOL_WINDOWS) - 1
SC_WIDTH = 256
SC_CONV = 3
D_FF = 2816
FFN_CONV = 3
EPS = 1e-6
PAST_LEN = 16384

N_GATE = 2 * DN_HEADS
W_Z = 3 * DN_WIDTH
REST_WIDTH = POOL_WIDTH + 3 * SC_WIDTH
D_REST = POOL_WIDTH + SC_WIDTH

CONV_HIST = SUBLANES
POOL_HIST = 24

L_GC, L_BETA, L_EGC, L_KDF, L_EGL = 0, DN_HEADS, SUBLANES, 2 * SUBLANES, 3 * SUBLANES


def _silu(t):
    return t / (1.0 + jnp.exp(-t))


def _softplus(t):
    return jnp.maximum(t, 0.0) + jnp.log1p(jnp.exp(-jnp.abs(t)))


def _rms_scale(t):
    return t * lax.rsqrt(jnp.mean(t * t, axis=-1, keepdims=True) + EPS)


def _dot(a, b):
    return jnp.dot(a, b, preferred_element_type=F32)


def _dot_nt(a, b):
    return lax.dot_general(a, b, (((1,), (1,)), ((), ())), preferred_element_type=F32)


def _head_slice(hh):
    return slice(hh * DN_HEAD_DIM, (hh + 1) * DN_HEAD_DIM)


B_AHEAD = (1.5, 1.0)


def _interleave(*stages, speeds=None):
    speeds = speeds or [1.0] * len(stages)
    clock = [0.0] * len(stages)
    live = list(range(len(stages)))
    while live:
        i = min(live, key=lambda s: clock[s])
        try:
            clock[i] += next(stages[i]) / speeds[i]
        except StopIteration:
            live.remove(i)


def _carry_history(j, length, items, first_tile=None):
    @pl.when(j == 0)
    def _():
        if first_tile is not None:
            first_tile()
        for ext, state_ref, hist, nbuf in items:
            ext[:, 0:hist - nbuf, :] = jnp.zeros((ext.shape[0], hist - nbuf, ext.shape[2]), F32)
            ext[:, hist - nbuf:hist, :] = state_ref[...]

    @pl.when(j != 0)
    def _():
        for ext, state_ref, hist, nbuf in items:
            ext[:, hist - nbuf:hist, :] = ext[:, hist + length - nbuf:hist + length, :]


def _causal_conv(ext, w_ref, cs, length, bs=slice(None)):
    taps = w_ref.shape[0]
    tiles = length // SUBLANES
    e4 = ext[bs, 0:CONV_HIST + length, cs]
    nb, _, width = e4.shape
    e4 = e4.reshape(nb, tiles + 1, SUBLANES, width)
    sub4 = lax.broadcasted_iota(jnp.int32, (nb, tiles, SUBLANES, width), 2)
    acc = None
    for i in range(taps):
        s = taps - 1 - i
        if s:
            rot = pltpu.roll(e4, s, axis=2)
            term = jnp.where(sub4 >= s, rot[:, 1:], rot[:, :-1])
        else:
            term = e4[:, 1:]
        term = term * w_ref[i:i + 1, cs][None, None]
        acc = term if acc is None else acc + term
    return acc.reshape(nb, length, width)


def _unit_lower_inverses(na16, log2c, out):
    heads = range(len(na16))
    n = na16[0].shape[0]
    row = lax.broadcasted_iota(jnp.int32, (n, n), 0)
    col = lax.broadcasted_iota(jnp.int32, (n, n), 1)
    pair = ((row >> 1) == (col >> 1)) & ((row & 1) == 1) & ((col & 1) == 0)
    eye = jnp.where(row == col, 1.0, 0.0).astype(F32)
    t16 = [jnp.where(pair, na16[i].astype(F32), eye).astype(BF16) for i in heads]
    yield 0.3
    for lvl in range(1, log2c):
        blk = 1 << lvl
        if blk < BF16_ROWS:
            off = (((row >> (lvl + 1)) == (col >> (lvl + 1)))
                   & (((row >> lvl) & 1) == 1) & (((col >> lvl) & 1) == 0))
            p16 = [_dot(na16[i], t16[i]).astype(BF16) for i in heads]
            yield 0.3
            t16 = [t16[i] + jnp.where(off, _dot(t16[i], p16[i]), 0.0).astype(BF16) for i in heads]
        else:
            pairs = n // (2 * blk)
            rowh = lax.broadcasted_iota(jnp.int32, (n // 2, n), 0)
            colh = lax.broadcasted_iota(jnp.int32, (n // 2, n), 1)
            offh = ((colh >> (lvl + 1)) == (rowh >> lvl)) & (((colh >> lvl) & 1) == 0)
            t4 = [t16[i].reshape(pairs, 2, blk, n) for i in heads]
            na_odd = [na16[i].reshape(pairs, 2, blk, n)[:, 1].reshape(n // 2, n) for i in heads]
            t_odd = [t4[i][:, 1].reshape(n // 2, n) for i in heads]
            p = [_dot(na_odd[i], t16[i]) for i in heads]
            p16 = [jnp.broadcast_to(p[i].reshape(pairs, 1, blk, n), (pairs, 2, blk, n))
                   .reshape(n, n).astype(BF16) for i in heads]
            yield 0.3
            x_odd = [jnp.where(offh, _dot(t_odd[i], p16[i]), 0.0).astype(BF16) for i in heads]
            t16 = [jnp.concatenate([t4[i][:, 0:1], (t_odd[i] + x_odd[i]).reshape(pairs, 1, blk, n)],
                                   axis=1).reshape(n, n) for i in heads]
        yield 0.3
    out.extend(t16)


def _stage_a(j, r, st, *, nb, length, chunk, pos0, groups):
    rows = nb * length
    grows, gnb = rows // groups, nb // groups
    log2c = int(math.log2(chunk))

    x = r.x[...].reshape(rows, D_MODEL)
    h16 = (_rms_scale(x) * r.nmix[...]).astype(BF16)

    gates_t = _dot_nt(r.wgt[...], h16)
    g_t = -jnp.exp(r.alog[...]) * _softplus(gates_t + r.dtb[...])
    beta_t = 1.0 / (1.0 + jnp.exp(-gates_t))
    pos_in_chunk = lax.broadcasted_iota(jnp.int32, (SUBLANES, rows), 1) & (chunk - 1)
    gc_t = g_t
    sfx_t = g_t
    step = 1
    while step < chunk:
        gc_t = gc_t + jnp.where(pos_in_chunk >= step, pltpu.roll(gc_t, step, axis=1), 0.0)
        sfx_t = sfx_t + jnp.where(pos_in_chunk + step < chunk,
                                  pltpu.roll(sfx_t, rows - step, axis=1), 0.0)
        step *= 2
    rest_t = sfx_t - g_t
    sub = lax.broadcasted_iota(jnp.int32, (SUBLANES, rows), 0)
    cols = jnp.concatenate(
        [jnp.where(sub < DN_HEADS, gc_t, -beta_t), jnp.exp(gc_t), jnp.exp(rest_t), jnp.exp(gc_t + rest_t),
         jnp.zeros((LANES - 4 * SUBLANES, rows), F32)], axis=0).T

    def colb(lane, g, width=LANES):
        return jnp.broadcast_to(cols[g * grows:(g + 1) * grows, lane:lane + 1], (grows, width))

    yield 1.0
    for part in range(3):
        cs = slice(part * DN_WIDTH, (part + 1) * DN_WIDTH)
        r.extd[:, CONV_HIST:CONV_HIST + length, cs] = _dot(h16, r.wqkvz[:, cs]).reshape(nb, length, DN_WIDTH)
        yield 0.3
    r.odc[...] = r.extd[:, CONV_HIST + length - (DN_CONV - 1):CONV_HIST + length, :]

    def conv_block(ci, g):
        cs = slice(ci * LANES, (ci + 1) * LANES)
        bs = slice(g * gnb, (g + 1) * gnb)
        return _silu(_causal_conv(r.extd, r.dnw, cs, length, bs)).reshape(grows, LANES)

    row = lax.broadcasted_iota(jnp.int32, (grows, grows), 0)
    col = lax.broadcasted_iota(jnp.int32, (grows, grows), 1)
    same_chunk = (row >> log2c) == (col >> log2c)
    m_incl = same_chunk & (row >= col)
    m_strict = same_chunk & (row > col)

    for c in range(groups * DN_HEADS):
        g, hh = divmod(c, DN_HEADS)
        nbeta_b = colb(L_BETA + hh, g)
        egc_b = colb(L_EGC + hh, g)
        qc = conv_block(hh, g)
        yield 0.2
        kc = conv_block(DN_HEADS + hh, g)
        yield 0.2
        vc = conv_block(2 * DN_HEADS + hh, g)
        q = qc * (lax.rsqrt(jnp.sum(qc * qc, axis=-1, keepdims=True) + EPS) * (DN_HEAD_DIM ** -0.5))
        k = kc * lax.rsqrt(jnp.sum(kc * kc, axis=-1, keepdims=True) + EPS)
        nkb = k * nbeta_b
        k16 = k.astype(BF16)
        yield 0.2
        diff = colb(L_GC + hh, g, grows) - gc_t[hh:hh + 1, g * grows:(g + 1) * grows]
        decay = jnp.where(m_incl, jnp.exp(jnp.where(m_incl, diff, 0.0)), 0.0)
        na = jnp.where(m_strict, _dot_nt(nkb.astype(BF16), k16) * decay, 0.0)
        st.na[c] = na.astype(BF16)
        yield 0.25
        st.qk[c] = (_dot_nt(q.astype(BF16), k16) * decay).astype(BF16)
        st.rhs[c] = jnp.concatenate([vc * nbeta_b, nkb * egc_b], axis=1).astype(BF16)
        st.qg[c] = q * egc_b
        st.kdt[c] = (k * colb(L_KDF + hh, g)).T
        st.egl[c] = colb(L_EGL + hh, g)[0:st.egl.shape[1]]
        yield 0.25

    prest = _dot(h16, r.wrest[...])
    st.z[...] = _dot(h16, r.wqkvz[:, W_Z:W_Z + DN_WIDTH])
    p_in = prest[:, 0:POOL_WIDTH]
    sc_x = prest[:, POOL_WIDTH:POOL_WIDTH + SC_WIDTH]
    sc_b = prest[:, POOL_WIDTH + SC_WIDTH:POOL_WIDTH + 2 * SC_WIDTH]
    sc_c = prest[:, POOL_WIDTH + 2 * SC_WIDTH:]
    yield 0.5

    ph = POOL_HIST
    r.extp[:, ph:ph + length, :] = p_in.reshape(nb, length, POOL_WIDTH)
    r.op[...] = r.extp[:, ph + length - POOL_BUF:ph + length, :]
    zero8 = jnp.zeros((nb, SUBLANES, POOL_WIDTH), F32)
    end = ph + length
    r.s2[:, 0:8, :] = zero8
    r.s4[:, 0:8, :] = zero8
    r.s8[:, 0:8, :] = zero8
    r.s2[:, 8:end, :] = r.extp[:, 8:end, :] + r.extp[:, 7:end - 1, :]
    r.s4[:, 8:end, :] = r.s2[:, 8:end, :] + r.s2[:, 6:end - 2, :]
    r.s8[:, 8:end, :] = r.s4[:, 8:end, :] + r.s4[:, 4:end - 4, :]
    s16 = r.s8[:, ph:end, :] + r.s8[:, ph - 8:end - 8, :]
    lane3 = lax.broadcasted_iota(jnp.int32, (nb, length, POOL_WIDTH), 2)
    grp = lane3 // POOL_GROUP_DIM
    wsum = jnp.where(grp == 0, r.s2[:, ph:end, :],
                     jnp.where(grp == 1, r.s4[:, ph:end, :],
                               jnp.where(grp == 2, r.s8[:, ph:end, :], s16)))
    win = jnp.where(grp == 0, POOL_WINDOWS[0],
                    jnp.where(grp == 1, POOL_WINDOWS[1],
                              jnp.where(grp == 2, POOL_WINDOWS[2], POOL_WINDOWS[3])))
    tpos = lax.broadcasted_iota(jnp.int32, (nb, length, POOL_WIDTH), 1) + (pos0 + j * length)
    cnt = jnp.minimum(tpos + 1, win).astype(F32)
    dpool = (wsum / cnt - r.extp[:, ph:end, :]).reshape(rows, POOL_WIDTH)
    r.cat_a[:, 0:POOL_WIDTH] = (_dot(dpool.astype(BF16), r.pw[...]) * r.pscale[...]).astype(BF16)
    yield 0.4

    r.exts[:, CONV_HIST:CONV_HIST + length, :] = (sc_c * sc_x).reshape(nb, length, SC_WIDTH)
    r.osc[...] = r.exts[:, CONV_HIST + length - (SC_CONV - 1):CONV_HIST + length, :]
    cconv = _causal_conv(r.exts, r.scw, slice(0, SC_WIDTH), length)
    r.cat_a[:, POOL_WIDTH:] = (sc_b * cconv.reshape(rows, SC_WIDTH)).astype(BF16)

    st.xy[...] = x + _dot(r.cat_a[...], r.wout[DN_WIDTH:, :])


def _stage_b(r, st, *, nb, length, chunk, groups):
    rows = nb * length
    grows = rows // groups
    chains = groups * DN_HEADS
    log2c = int(math.log2(chunk))

    tinv = []
    yield from _unit_lower_inverses([st.na[c] for c in range(chains)], log2c, tinv)
    for c in range(chains):
        nsol = _dot(tinv[c], st.rhs[c])
        r.nu[c] = nsol[:, :DN_HEAD_DIM]
        r.nw[c] = nsol[:, DN_HEAD_DIM:]
    yield 0.5

    if grows == chunk:
        assert groups == nb
        for c in range(chains):
            g, hh = divmod(c, DN_HEADS)
            s = r.od[g, hh]
            nwq = jnp.concatenate([r.nw[c], st.qg[c]], axis=0).astype(BF16)
            res = _dot(nwq, s.astype(BF16))
            vn16 = (res[:grows] - r.nu[c]).astype(BF16)
            r.o[g * grows:(g + 1) * grows, _head_slice(hh)] = res[grows:] + _dot(st.qk[c], vn16)
            r.od[g, hh] = s * st.egl[c, 0:1, :] + _dot(st.kdt[c].astype(BF16), vn16)
            yield 0.3
    else:
        assert groups == 1 and rows // chunk == nb
        seg = lax.broadcasted_iota(jnp.int32, (nb, 1, rows), 2) >> log2c
        own = seg == lax.broadcasted_iota(jnp.int32, (nb, 1, rows), 0)
        for hh in range(DN_HEADS):
            vn_parts, qs_parts = [], []
            for b in range(nb):
                rs = slice(b * chunk, (b + 1) * chunk)
                nwq = jnp.concatenate([r.nw[hh, rs, :], st.qg[hh, rs, :]], axis=0).astype(BF16)
                res = _dot(nwq, r.od[b, hh].astype(BF16))
                vn_parts.append(res[:chunk] - r.nu[hh, rs, :])
                qs_parts.append(res[chunk:])
            vn16 = jnp.concatenate(vn_parts, axis=0).astype(BF16)
            r.o[:, _head_slice(hh)] = jnp.concatenate(qs_parts, axis=0) + _dot(st.qk[hh], vn16)
            kd_own = jnp.where(own, st.kdt[hh][None], 0.0).astype(BF16).reshape(nb * DN_HEAD_DIM, rows)
            upd = _dot(kd_own, vn16).reshape(nb, DN_HEAD_DIM, DN_HEAD_DIM)
            egl = st.egl[hh].reshape(nb, chunk, DN_HEAD_DIM)[:, 0:1, :]
            r.od[:, hh, :, :] = r.od[:, hh, :, :] * egl + upd
            yield 0.3

    for hh in range(DN_HEADS):
        hs = _head_slice(hh)
        r.cat_b[:, hs] = (_rms_scale(r.o[:, hs]) * r.onorm[...] * _silu(st.z[:, hs])).astype(BF16)
    y = st.xy[...] + _dot(r.cat_b[...], r.wout[:DN_WIDTH, :])
    r.xo[...] = y.reshape(nb, length, D_MODEL)


_MIXER_IN = ("x", "sd", "sdc", "sp", "ssc", "nmix", "wgt", "wqkvz", "wrest", "dnw", "alog", "dtb", "onorm", "pw",
             "pscale", "scw", "wout")
_MIXER_OUT = ("xo", "od", "odc", "op", "osc")
_MIXER_PRIVATE = ("extd", "extp", "s2", "s4", "s8", "exts", "cat_a", "nu", "nw", "o", "cat_b")
_MIXER_SET = ("na", "qk", "rhs", "qg", "kdt", "egl", "z", "xy")


def _mixer_scratch(nb, length, chunk, n_sets, groups):
    rows = nb * length
    grows = rows // groups
    chains = groups * DN_HEADS
    one_chunk = grows == chunk
    private = [
        pltpu.VMEM((nb, CONV_HIST + length, 3 * DN_WIDTH), F32),
        pltpu.VMEM((nb, POOL_HIST + length, POOL_WIDTH), F32),
        pltpu.VMEM((nb, POOL_HIST + length, POOL_WIDTH), F32),
        pltpu.VMEM((nb, POOL_HIST + length, POOL_WIDTH), F32),
        pltpu.VMEM((nb, POOL_HIST + length, POOL_WIDTH), F32),
        pltpu.VMEM((nb, CONV_HIST + length, SC_WIDTH), F32),
        pltpu.VMEM((rows, D_REST), BF16),
        pltpu.VMEM((chains, grows, DN_HEAD_DIM), F32),
        pltpu.VMEM((chains, grows, DN_HEAD_DIM), F32),
        pltpu.VMEM((rows, DN_WIDTH), F32),
        pltpu.VMEM((rows, DN_WIDTH), BF16),
    ]
    one_set = [
        pltpu.VMEM((chains, grows, grows), BF16),
        pltpu.VMEM((chains, grows, grows), BF16),
        pltpu.VMEM((chains, grows, 2 * DN_HEAD_DIM), BF16),
        pltpu.VMEM((chains, grows, DN_HEAD_DIM), F32),
        pltpu.VMEM((chains, DN_HEAD_DIM, grows), F32),
        pltpu.VMEM((chains, SUBLANES if one_chunk else grows, DN_HEAD_DIM), F32),
        pltpu.VMEM((rows, DN_WIDTH), F32),
        pltpu.VMEM((rows, D_MODEL), F32),
    ]
    return private + one_set * n_sets


def _zero_other_layers(ref, layer):
    for l in range(ref.shape[0]):
        if l != layer:
            ref[l] = jnp.zeros(ref.shape[1:], F32)


def _mixer_kernel(*refs, nb, length, chunk, pos0, n_tiles, n_alias, layer, whole, groups):
    n_in, n_set = len(_MIXER_IN), len(_MIXER_SET)
    refs = refs[:n_in] + refs[n_in + n_alias:]
    names = _MIXER_IN + _MIXER_OUT + _MIXER_PRIVATE
    r = types.SimpleNamespace(**dict(zip(names, refs[:len(names)])))
    set_refs = refs[len(names):]
    sets = [types.SimpleNamespace(**dict(zip(_MIXER_SET, set_refs[i * n_set:(i + 1) * n_set])))
            for i in range(len(set_refs) // n_set)]
    j = pl.program_id(1)
    whole_refs = (r.od, r.odc, r.op, r.osc)
    if whole:
        r.od, r.odc, r.op, r.osc = (f.at[layer] for f in whole_refs)

    def init_state():
        if whole:
            for f in whole_refs:
                _zero_other_layers(f, layer)
        r.od[...] = r.sd[...]

    _carry_history(j, length,
                   [(r.extd, r.sdc, CONV_HIST, DN_CONV - 1), (r.extp, r.sp, POOL_HIST, POOL_BUF),
                    (r.exts, r.ssc, CONV_HIST, SC_CONV - 1)], first_tile=init_state)

    stage_a = functools.partial(_stage_a, j, r, nb=nb, length=length, chunk=chunk, pos0=pos0, groups=groups)
    stage_b = functools.partial(_stage_b, r, nb=nb, length=length, chunk=chunk, groups=groups)

    if len(sets) == 1:
        _interleave(stage_a(sets[0]))
        _interleave(stage_b(sets[0]))
        return

    middle = (j > 0) & (j < n_tiles)

    @pl.when(j == 0)
    def _():
        _interleave(stage_a(sets[0]))

    @pl.when(middle & (j % 2 == 1))
    def _():
        _interleave(stage_b(sets[0]), stage_a(sets[1]), speeds=B_AHEAD)

    @pl.when(middle & (j % 2 == 0))
    def _():
        _interleave(stage_b(sets[1]), stage_a(sets[0]), speeds=B_AHEAD)

    @pl.when(j == n_tiles)
    def _():
        _interleave(stage_b(sets[(n_tiles - 1) % 2]))


def _ffn_kernel(x_ref, sf_ref, nffn_ref, wg_ref, fw_ref, wu_ref, wd_ref, fnorm_ref, *rest,
                nb, length, final, layer, whole):
    xo_ref, of_full, extf_s = rest[-3:]
    of_ref = of_full.at[layer] if whole else of_full
    j = pl.program_id(1)
    rows = nb * length
    _carry_history(j, length, [(extf_s, sf_ref, CONV_HIST, FFN_CONV - 1)],
                   first_tile=functools.partial(_zero_other_layers, of_full, layer) if whole else None)
    x = x_ref[...].reshape(rows, D_MODEL)
    h16 = (_rms_scale(x) * nffn_ref[...]).astype(BF16)
    extf_s[:, CONV_HIST:CONV_HIST + length, :] = _dot(h16, wg_ref[...]).reshape(nb, length, D_FF)
    of_ref[...] = extf_s[:, CONV_HIST + length - (FFN_CONV - 1):CONV_HIST + length, :]
    gate = _causal_conv(extf_s, fw_ref, slice(0, D_FF), length)
    up = _dot(h16, wu_ref[...])
    act = (_silu(gate).reshape(rows, D_FF) * up).astype(BF16)
    y = x + _dot(act, wd_ref[...])
    if final:
        y = _rms_scale(y) * fnorm_ref[...]
    xo_ref[...] = y.reshape(nb, length, D_MODEL)


def _layer_spec(arr, layer):
    tail = arr.shape[1:]
    zeros = (0,) * len(tail)
    return pl.BlockSpec((None,) + tail, lambda i, j: (layer,) + zeros, pipeline_mode=pl.Buffered(1))


def _state_spec(arr, layer, nb):
    tail = arr.shape[2:]
    zeros = (0,) * len(tail)
    return pl.BlockSpec((None, nb) + tail, lambda i, j: (layer, i) + zeros)


def _state_out_spec(shape, layer, nb, whole):
    tail = shape[2:]
    zeros = (0,) * len(tail)
    if whole:
        return pl.BlockSpec((shape[0], nb) + tail, lambda i, j: (0, i) + zeros)
    return pl.BlockSpec((None, nb) + tail, lambda i, j: (layer, i) + zeros)


def _alias_args(prev, n_in, first_out):
    if prev is None:
        return [], [], {}
    specs = [pl.BlockSpec(memory_space=pl.ANY)] * len(prev)
    return list(prev), specs, {n_in + k: first_out + k for k in range(len(prev))}


def _mixer_call(x, states, state_layer, layer, depth, w, prev, *, nb, length, chunk, pos0, groups):
    batch, seq, _ = x.shape
    sd, sdc, sp, ssc = states
    n_tiles = seq // length
    pipelined = n_tiles > 1
    if pipelined:
        last = n_tiles - 1
        grid = (batch // nb, n_tiles + 1)
        x_in_spec = pl.BlockSpec((nb, length, D_MODEL), lambda i, j: (i, jnp.minimum(j, last), 0))
        x_out_spec = pl.BlockSpec((nb, length, D_MODEL), lambda i, j: (i, jnp.maximum(j - 1, 0), 0))
    else:
        grid = (batch // nb, n_tiles)
        x_in_spec = x_out_spec = pl.BlockSpec((nb, length, D_MODEL), lambda i, j: (i, j, 0))
    weights = tuple(w[name] for name in _MIXER_IN[5:])
    out_shape = (
        jax.ShapeDtypeStruct(x.shape, F32),
        jax.ShapeDtypeStruct((depth, batch) + sd.shape[2:], F32),
        jax.ShapeDtypeStruct((depth, batch) + sdc.shape[2:], F32),
        jax.ShapeDtypeStruct((depth, batch) + sp.shape[2:], F32),
        jax.ShapeDtypeStruct((depth, batch) + ssc.shape[2:], F32),
    )
    operands = (x, sd, sdc, sp, ssc) + weights
    alias_ops, alias_specs, aliases = _alias_args(prev, len(operands), 1)
    whole = prev is None
    kern = functools.partial(_mixer_kernel, nb=nb, length=length, chunk=chunk, pos0=pos0, n_tiles=n_tiles,
                             n_alias=len(alias_ops), layer=layer, whole=whole, groups=groups)
    return pl.pallas_call(
        kern,
        grid=grid,
        in_specs=[x_in_spec, _state_spec(sd, state_layer, nb), _state_spec(sdc, state_layer, nb),
                  _state_spec(sp, state_layer, nb), _state_spec(ssc, state_layer, nb)]
                 + [_layer_spec(a, layer) for a in weights] + alias_specs,
        out_specs=(x_out_spec,) + tuple(_state_out_spec(s.shape, layer, nb, whole) for s in out_shape[1:]),
        out_shape=out_shape,
        input_output_aliases=aliases,
        scratch_shapes=_mixer_scratch(nb, length, chunk, 2 if pipelined else 1, groups),
        compiler_params=pltpu.CompilerParams(
            dimension_semantics=("parallel", "arbitrary"), vmem_limit_bytes=VMEM_LIMIT_BYTES),
        name="mixer",
    )(*operands, *alias_ops)


def _ffn_call(x, sf, state_layer, layer, depth, w, prev, *, nb, length, final):
    batch, seq, _ = x.shape
    grid = (batch // nb, seq // length)
    x_spec = pl.BlockSpec((nb, length, D_MODEL), lambda i, j: (i, j, 0))
    weights = (w["nffn"], w["wg"], w["fw"], w["wu"], w["wd"])
    fnorm = w["fnorm"]
    out_shape = (jax.ShapeDtypeStruct(x.shape, F32),
                 jax.ShapeDtypeStruct((depth, batch) + sf.shape[2:], F32))
    operands = (x, sf) + weights + (fnorm,)
    alias_ops, alias_specs, aliases = _alias_args(prev, len(operands), 1)
    whole = prev is None
    kern = functools.partial(_ffn_kernel, nb=nb, length=length, final=final, layer=layer, whole=whole)
    return pl.pallas_call(
        kern,
        grid=grid,
        in_specs=[x_spec, _state_spec(sf, state_layer, nb)] + [_layer_spec(a, layer) for a in weights]
                 + [pl.BlockSpec(fnorm.shape, lambda i, j: (0, 0))] + alias_specs,
        out_specs=(x_spec, _state_out_spec(out_shape[1].shape, layer, nb, whole)),
        out_shape=out_shape,
        input_output_aliases=aliases,
        scratch_shapes=[pltpu.VMEM((nb, CONV_HIST + length, D_FF), F32)],
        compiler_params=pltpu.CompilerParams(
            dimension_semantics=("parallel", "arbitrary"), vmem_limit_bytes=VMEM_LIMIT_BYTES),
        name="ffn",
    )(*operands, *alias_ops)


def _prepare_weights(norm_mix, w_in, dn_conv_w, dn_a_log, dn_dt_bias, dn_out_norm, pool_w, pool_scale,
                     sconv_w, w_out, norm_ffn, w_ffn_gate, ffn_conv_w, w_ffn_up, w_ffn_down, final_norm):
    depth = w_in.shape[0]
    assert N_GATE == SUBLANES
    g0 = W_Z + DN_WIDTH
    rest0 = g0 + N_GATE
    wgt = jnp.swapaxes(w_in[:, :, g0:g0 + N_GATE], 1, 2).astype(BF16)
    pad = jnp.zeros((depth, SUBLANES - DN_HEADS, 1), F32)
    eye = jnp.eye(len(POOL_WINDOWS), dtype=F32)
    pw = (pool_w[:, :, :, None, :] * eye[None, :, None, :, None]).reshape(depth, POOL_WIDTH, POOL_WIDTH)
    return {
        "nmix": norm_mix[:, None, :],
        "wgt": wgt,
        "wqkvz": w_in[:, :, :g0].astype(BF16),
        "wrest": w_in[:, :, rest0:rest0 + REST_WIDTH].astype(BF16),
        "dnw": dn_conv_w,
        "alog": jnp.concatenate([dn_a_log[:, :, None], pad], axis=1),
        "dtb": jnp.concatenate([dn_dt_bias[:, :, None], pad], axis=1),
        "onorm": dn_out_norm[:, None, :],
        "pw": pw.astype(BF16),
        "pscale": pool_scale[:, None, :],
        "scw": sconv_w,
        "wout": w_out.astype(BF16),
        "nffn": norm_ffn[:, None, :],
        "wg": w_ffn_gate.astype(BF16),
        "fw": ffn_conv_w,
        "wu": w_ffn_up.astype(BF16),
        "wd": w_ffn_down.astype(BF16),
        "fnorm": final_norm[None, :],
    }


def _trunk(x, states, state_layers, weights, depth, *, nb_mix, groups, nb_ffn, length, ffn_length, chunk,
           pos0):
    sd, sdc, sp, ssc, sf = states
    mix_out, ffn_out = None, None
    for l in range(depth):
        sl = state_layers[l]
        x, *mix_out = _mixer_call(x, (sd, sdc, sp, ssc), sl, l, depth, weights, mix_out,
                                  nb=nb_mix, length=length, chunk=chunk, pos0=pos0, groups=groups)
        x, *ffn_out = _ffn_call(x, sf, sl, l, depth, weights, ffn_out,
                                nb=nb_ffn, length=ffn_length, final=(l == depth - 1))
    return x, tuple(mix_out) + tuple(ffn_out)


PROMPT_TILE = 256
PROMPT_MIX_ROWS = 2
PROMPT_FFN_TILE = 512
SAMPLE_MIX_ROWS = 16
SAMPLE_FFN_ROWS = 32


def kernel(x_prompt, x_sample, state_delta, state_delta_conv, state_pool, state_sconv, state_ffn_conv,
           norm_mix, w_in, dn_conv_w, dn_a_log, dn_dt_bias, dn_out_norm, pool_w, pool_scale,
           sconv_w, w_out, norm_ffn, w_ffn_gate, ffn_conv_w, w_ffn_up, w_ffn_down, final_norm):
    depth = w_in.shape[0]
    weights = _prepare_weights(norm_mix, w_in, dn_conv_w, dn_a_log, dn_dt_bias, dn_out_norm, pool_w,
                               pool_scale, sconv_w, w_out, norm_ffn, w_ffn_gate, ffn_conv_w, w_ffn_up,
                               w_ffn_down, final_norm)

    batch, seq, _ = x_prompt.shape
    dec_batch, dec_seq, _ = x_sample.shape
    sample_states = (state_delta, state_delta_conv, state_pool, state_sconv, state_ffn_conv)
    zero_states = tuple(jnp.zeros((1, batch) + s.shape[2:], F32) for s in sample_states)

    y_prompt, p_st = _trunk(x_prompt, zero_states, (0,) * depth, weights, depth,
                            nb_mix=PROMPT_MIX_ROWS, groups=PROMPT_MIX_ROWS, nb_ffn=1,
                            length=min(PROMPT_TILE, seq), ffn_length=min(PROMPT_FFN_TILE, seq),
                            chunk=min(PROMPT_TILE, seq), pos0=0)
    y_sample, s_st = _trunk(x_sample, sample_states, tuple(range(depth)), weights, depth,
                            nb_mix=SAMPLE_MIX_ROWS, groups=1, nb_ffn=SAMPLE_FFN_ROWS,
                            length=dec_seq, ffn_length=dec_seq,
                            chunk=min(DN_CHUNK, dec_seq), pos0=PAST_LEN)
    return (y_prompt, y_sample) + p_st + s_st
```

```python
import functools
import math
import types

import jax
import jax.numpy as jnp
from jax import lax
from jax.experimental import pallas as pl
from jax.experimental.pallas import tpu as pltpu

F32 = jnp.float32
BF16 = jnp.bfloat16

LANES = 128
SUBLANES = 8
BF16_ROWS = 16
VMEM_LIMIT_BYTES = 56 * 1024 * 1024

D_MODEL = 1024
DN_HEADS = 4
DN_HEAD_DIM = 128
DN_WIDTH = DN_HEADS * DN_HEAD_DIM
DN_CONV = 4
DN_CHUNK = 64
POOL_WINDOWS = (2, 4, 8, 16)
POOL_WIDTH = 256
POOL_GROUP_DIM = POOL_WIDTH // len(POOL_WINDOWS)
POOL_BUF = max(POOL_WINDOWS) - 1
SC_WIDTH = 256
SC_CONV = 3
D_FF = 2816
FFN_CONV = 3
EPS = 1e-6
PAST_LEN = 16384

N_GATE = 2 * DN_HEADS
W_Z = 3 * DN_WIDTH
REST_WIDTH = POOL_WIDTH + 3 * SC_WIDTH
D_REST = POOL_WIDTH + SC_WIDTH

CONV_HIST = SUBLANES
POOL_HIST = 24

L_GC, L_BETA, L_EGC, L_KDF, L_EGL = 0, DN_HEADS, SUBLANES, 2 * SUBLANES, 3 * SUBLANES


def _silu(t):
    return t / (1.0 + jnp.exp(-t))


def _softplus(t):
    return jnp.maximum(t, 0.0) + jnp.log1p(jnp.exp(-jnp.abs(t)))


def _rms_scale(t):
    return t * lax.rsqrt(jnp.mean(t * t, axis=-1, keepdims=True) + EPS)


def _dot(a, b):
    return jnp.dot(a, b, preferred_element_type=F32)


def _dot_nt(a, b):
    return lax.dot_general(a, b, (((1,), (1,)), ((), ())), preferred_element_type=F32)


def _head_slice(hh):
    return slice(hh * DN_HEAD_DIM, (hh + 1) * DN_HEAD_DIM)


B_AHEAD = (1.5, 1.0)


def _interleave(*stages, speeds=None):
    speeds = speeds or [1.0] * len(stages)
    clock = [0.0] * len(stages)
    live = list(range(len(stages)))
    while live:
        i = min(live, key=lambda s: clock[s])
        try:
            clock[i] += next(stages[i]) / speeds[i]
        except StopIteration:
            live.remove(i)


def _carry_history(j, length, items, first_tile=None):
    @pl.when(j == 0)
    def _():
        if first_tile is not None:
            first_tile()
        for ext, state_ref, hist, nbuf in items:
            ext[:, 0:hist - nbuf, :] = jnp.zeros((ext.shape[0], hist - nbuf, ext.shape[2]), F32)
            ext[:, hist - nbuf:hist, :] = state_ref[...]

    @pl.when(j != 0)
    def _():
        for ext, state_ref, hist, nbuf in items:
            ext[:, hist - nbuf:hist, :] = ext[:, hist + length - nbuf:hist + length, :]


def _causal_conv(ext, w_ref, cs, length, bs=slice(None)):
    taps = w_ref.shape[0]
    tiles = length // SUBLANES
    e4 = ext[bs, 0:CONV_HIST + length, cs]
    nb, _, width = e4.shape
    e4 = e4.reshape(nb, tiles + 1, SUBLANES, width)
    sub4 = lax.broadcasted_iota(jnp.int32, (nb, tiles, SUBLANES, width), 2)
    acc = None
    for i in range(taps):
        s = taps - 1 - i
        if s:
            rot = pltpu.roll(e4, s, axis=2)
            term = jnp.where(sub4 >= s, rot[:, 1:], rot[:, :-1])
        else:
            term = e4[:, 1:]
        term = term * w_ref[i:i + 1, cs][None, None]
        acc = term if acc is None else acc + term
    return acc.reshape(nb, length, width)


def _unit_lower_inverses(na16, log2c, out):
    heads = range(len(na16))
    n = na16[0].shape[0]
    row = lax.broadcasted_iota(jnp.int32, (n, n), 0)
    col = lax.broadcasted_iota(jnp.int32, (n, n), 1)
    pair = ((row >> 1) == (col >> 1)) & ((row & 1) == 1) & ((col & 1) == 0)
    eye = jnp.where(row == col, 1.0, 0.0).astype(F32)
    t16 = [jnp.where(pair, na16[i].astype(F32), eye).astype(BF16) for i in heads]
    yield 0.3
    for lvl in range(1, log2c):
        blk = 1 << lvl
        if blk < BF16_ROWS:
            off = (((row >> (lvl + 1)) == (col >> (lvl + 1)))
                   & (((row >> lvl) & 1) == 1) & (((col >> lvl) & 1) == 0))
            p16 = [_dot(na16[i], t16[i]).astype(BF16) for i in heads]
            yield 0.3
            t16 = [t16[i] + jnp.where(off, _dot(t16[i], p16[i]), 0.0).astype(BF16) for i in heads]
        else:
            pairs = n // (2 * blk)
            rowh = lax.broadcasted_iota(jnp.int32, (n // 2, n), 0)
            colh = lax.broadcasted_iota(jnp.int32, (n // 2, n), 1)
            offh = ((colh >> (lvl + 1)) == (rowh >> lvl)) & (((colh >> lvl) & 1) == 0)
            t4 = [t16[i].reshape(pairs, 2, blk, n) for i in heads]
            na_odd = [na16[i].reshape(pairs, 2, blk, n)[:, 1].reshape(n // 2, n) for i in heads]
            t_odd = [t4[i][:, 1].reshape(n // 2, n) for i in heads]
            p = [_dot(na_odd[i], t16[i]) for i in heads]
            p16 = [jnp.broadcast_to(p[i].reshape(pairs, 1, blk, n), (pairs, 2, blk, n))
                   .reshape(n, n).astype(BF16) for i in heads]
            yield 0.3
            x_odd = [jnp.where(offh, _dot(t_odd[i], p16[i]), 0.0).astype(BF16) for i in heads]
            t16 = [jnp.concatenate([t4[i][:, 0:1], (t_odd[i] + x_odd[i]).reshape(pairs, 1, blk, n)],
                                   axis=1).reshape(n, n) for i in heads]
        yield 0.3
    out.extend(t16)


def _stage_a(j, r, st, *, nb, length, chunk, pos0, groups):
    rows = nb * length
    grows, gnb = rows // groups, nb // groups
    log2c = int(math.log2(chunk))

    x = r.x[...].reshape(rows, D_MODEL)
    h16 = (_rms_scale(x) * r.nmix[...]).astype(BF16)

    gates_t = _dot_nt(r.wgt[...], h16)
    g_t = -jnp.exp(r.alog[...]) * _softplus(gates_t + r.dtb[...])
    beta_t = 1.0 / (1.0 + jnp.exp(-gates_t))
    pos_in_chunk = lax.broadcasted_iota(jnp.int32, (SUBLANES, rows), 1) & (chunk - 1)
    gc_t = g_t
    sfx_t = g_t
    step = 1
    while step < chunk:
        gc_t = gc_t + jnp.where(pos_in_chunk >= step, pltpu.roll(gc_t, step, axis=1), 0.0)
        sfx_t = sfx_t + jnp.where(pos_in_chunk + step < chunk,
                                  pltpu.roll(sfx_t, rows - step, axis=1), 0.0)
        step *= 2
    rest_t = sfx_t - g_t
    sub = lax.broadcasted_iota(jnp.int32, (SUBLANES, rows), 0)
    cols = jnp.concatenate(
        [jnp.where(sub < DN_HEADS, gc_t, -beta_t), jnp.exp(gc_t), jnp.exp(rest_t), jnp.exp(gc_t + rest_t),
         jnp.zeros((LANES - 4 * SUBLANES, rows), F32)], axis=0).T

    def colb(lane, g, width=LANES):
        return jnp.broadcast_to(cols[g * grows:(g + 1) * grows, lane:lane + 1], (grows, width))

    yield 1.0
    for part in range(3):
        cs = slice(part * DN_WIDTH, (part + 1) * DN_WIDTH)
        r.extd[:, CONV_HIST:CONV_HIST + length, cs] = _dot(h16, r.wqkvz[:, cs]).reshape(nb, length, DN_WIDTH)
        yield 0.3
    r.odc[...] = r.extd[:, CONV_HIST + length - (DN_CONV - 1):CONV_HIST + length, :]

    def conv_block(ci, g):
        cs = slice(ci * LANES, (ci + 1) * LANES)
        bs = slice(g * gnb, (g + 1) * gnb)
        return _silu(_causal_conv(r.extd, r.dnw, cs, length, bs)).reshape(grows, LANES)

    row = lax.broadcasted_iota(jnp.int32, (grows, grows), 0)
    col = lax.broadcasted_iota(jnp.int32, (grows, grows), 1)
    same_chunk = (row >> log2c) == (col >> log2c)
    m_incl = same_chunk & (row >= col)
    m_strict = same_chunk & (row > col)

    for c in range(groups * DN_HEADS):
        g, hh = divmod(c, DN_HEADS)
        nbeta_b = colb(L_BETA + hh, g)
        egc_b = colb(L_EGC + hh, g)
        qc = conv_block(hh, g)
        yield 0.2
        kc = conv_block(DN_HEADS + hh, g)
        yield 0.2
        vc = conv_block(2 * DN_HEADS + hh, g)
        q = qc * (lax.rsqrt(jnp.sum(qc * qc, axis=-1, keepdims=True) + EPS) * (DN_HEAD_DIM ** -0.5))
        k = kc * lax.rsqrt(jnp.sum(kc * kc, axis=-1, keepdims=True) + EPS)
        nkb = k * nbeta_b
        k16 = k.astype(BF16)
        yield 0.2
        diff = colb(L_GC + hh, g, grows) - gc_t[hh:hh + 1, g * grows:(g + 1) * grows]
        decay = jnp.where(m_incl, jnp.exp(diff), 0.0)
        na = jnp.where(m_strict, _dot_nt(nkb.astype(BF16), k16) * decay, 0.0)
        st.na[c] = na.astype(BF16)
        yield 0.25
        st.qk[c] = (_dot_nt(q.astype(BF16), k16) * decay).astype(BF16)
        st.rhs[c] = jnp.concatenate([vc * nbeta_b, nkb * egc_b], axis=1).astype(BF16)
        st.qg[c] = q * egc_b
        st.kdt[c] = (k * colb(L_KDF + hh, g)).T
        st.egl[c] = colb(L_EGL + hh, g)[0:st.egl.shape[1]]
        yield 0.25

    prest = _dot(h16, r.wrest[...])
    st.z[...] = _dot(h16, r.wqkvz[:, W_Z:W_Z + DN_WIDTH])
    p_in = prest[:, 0:POOL_WIDTH]
    sc_x = prest[:, POOL_WIDTH:POOL_WIDTH + SC_WIDTH]
    sc_b = prest[:, POOL_WIDTH + SC_WIDTH:POOL_WIDTH + 2 * SC_WIDTH]
    sc_c = prest[:, POOL_WIDTH + 2 * SC_WIDTH:]
    yield 0.5

    ph = POOL_HIST
    r.extp[:, ph:ph + length, :] = p_in.reshape(nb, length, POOL_WIDTH)
    r.op[...] = r.extp[:, ph + length - POOL_BUF:ph + length, :]
    zero8 = jnp.zeros((nb, SUBLANES, POOL_WIDTH), F32)
    end = ph + length
    r.s2[:, 0:8, :] = zero8
    r.s4[:, 0:8, :] = zero8
    r.s8[:, 0:8, :] = zero8
    r.s2[:, 8:end, :] = r.extp[:, 8:end, :] + r.extp[:, 7:end - 1, :]
    r.s4[:, 8:end, :] = r.s2[:, 8:end, :] + r.s2[:, 6:end - 2, :]
    r.s8[:, 8:end, :] = r.s4[:, 8:end, :] + r.s4[:, 4:end - 4, :]
    s16 = r.s8[:, ph:end, :] + r.s8[:, ph - 8:end - 8, :]
    lane3 = lax.broadcasted_iota(jnp.int32, (nb, length, POOL_WIDTH), 2)
    grp = lane3 // POOL_GROUP_DIM
    wsum = jnp.where(grp == 0, r.s2[:, ph:end, :],
                     jnp.where(grp == 1, r.s4[:, ph:end, :],
                               jnp.where(grp == 2, r.s8[:, ph:end, :], s16)))
    win = jnp.where(grp == 0, POOL_WINDOWS[0],
                    jnp.where(grp == 1, POOL_WINDOWS[1],
                              jnp.where(grp == 2, POOL_WINDOWS[2], POOL_WINDOWS[3])))
    tpos = lax.broadcasted_iota(jnp.int32, (nb, length, POOL_WIDTH), 1) + (pos0 + j * length)
    cnt = jnp.minimum(tpos + 1, win).astype(F32)
    dpool = (wsum / cnt - r.extp[:, ph:end, :]).reshape(rows, POOL_WIDTH)
    r.cat_a[:, 0:POOL_WIDTH] = (_dot(dpool.astype(BF16), r.pw[...]) * r.pscale[...]).astype(BF16)
    yield 0.4

    r.exts[:, CONV_HIST:CONV_HIST + length, :] = (sc_c * sc_x).reshape(nb, length, SC_WIDTH)
    r.osc[...] = r.exts[:, CONV_HIST + length - (SC_CONV - 1):CONV_HIST + length, :]
    cconv = _causal_conv(r.exts, r.scw, slice(0, SC_WIDTH), length)
    r.cat_a[:, POOL_WIDTH:] = (sc_b * cconv.reshape(rows, SC_WIDTH)).astype(BF16)

    st.xy[...] = x + _dot(r.cat_a[...], r.wout[DN_WIDTH:, :])


def _stage_b(r, st, *, nb, length, chunk, groups):
    rows = nb * length
    grows = rows // groups
    chains = groups * DN_HEADS
    log2c = int(math.log2(chunk))

    tinv = []
    yield from _unit_lower_inverses([st.na[c] for c in range(chains)], log2c, tinv)
    for c in range(chains):
        nsol = _dot(tinv[c], st.rhs[c])
        r.nu[c] = nsol[:, :DN_HEAD_DIM]
        r.nw[c] = nsol[:, DN_HEAD_DIM:]
    yield 0.5

    if grows == chunk:
        assert groups == nb
        assert DN_HEADS % 2 == 0
        zero = jnp.zeros((DN_HEAD_DIM, DN_HEAD_DIM), BF16)
        for c0 in range(0, chains, 2):
            g, h0 = divmod(c0, DN_HEADS)
            pair = ((c0, h0, slice(0, DN_HEAD_DIM)), (c0 + 1, h0 + 1, slice(DN_HEAD_DIM, 2 * DN_HEAD_DIM)))
            s_pair = [r.od[g, hh] for _, hh, _ in pair]
            s2 = jnp.concatenate([jnp.concatenate([s_pair[0].astype(BF16), zero], axis=1),
                                  jnp.concatenate([zero, s_pair[1].astype(BF16)], axis=1)], axis=0)
            nwq = jnp.concatenate([jnp.concatenate([r.nw[c] for c, _, _ in pair], axis=1),
                                   jnp.concatenate([st.qg[c] for c, _, _ in pair], axis=1)],
                                  axis=0).astype(BF16)
            res = _dot(nwq, s2)
            for (c, hh, lanes), s in zip(pair, s_pair):
                vn16 = (res[:grows, lanes] - r.nu[c]).astype(BF16)
                r.o[g * grows:(g + 1) * grows, _head_slice(hh)] = res[grows:, lanes] + _dot(st.qk[c], vn16)
                r.od[g, hh] = s * st.egl[c, 0:1, :] + _dot(st.kdt[c].astype(BF16), vn16)
            yield 0.6
    else:
        assert groups == 1 and rows // chunk == nb
        seg = lax.broadcasted_iota(jnp.int32, (nb, 1, rows), 2) >> log2c
        own = seg == lax.broadcasted_iota(jnp.int32, (nb, 1, rows), 0)
        for hh in range(DN_HEADS):
            vn_parts, qs_parts = [], []
            for b in range(nb):
                rs = slice(b * chunk, (b + 1) * chunk)
                nwq = jnp.concatenate([r.nw[hh, rs, :], st.qg[hh, rs, :]], axis=0).astype(BF16)
                res = _dot(nwq, r.od[b, hh].astype(BF16))
                vn_parts.append(res[:chunk] - r.nu[hh, rs, :])
                qs_parts.append(res[chunk:])
            vn16 = jnp.concatenate(vn_parts, axis=0).astype(BF16)
            r.o[:, _head_slice(hh)] = jnp.concatenate(qs_parts, axis=0) + _dot(st.qk[hh], vn16)
            kd_own = jnp.where(own, st.kdt[hh][None], 0.0).astype(BF16).reshape(nb * DN_HEAD_DIM, rows)
            upd = _dot(kd_own, vn16).reshape(nb, DN_HEAD_DIM, DN_HEAD_DIM)
            egl = st.egl[hh].reshape(nb, chunk, DN_HEAD_DIM)[:, 0:1, :]
            r.od[:, hh, :, :] = r.od[:, hh, :, :] * egl + upd
            yield 0.3

    for hh in range(DN_HEADS):
        hs = _head_slice(hh)
        r.cat_b[:, hs] = (_rms_scale(r.o[:, hs]) * r.onorm[...] * _silu(st.z[:, hs])).astype(BF16)
    y = st.xy[...] + _dot(r.cat_b[...], r.wout[:DN_WIDTH, :])
    r.xo[...] = y.reshape(nb, length, D_MODEL)


_MIXER_IN = ("x", "sd", "sdc", "sp", "ssc", "nmix", "wgt", "wqkvz", "wrest", "dnw", "alog", "dtb", "onorm", "pw",
             "pscale", "scw", "wout")
_MIXER_OUT = ("xo", "od", "odc", "op", "osc")
_MIXER_PRIVATE = ("extd", "extp", "s2", "s4", "s8", "exts", "cat_a", "nu", "nw", "o", "cat_b")
_MIXER_SET = ("na", "qk", "rhs", "qg", "kdt", "egl", "z", "xy")


def _mixer_scratch(nb, length, chunk, n_sets, groups):
    rows = nb * length
    grows = rows // groups
    chains = groups * DN_HEADS
    one_chunk = grows == chunk
    private = [
        pltpu.VMEM((nb, CONV_HIST + length, 3 * DN_WIDTH), F32),
        pltpu.VMEM((nb, POOL_HIST + length, POOL_WIDTH), F32),
        pltpu.VMEM((nb, POOL_HIST + length, POOL_WIDTH), F32),
        pltpu.VMEM((nb, POOL_HIST + length, POOL_WIDTH), F32),
        pltpu.VMEM((nb, POOL_HIST + length, POOL_WIDTH), F32),
        pltpu.VMEM((nb, CONV_HIST + length, SC_WIDTH), F32),
        pltpu.VMEM((rows, D_REST), BF16),
        pltpu.VMEM((chains, grows, DN_HEAD_DIM), F32),
        pltpu.VMEM((chains, grows, DN_HEAD_DIM), F32),
        pltpu.VMEM((rows, DN_WIDTH), F32),
        pltpu.VMEM((rows, DN_WIDTH), BF16),
    ]
    one_set = [
        pltpu.VMEM((chains, grows, grows), BF16),
        pltpu.VMEM((chains, grows, grows), BF16),
        pltpu.VMEM((chains, grows, 2 * DN_HEAD_DIM), BF16),
        pltpu.VMEM((chains, grows, DN_HEAD_DIM), F32),
        pltpu.VMEM((chains, DN_HEAD_DIM, grows), F32),
        pltpu.VMEM((chains, SUBLANES if one_chunk else grows, DN_HEAD_DIM), F32),
        pltpu.VMEM((rows, DN_WIDTH), F32),
        pltpu.VMEM((rows, D_MODEL), F32),
    ]
    return private + one_set * n_sets


def _zero_other_layers(ref, layer):
    for l in range(ref.shape[0]):
        if l != layer:
            ref[l] = jnp.zeros(ref.shape[1:], F32)


def _mixer_kernel(*refs, nb, length, chunk, pos0, n_tiles, n_alias, layer, whole, groups):
    n_in, n_set = len(_MIXER_IN), len(_MIXER_SET)
    refs = refs[:n_in] + refs[n_in + n_alias:]
    names = _MIXER_IN + _MIXER_OUT + _MIXER_PRIVATE
    r = types.SimpleNamespace(**dict(zip(names, refs[:len(names)])))
    set_refs = refs[len(names):]
    sets = [types.SimpleNamespace(**dict(zip(_MIXER_SET, set_refs[i * n_set:(i + 1) * n_set])))
            for i in range(len(set_refs) // n_set)]
    j = pl.program_id(1)
    whole_refs = (r.od, r.odc, r.op, r.osc)
    if whole:
        r.od, r.odc, r.op, r.osc = (f.at[layer] for f in whole_refs)

    def init_state():
        if whole:
            for f in whole_refs:
                _zero_other_layers(f, layer)
        r.od[...] = r.sd[...]

    _carry_history(j, length,
                   [(r.extd, r.sdc, CONV_HIST, DN_CONV - 1), (r.extp, r.sp, POOL_HIST, POOL_BUF),
                    (r.exts, r.ssc, CONV_HIST, SC_CONV - 1)], first_tile=init_state)

    stage_a = functools.partial(_stage_a, j, r, nb=nb, length=length, chunk=chunk, pos0=pos0, groups=groups)
    stage_b = functools.partial(_stage_b, r, nb=nb, length=length, chunk=chunk, groups=groups)

    if len(sets) == 1:
        _interleave(stage_a(sets[0]))
        _interleave(stage_b(sets[0]))
        return

    middle = (j > 0) & (j < n_tiles)

    @pl.when(j == 0)
    def _():
        _interleave(stage_a(sets[0]))

    @pl.when(middle & (j % 2 == 1))
    def _():
        _interleave(stage_b(sets[0]), stage_a(sets[1]), speeds=B_AHEAD)

    @pl.when(middle & (j % 2 == 0))
    def _():
        _interleave(stage_b(sets[1]), stage_a(sets[0]), speeds=B_AHEAD)

    @pl.when(j == n_tiles)
    def _():
        _interleave(stage_b(sets[(n_tiles - 1) % 2]))


def _ffn_kernel(x_ref, sf_ref, nffn_ref, wg_ref, fw_ref, wu_ref, wd_ref, fnorm_ref, *rest,
                nb, length, final, layer, whole):
    xo_ref, of_full, extf_s = rest[-3:]
    of_ref = of_full.at[layer] if whole else of_full
    j = pl.program_id(1)
    rows = nb * length
    _carry_history(j, length, [(extf_s, sf_ref, CONV_HIST, FFN_CONV - 1)],
                   first_tile=functools.partial(_zero_other_layers, of_full, layer) if whole else None)
    x = x_ref[...].reshape(rows, D_MODEL)
    h16 = (_rms_scale(x) * nffn_ref[...]).astype(BF16)
    extf_s[:, CONV_HIST:CONV_HIST + length, :] = _dot(h16, wg_ref[...]).reshape(nb, length, D_FF)
    of_ref[...] = extf_s[:, CONV_HIST + length - (FFN_CONV - 1):CONV_HIST + length, :]
    gate = _causal_conv(extf_s, fw_ref, slice(0, D_FF), length)
    up = _dot(h16, wu_ref[...])
    act = (_silu(gate).reshape(rows, D_FF) * up).astype(BF16)
    y = x + _dot(act, wd_ref[...])
    if final:
        y = _rms_scale(y) * fnorm_ref[...]
    xo_ref[...] = y.reshape(nb, length, D_MODEL)


def _layer_spec(arr, layer, cols=None):
    tail = arr.shape[1:] if cols is None else arr.shape[1:-1] + (cols,)
    zeros = (0,) * len(tail)
    return pl.BlockSpec((None,) + tail, lambda i, j: (layer,) + zeros, pipeline_mode=pl.Buffered(1))


def _state_spec(arr, layer, nb):
    tail = arr.shape[2:]
    zeros = (0,) * len(tail)
    return pl.BlockSpec((None, nb) + tail, lambda i, j: (layer, i) + zeros)


def _state_out_spec(shape, layer, nb, whole):
    tail = shape[2:]
    zeros = (0,) * len(tail)
    if whole:
        return pl.BlockSpec((shape[0], nb) + tail, lambda i, j: (0, i) + zeros)
    return pl.BlockSpec((None, nb) + tail, lambda i, j: (layer, i) + zeros)


def _alias_args(prev, n_in, first_out):
    if prev is None:
        return [], [], {}
    specs = [pl.BlockSpec(memory_space=pl.ANY)] * len(prev)
    return list(prev), specs, {n_in + k: first_out + k for k in range(len(prev))}


def _mixer_call(x, states, state_layer, layer, depth, w, prev, *, nb, length, chunk, pos0, groups):
    batch, seq, _ = x.shape
    sd, sdc, sp, ssc = states
    n_tiles = seq // length
    pipelined = n_tiles > 1
    if pipelined:
        last = n_tiles - 1
        grid = (batch // nb, n_tiles + 1)
        x_in_spec = pl.BlockSpec((nb, length, D_MODEL), lambda i, j: (i, jnp.minimum(j, last), 0))
        x_out_spec = pl.BlockSpec((nb, length, D_MODEL), lambda i, j: (i, jnp.maximum(j - 1, 0), 0))
    else:
        grid = (batch // nb, n_tiles)
        x_in_spec = x_out_spec = pl.BlockSpec((nb, length, D_MODEL), lambda i, j: (i, j, 0))
    weights = tuple(w[name] for name in _MIXER_IN[5:])
    out_shape = (
        jax.ShapeDtypeStruct(x.shape, F32),
        jax.ShapeDtypeStruct((depth, batch) + sd.shape[2:], F32),
        jax.ShapeDtypeStruct((depth, batch) + sdc.shape[2:], F32),
        jax.ShapeDtypeStruct((depth, batch) + sp.shape[2:], F32),
        jax.ShapeDtypeStruct((depth, batch) + ssc.shape[2:], F32),
    )
    operands = (x, sd, sdc, sp, ssc) + weights
    alias_ops, alias_specs, aliases = _alias_args(prev, len(operands), 1)
    whole = prev is None
    kern = functools.partial(_mixer_kernel, nb=nb, length=length, chunk=chunk, pos0=pos0, n_tiles=n_tiles,
                             n_alias=len(alias_ops), layer=layer, whole=whole, groups=groups)
    return pl.pallas_call(
        kern,
        grid=grid,
        in_specs=[x_in_spec, _state_spec(sd, state_layer, nb), _state_spec(sdc, state_layer, nb),
                  _state_spec(sp, state_layer, nb), _state_spec(ssc, state_layer, nb)]
                 + [_layer_spec(a, layer, W_Z + DN_WIDTH if name == "wqkvz" else None)
                    for name, a in zip(_MIXER_IN[5:], weights)] + alias_specs,
        out_specs=(x_out_spec,) + tuple(_state_out_spec(s.shape, layer, nb, whole) for s in out_shape[1:]),
        out_shape=out_shape,
        input_output_aliases=aliases,
        scratch_shapes=_mixer_scratch(nb, length, chunk, 2 if pipelined else 1, groups),
        compiler_params=pltpu.CompilerParams(
            dimension_semantics=("parallel", "arbitrary"), vmem_limit_bytes=VMEM_LIMIT_BYTES),
        name="mixer",
    )(*operands, *alias_ops)


def _ffn_call(x, sf, state_layer, layer, depth, w, prev, *, nb, length, final):
    batch, seq, _ = x.shape
    grid = (batch // nb, seq // length)
    x_spec = pl.BlockSpec((nb, length, D_MODEL), lambda i, j: (i, j, 0))
    weights = (w["nffn"], w["wg"], w["fw"], w["wu"], w["wd"])
    fnorm = w["fnorm"]
    out_shape = (jax.ShapeDtypeStruct(x.shape, F32),
                 jax.ShapeDtypeStruct((depth, batch) + sf.shape[2:], F32))
    operands = (x, sf) + weights + (fnorm,)
    alias_ops, alias_specs, aliases = _alias_args(prev, len(operands), 1)
    whole = prev is None
    kern = functools.partial(_ffn_kernel, nb=nb, length=length, final=final, layer=layer, whole=whole)
    return pl.pallas_call(
        kern,
        grid=grid,
        in_specs=[x_spec, _state_spec(sf, state_layer, nb)] + [_layer_spec(a, layer) for a in weights]
                 + [pl.BlockSpec(fnorm.shape, lambda i, j: (0, 0))] + alias_specs,
        out_specs=(x_spec, _state_out_spec(out_shape[1].shape, layer, nb, whole)),
        out_shape=out_shape,
        input_output_aliases=aliases,
        scratch_shapes=[pltpu.VMEM((nb, CONV_HIST + length, D_FF), F32)],
        compiler_params=pltpu.CompilerParams(
            dimension_semantics=("parallel", "arbitrary"), vmem_limit_bytes=VMEM_LIMIT_BYTES),
        name="ffn",
    )(*operands, *alias_ops)


def _prepare_weights(norm_mix, w_in, dn_conv_w, dn_a_log, dn_dt_bias, dn_out_norm, pool_w, pool_scale,
                     sconv_w, w_out, norm_ffn, w_ffn_gate, ffn_conv_w, w_ffn_up, w_ffn_down, final_norm):
    depth = w_in.shape[0]
    assert N_GATE == SUBLANES
    g0 = W_Z + DN_WIDTH
    rest0 = g0 + N_GATE
    w16 = w_in.astype(BF16)
    wgt = jnp.swapaxes(w16[:, :, g0:g0 + N_GATE], 1, 2)
    pad = jnp.zeros((depth, SUBLANES - DN_HEADS, 1), F32)
    eye = jnp.eye(len(POOL_WINDOWS), dtype=F32)
    pw = (pool_w[:, :, :, None, :] * eye[None, :, None, :, None]).reshape(depth, POOL_WIDTH, POOL_WIDTH)
    return {
        "nmix": norm_mix[:, None, :],
        "wgt": wgt,
        "wqkvz": w16,
        "wrest": w16[:, :, rest0:rest0 + REST_WIDTH],
        "dnw": dn_conv_w,
        "alog": jnp.concatenate([dn_a_log[:, :, None], pad], axis=1),
        "dtb": jnp.concatenate([dn_dt_bias[:, :, None], pad], axis=1),
        "onorm": dn_out_norm[:, None, :],
        "pw": pw.astype(BF16),
        "pscale": pool_scale[:, None, :],
        "scw": sconv_w,
        "wout": w_out.astype(BF16),
        "nffn": norm_ffn[:, None, :],
        "wg": w_ffn_gate.astype(BF16),
        "fw": ffn_conv_w,
        "wu": w_ffn_up.astype(BF16),
        "wd": w_ffn_down.astype(BF16),
        "fnorm": final_norm[None, :],
    }


def _trunk(x, states, state_layers, weights, depth, *, nb_mix, groups, nb_ffn, length, ffn_length, chunk,
           pos0):
    sd, sdc, sp, ssc, sf = states
    mix_out, ffn_out = None, None
    for l in range(depth):
        sl = state_layers[l]
        x, *mix_out = _mixer_call(x, (sd, sdc, sp, ssc), sl, l, depth, weights, mix_out,
                                  nb=nb_mix, length=length, chunk=chunk, pos0=pos0, groups=groups)
        x, *ffn_out = _ffn_call(x, sf, sl, l, depth, weights, ffn_out,
                                nb=nb_ffn, length=ffn_length, final=(l == depth - 1))
    return x, tuple(mix_out) + tuple(ffn_out)


PROMPT_TILE = 256
PROMPT_MIX_ROWS = 2
PROMPT_FFN_TILE = 512
SAMPLE_MIX_ROWS = 16
SAMPLE_FFN_ROWS = 64


def kernel(x_prompt, x_sample, state_delta, state_delta_conv, state_pool, state_sconv, state_ffn_conv,
           norm_mix, w_in, dn_conv_w, dn_a_log, dn_dt_bias, dn_out_norm, pool_w, pool_scale,
           sconv_w, w_out, norm_ffn, w_ffn_gate, ffn_conv_w, w_ffn_up, w_ffn_down, final_norm):
    depth = w_in.shape[0]
    weights = _prepare_weights(norm_mix, w_in, dn_conv_w, dn_a_log, dn_dt_bias, dn_out_norm, pool_w,
                               pool_scale, sconv_w, w_out, norm_ffn, w_ffn_gate, ffn_conv_w, w_ffn_up,
                               w_ffn_down, final_norm)

    batch, seq, _ = x_prompt.shape
    dec_batch, dec_seq, _ = x_sample.shape
    sample_states = (state_delta, state_delta_conv, state_pool, state_sconv, state_ffn_conv)
    zero_states = tuple(jnp.zeros((1, batch) + s.shape[2:], F32) for s in sample_states)

    y_prompt, p_st = _trunk(x_prompt, zero_states, (0,) * depth, weights, depth,
                            nb_mix=PROMPT_MIX_ROWS, groups=PROMPT_MIX_ROWS, nb_ffn=1,
                            length=min(PROMPT_TILE, seq), ffn_length=min(PROMPT_FFN_TILE, seq),
                            chunk=min(PROMPT_TILE, seq), pos0=0)
    y_sample, s_st = _trunk(x_sample, sample_states, tuple(range(depth)), weights, depth,
                            nb_mix=SAMPLE_MIX_ROWS, groups=1, nb_ffn=SAMPLE_FFN_ROWS,
                            length=dec_seq, ffn_length=dec_seq,
                            chunk=min(DN_CHUNK, dec_seq), pos0=PAST_LEN)
    return (y_prompt, y_sample) + p_st + s_st
```

```python
import functools
import math
import types

import jax
import jax.numpy as jnp
from jax import lax
from jax.experimental import pallas as pl
from jax.experimental.pallas import tpu as pltpu

F32 = jnp.float32
BF16 = jnp.bfloat16

LANES = 128
SUBLANES = 8
BF16_ROWS = 16
VMEM_LIMIT_BYTES = 56 * 1024 * 1024

D_MODEL = 1024
DN_HEADS = 4
DN_HEAD_DIM = 128
DN_WIDTH = DN_HEADS * DN_HEAD_DIM
DN_CONV = 4
DN_CHUNK = 64
POOL_WINDOWS = (2, 4, 8, 16)
POOL_WIDTH = 256
POOL_GROUP_DIM = POOL_WIDTH // len(POOL_WINDOWS)
POOL_BUF = max(POOL_WINDOWS) - 1
SC_WIDTH = 256
SC_CONV = 3
D_FF = 2816
FFN_CONV = 3
EPS = 1e-6
PAST_LEN = 16384

N_GATE = 2 * DN_HEADS
W_Z = 3 * DN_WIDTH
REST_WIDTH = POOL_WIDTH + 3 * SC_WIDTH
D_REST = POOL_WIDTH + SC_WIDTH

CONV_HIST = SUBLANES
POOL_HIST = 24

L_GC, L_BETA, L_EGC, L_KDF, L_EGL = 0, DN_HEADS, SUBLANES, 2 * SUBLANES, 3 * SUBLANES


def _silu(t):
    return t / (1.0 + jnp.exp(-t))


def _softplus(t):
    return jnp.maximum(t, 0.0) + jnp.log1p(jnp.exp(-jnp.abs(t)))


def _rms_scale(t):
    return t * lax.rsqrt(jnp.mean(t * t, axis=-1, keepdims=True) + EPS)


def _dot(a, b):
    return jnp.dot(a, b, preferred_element_type=F32)


def _dot_nt(a, b):
    return lax.dot_general(a, b, (((1,), (1,)), ((), ())), preferred_element_type=F32)


def _head_slice(hh):
    return slice(hh * DN_HEAD_DIM, (hh + 1) * DN_HEAD_DIM)


B_AHEAD = (4.0, 1.0)


def _interleave(*stages, speeds=None):
    speeds = speeds or [1.0] * len(stages)
    clock = [0.0] * len(stages)
    live = list(range(len(stages)))
    while live:
        i = min(live, key=lambda s: clock[s])
        try:
            clock[i] += next(stages[i]) / speeds[i]
        except StopIteration:
            live.remove(i)


def _carry_history(j, length, items, first_tile=None):
    @pl.when(j == 0)
    def _():
        if first_tile is not None:
            first_tile()
        for ext, state_ref, hist, nbuf in items:
            ext[:, 0:hist - nbuf, :] = jnp.zeros((ext.shape[0], hist - nbuf, ext.shape[2]), F32)
            ext[:, hist - nbuf:hist, :] = state_ref[...]

    @pl.when(j != 0)
    def _():
        for ext, state_ref, hist, nbuf in items:
            ext[:, hist - nbuf:hist, :] = ext[:, hist + length - nbuf:hist + length, :]


def _causal_conv(ext, w_ref, cs, length, bs=slice(None)):
    taps = w_ref.shape[0]
    tiles = length // SUBLANES
    e4 = ext[bs, 0:CONV_HIST + length, cs]
    nb, _, width = e4.shape
    e4 = e4.reshape(nb, tiles + 1, SUBLANES, width)
    sub4 = lax.broadcasted_iota(jnp.int32, (nb, tiles, SUBLANES, width), 2)
    acc = None
    for i in range(taps):
        s = taps - 1 - i
        if s:
            rot = pltpu.roll(e4, s, axis=2)
            term = jnp.where(sub4 >= s, rot[:, 1:], rot[:, :-1])
        else:
            term = e4[:, 1:]
        term = term * w_ref[i:i + 1, cs][None, None]
        acc = term if acc is None else acc + term
    return acc.reshape(nb, length, width)


def _unit_lower_inverses(na16, log2c, out):
    heads = range(len(na16))
    n = na16[0].shape[0]

    def spread(fn):
        vals = []
        for i in heads:
            vals.append(fn(i))
            if (i + 1) % DN_HEADS == 0:
                yield 0.3
        return vals

    row = lax.broadcasted_iota(jnp.int32, (n, n), 0)
    col = lax.broadcasted_iota(jnp.int32, (n, n), 1)
    pair = ((row >> 1) == (col >> 1)) & ((row & 1) == 1) & ((col & 1) == 0)
    eye = jnp.where(row == col, 1.0, 0.0).astype(F32)
    t16 = yield from spread(lambda i: jnp.where(pair, na16[i].astype(F32), eye).astype(BF16))
    for lvl in range(1, log2c):
        blk = 1 << lvl
        if blk < BF16_ROWS:
            off = (((row >> (lvl + 1)) == (col >> (lvl + 1)))
                   & (((row >> lvl) & 1) == 1) & (((col >> lvl) & 1) == 0))
            p16 = yield from spread(lambda i: _dot(na16[i], t16[i]).astype(BF16))
            t16 = yield from spread(
                lambda i: t16[i] + jnp.where(off, _dot(t16[i], p16[i]), 0.0).astype(BF16))
        else:
            pairs = n // (2 * blk)
            rowh = lax.broadcasted_iota(jnp.int32, (n // 2, n), 0)
            colh = lax.broadcasted_iota(jnp.int32, (n // 2, n), 1)
            offh = ((colh >> (lvl + 1)) == (rowh >> lvl)) & (((colh >> lvl) & 1) == 0)
            t4 = [t16[i].reshape(pairs, 2, blk, n) for i in heads]
            na_odd = [na16[i].reshape(pairs, 2, blk, n)[:, 1].reshape(n // 2, n) for i in heads]
            t_odd = [t4[i][:, 1].reshape(n // 2, n) for i in heads]
            p16 = yield from spread(
                lambda i: jnp.broadcast_to(_dot(na_odd[i], t16[i]).reshape(pairs, 1, blk, n),
                                           (pairs, 2, blk, n)).reshape(n, n).astype(BF16))
            x_odd = yield from spread(
                lambda i: jnp.where(offh, _dot(t_odd[i], p16[i]), 0.0).astype(BF16))
            t16 = [jnp.concatenate([t4[i][:, 0:1], (t_odd[i] + x_odd[i]).reshape(pairs, 1, blk, n)],
                                   axis=1).reshape(n, n) for i in heads]
    out.extend(t16)


def _stage_a(j, r, st, *, nb, length, chunk, pos0, groups):
    rows = nb * length
    grows, gnb = rows // groups, nb // groups
    log2c = int(math.log2(chunk))

    x = r.x[...].reshape(rows, D_MODEL)
    h16 = (_rms_scale(x) * r.nmix[...]).astype(BF16)

    gates_t = _dot_nt(r.wgt[...], h16)
    g_t = -jnp.exp(r.alog[...]) * _softplus(gates_t + r.dtb[...])
    beta_t = 1.0 / (1.0 + jnp.exp(-gates_t))
    pos_in_chunk = lax.broadcasted_iota(jnp.int32, (SUBLANES, rows), 1) & (chunk - 1)
    gc_t = g_t
    sfx_t = g_t
    step = 1
    while step < chunk:
        gc_t = gc_t + jnp.where(pos_in_chunk >= step, pltpu.roll(gc_t, step, axis=1), 0.0)
        sfx_t = sfx_t + jnp.where(pos_in_chunk + step < chunk,
                                  pltpu.roll(sfx_t, rows - step, axis=1), 0.0)
        step *= 2
    rest_t = sfx_t - g_t
    sub = lax.broadcasted_iota(jnp.int32, (SUBLANES, rows), 0)
    cols = jnp.concatenate(
        [jnp.where(sub < DN_HEADS, gc_t, -beta_t), jnp.exp(gc_t), jnp.exp(rest_t), jnp.exp(gc_t + rest_t),
         jnp.zeros((LANES - 4 * SUBLANES, rows), F32)], axis=0).T

    def colb(lane, g, width=LANES):
        return jnp.broadcast_to(cols[g * grows:(g + 1) * grows, lane:lane + 1], (grows, width))

    yield 1.0
    for part in range(3):
        cs = slice(part * DN_WIDTH, (part + 1) * DN_WIDTH)
        r.extd[:, CONV_HIST:CONV_HIST + length, cs] = _dot(h16, r.wqkvz[:, cs]).reshape(nb, length, DN_WIDTH)
        yield 0.3
    r.odc[...] = r.extd[:, CONV_HIST + length - (DN_CONV - 1):CONV_HIST + length, :]

    def conv_block(ci, g):
        cs = slice(ci * LANES, (ci + 1) * LANES)
        bs = slice(g * gnb, (g + 1) * gnb)
        return _silu(_causal_conv(r.extd, r.dnw, cs, length, bs)).reshape(grows, LANES)

    row = lax.broadcasted_iota(jnp.int32, (grows, grows), 0)
    col = lax.broadcasted_iota(jnp.int32, (grows, grows), 1)
    same_chunk = (row >> log2c) == (col >> log2c)
    m_incl = same_chunk & (row >= col)
    m_strict = same_chunk & (row > col)

    for c in range(groups * DN_HEADS):
        g, hh = divmod(c, DN_HEADS)
        nbeta_b = colb(L_BETA + hh, g)
        egc_b = colb(L_EGC + hh, g)
        qc = conv_block(hh, g)
        yield 0.2
        kc = conv_block(DN_HEADS + hh, g)
        yield 0.2
        vc = conv_block(2 * DN_HEADS + hh, g)
        q = qc * (lax.rsqrt(jnp.sum(qc * qc, axis=-1, keepdims=True) + EPS) * (DN_HEAD_DIM ** -0.5))
        k = kc * lax.rsqrt(jnp.sum(kc * kc, axis=-1, keepdims=True) + EPS)
        nkb = k * nbeta_b
        k16 = k.astype(BF16)
        yield 0.2
        diff = colb(L_GC + hh, g, grows) - gc_t[hh:hh + 1, g * grows:(g + 1) * grows]
        decay = jnp.where(m_incl, jnp.exp(diff), 0.0)
        na = jnp.where(m_strict, _dot_nt(nkb.astype(BF16), k16) * decay, 0.0)
        st.na[c] = na.astype(BF16)
        yield 0.25
        st.qk[c] = (_dot_nt(q.astype(BF16), k16) * decay).astype(BF16)
        st.rhs[c] = jnp.concatenate([vc * nbeta_b, nkb * egc_b], axis=1).astype(BF16)
        st.qg[c] = q * egc_b
        st.kdt[c] = (k * colb(L_KDF + hh, g)).T
        st.egl[c] = colb(L_EGL + hh, g)[0:st.egl.shape[1]]
        yield 0.25

    prest = _dot(h16, r.wrest[...])
    st.z[...] = _dot(h16, r.wqkvz[:, W_Z:W_Z + DN_WIDTH])
    p_in = prest[:, 0:POOL_WIDTH]
    sc_x = prest[:, POOL_WIDTH:POOL_WIDTH + SC_WIDTH]
    sc_b = prest[:, POOL_WIDTH + SC_WIDTH:POOL_WIDTH + 2 * SC_WIDTH]
    sc_c = prest[:, POOL_WIDTH + 2 * SC_WIDTH:]
    yield 0.5

    ph = POOL_HIST
    r.extp[:, ph:ph + length, :] = p_in.reshape(nb, length, POOL_WIDTH)
    r.op[...] = r.extp[:, ph + length - POOL_BUF:ph + length, :]
    zero8 = jnp.zeros((nb, SUBLANES, POOL_WIDTH), F32)
    end = ph + length
    r.s2[:, 0:8, :] = zero8
    r.s4[:, 0:8, :] = zero8
    r.s8[:, 0:8, :] = zero8
    r.s2[:, 8:end, :] = r.extp[:, 8:end, :] + r.extp[:, 7:end - 1, :]
    r.s4[:, 8:end, :] = r.s2[:, 8:end, :] + r.s2[:, 6:end - 2, :]
    r.s8[:, 8:end, :] = r.s4[:, 8:end, :] + r.s4[:, 4:end - 4, :]
    s16 = r.s8[:, ph:end, :] + r.s8[:, ph - 8:end - 8, :]
    lane3 = lax.broadcasted_iota(jnp.int32, (nb, length, POOL_WIDTH), 2)
    grp = lane3 // POOL_GROUP_DIM
    wsum = jnp.where(grp == 0, r.s2[:, ph:end, :],
                     jnp.where(grp == 1, r.s4[:, ph:end, :],
                               jnp.where(grp == 2, r.s8[:, ph:end, :], s16)))
    win = jnp.where(grp == 0, POOL_WINDOWS[0],
                    jnp.where(grp == 1, POOL_WINDOWS[1],
                              jnp.where(grp == 2, POOL_WINDOWS[2], POOL_WINDOWS[3])))
    tpos = lax.broadcasted_iota(jnp.int32, (nb, length, POOL_WIDTH), 1) + (pos0 + j * length)
    cnt = jnp.minimum(tpos + 1, win).astype(F32)
    dpool = (wsum / cnt - r.extp[:, ph:end, :]).reshape(rows, POOL_WIDTH)
    r.cat_a[:, 0:POOL_WIDTH] = (_dot(dpool.astype(BF16), r.pw[...]) * r.pscale[...]).astype(BF16)
    yield 0.4

    r.exts[:, CONV_HIST:CONV_HIST + length, :] = (sc_c * sc_x).reshape(nb, length, SC_WIDTH)
    r.osc[...] = r.exts[:, CONV_HIST + length - (SC_CONV - 1):CONV_HIST + length, :]
    cconv = _causal_conv(r.exts, r.scw, slice(0, SC_WIDTH), length)
    r.cat_a[:, POOL_WIDTH:] = (sc_b * cconv.reshape(rows, SC_WIDTH)).astype(BF16)

    st.xy[...] = x + _dot(r.cat_a[...], r.wout[DN_WIDTH:, :])


def _stage_b(r, st, *, nb, length, chunk, groups):
    rows = nb * length
    grows = rows // groups
    chains = groups * DN_HEADS
    log2c = int(math.log2(chunk))

    tinv = []
    yield from _unit_lower_inverses([st.na[c] for c in range(chains)], log2c, tinv)
    for c in range(chains):
        nsol = _dot(tinv[c], st.rhs[c])
        r.nu[c] = nsol[:, :DN_HEAD_DIM]
        r.nw[c] = nsol[:, DN_HEAD_DIM:]
        if (c + 1) % DN_HEADS == 0:
            yield 0.5

    if grows == chunk:
        assert groups == nb
        assert DN_HEADS % 2 == 0
        zero = jnp.zeros((DN_HEAD_DIM, DN_HEAD_DIM), BF16)
        for c0 in range(0, chains, 2):
            g, h0 = divmod(c0, DN_HEADS)
            pair = ((c0, h0, slice(0, DN_HEAD_DIM)), (c0 + 1, h0 + 1, slice(DN_HEAD_DIM, 2 * DN_HEAD_DIM)))
            s_pair = [r.od[g, hh] for _, hh, _ in pair]
            s2 = jnp.concatenate([jnp.concatenate([s_pair[0].astype(BF16), zero], axis=1),
                                  jnp.concatenate([zero, s_pair[1].astype(BF16)], axis=1)], axis=0)
            nwq = jnp.concatenate([jnp.concatenate([r.nw[c] for c, _, _ in pair], axis=1),
                                   jnp.concatenate([st.qg[c] for c, _, _ in pair], axis=1)],
                                  axis=0).astype(BF16)
            res = _dot(nwq, s2)
            for (c, hh, lanes), s in zip(pair, s_pair):
                vn16 = (res[:grows, lanes] - r.nu[c]).astype(BF16)
                r.o[g * grows:(g + 1) * grows, _head_slice(hh)] = res[grows:, lanes] + _dot(st.qk[c], vn16)
                r.od[g, hh] = s * st.egl[c, 0:1, :] + _dot(st.kdt[c].astype(BF16), vn16)
            yield 0.6
    else:
        assert groups == 1 and rows // chunk == nb
        seg = lax.broadcasted_iota(jnp.int32, (nb, 1, rows), 2) >> log2c
        own = seg == lax.broadcasted_iota(jnp.int32, (nb, 1, rows), 0)
        for hh in range(DN_HEADS):
            vn_parts, qs_parts = [], []
            for b in range(nb):
                rs = slice(b * chunk, (b + 1) * chunk)
                nwq = jnp.concatenate([r.nw[hh, rs, :], st.qg[hh, rs, :]], axis=0).astype(BF16)
                res = _dot(nwq, r.od[b, hh].astype(BF16))
                vn_parts.append(res[:chunk] - r.nu[hh, rs, :])
                qs_parts.append(res[chunk:])
            vn16 = jnp.concatenate(vn_parts, axis=0).astype(BF16)
            r.o[:, _head_slice(hh)] = jnp.concatenate(qs_parts, axis=0) + _dot(st.qk[hh], vn16)
            kd_own = jnp.where(own, st.kdt[hh][None], 0.0).astype(BF16).reshape(nb * DN_HEAD_DIM, rows)
            upd = _dot(kd_own, vn16).reshape(nb, DN_HEAD_DIM, DN_HEAD_DIM)
            egl = st.egl[hh].reshape(nb, chunk, DN_HEAD_DIM)[:, 0:1, :]
            r.od[:, hh, :, :] = r.od[:, hh, :, :] * egl + upd
            yield 0.3

    for hh in range(DN_HEADS):
        hs = _head_slice(hh)
        r.cat_b[:, hs] = (_rms_scale(r.o[:, hs]) * r.onorm[...] * _silu(st.z[:, hs])).astype(BF16)
    y = st.xy[...] + _dot(r.cat_b[...], r.wout[:DN_WIDTH, :])
    r.xo[...] = y.reshape(nb, length, D_MODEL)


_MIXER_IN = ("x", "sd", "sdc", "sp", "ssc", "nmix", "wgt", "wqkvz", "wrest", "dnw", "alog", "dtb", "onorm", "pw",
             "pscale", "scw", "wout")
_MIXER_OUT = ("xo", "od", "odc", "op", "osc")
_MIXER_PRIVATE = ("extd", "extp", "s2", "s4", "s8", "exts", "cat_a", "nu", "nw", "o", "cat_b")
_MIXER_SET = ("na", "qk", "rhs", "qg", "kdt", "egl", "z", "xy")


def _mixer_scratch(nb, length, chunk, n_sets, groups):
    rows = nb * length
    grows = rows // groups
    chains = groups * DN_HEADS
    one_chunk = grows == chunk
    private = [
        pltpu.VMEM((nb, CONV_HIST + length, 3 * DN_WIDTH), F32),
        pltpu.VMEM((nb, POOL_HIST + length, POOL_WIDTH), F32),
        pltpu.VMEM((nb, POOL_HIST + length, POOL_WIDTH), F32),
        pltpu.VMEM((nb, POOL_HIST + length, POOL_WIDTH), F32),
        pltpu.VMEM((nb, POOL_HIST + length, POOL_WIDTH), F32),
        pltpu.VMEM((nb, CONV_HIST + length, SC_WIDTH), F32),
        pltpu.VMEM((rows, D_REST), BF16),
        pltpu.VMEM((chains, grows, DN_HEAD_DIM), F32),
        pltpu.VMEM((chains, grows, DN_HEAD_DIM), F32),
        pltpu.VMEM((rows, DN_WIDTH), F32),
        pltpu.VMEM((rows, DN_WIDTH), BF16),
    ]
    one_set = [
        pltpu.VMEM((chains, grows, grows), BF16),
        pltpu.VMEM((chains, grows, grows), BF16),
        pltpu.VMEM((chains, grows, 2 * DN_HEAD_DIM), BF16),
        pltpu.VMEM((chains, grows, DN_HEAD_DIM), F32),
        pltpu.VMEM((chains, DN_HEAD_DIM, grows), F32),
        pltpu.VMEM((chains, SUBLANES if one_chunk else grows, DN_HEAD_DIM), F32),
        pltpu.VMEM((rows, DN_WIDTH), F32),
        pltpu.VMEM((rows, D_MODEL), F32),
    ]
    return private + one_set * n_sets


def _zero_other_layers(ref, layer):
    for l in range(ref.shape[0]):
        if l != layer:
            ref[l] = jnp.zeros(ref.shape[1:], F32)


def _mixer_kernel(*refs, nb, length, chunk, pos0, n_tiles, n_alias, layer, whole, groups):
    n_in, n_set = len(_MIXER_IN), len(_MIXER_SET)
    refs = refs[:n_in] + refs[n_in + n_alias:]
    names = _MIXER_IN + _MIXER_OUT + _MIXER_PRIVATE
    r = types.SimpleNamespace(**dict(zip(names, refs[:len(names)])))
    set_refs = refs[len(names):]
    sets = [types.SimpleNamespace(**dict(zip(_MIXER_SET, set_refs[i * n_set:(i + 1) * n_set])))
            for i in range(len(set_refs) // n_set)]
    j = pl.program_id(1)
    whole_refs = (r.od, r.odc, r.op, r.osc)
    if whole:
        r.od, r.odc, r.op, r.osc = (f.at[layer] for f in whole_refs)

    def init_state():
        if whole:
            for f in whole_refs:
                _zero_other_layers(f, layer)
        r.od[...] = r.sd[...]

    _carry_history(j, length,
                   [(r.extd, r.sdc, CONV_HIST, DN_CONV - 1), (r.extp, r.sp, POOL_HIST, POOL_BUF),
                    (r.exts, r.ssc, CONV_HIST, SC_CONV - 1)], first_tile=init_state)

    stage_a = functools.partial(_stage_a, j, r, nb=nb, length=length, chunk=chunk, pos0=pos0, groups=groups)
    stage_b = functools.partial(_stage_b, r, nb=nb, length=length, chunk=chunk, groups=groups)

    if len(sets) == 1:
        _interleave(stage_a(sets[0]))
        _interleave(stage_b(sets[0]))
        return

    middle = (j > 0) & (j < n_tiles)

    @pl.when(j == 0)
    def _():
        _interleave(stage_a(sets[0]))

    @pl.when(middle & (j % 2 == 1))
    def _():
        _interleave(stage_b(sets[0]), stage_a(sets[1]), speeds=B_AHEAD)

    @pl.when(middle & (j % 2 == 0))
    def _():
        _interleave(stage_b(sets[1]), stage_a(sets[0]), speeds=B_AHEAD)

    @pl.when(j == n_tiles)
    def _():
        _interleave(stage_b(sets[(n_tiles - 1) % 2]))


def _ffn_kernel(x_ref, sf_ref, nffn_ref, wg_ref, fw_ref, wu_ref, wd_ref, fnorm_ref, *rest,
                nb, length, final, layer, whole):
    xo_ref, of_full, extf_s = rest[-3:]
    of_ref = of_full.at[layer] if whole else of_full
    j = pl.program_id(1)
    rows = nb * length
    _carry_history(j, length, [(extf_s, sf_ref, CONV_HIST, FFN_CONV - 1)],
                   first_tile=functools.partial(_zero_other_layers, of_full, layer) if whole else None)
    x = x_ref[...].reshape(rows, D_MODEL)
    h16 = (_rms_scale(x) * nffn_ref[...]).astype(BF16)
    extf_s[:, CONV_HIST:CONV_HIST + length, :] = _dot(h16, wg_ref[...]).reshape(nb, length, D_FF)
    of_ref[...] = extf_s[:, CONV_HIST + length - (FFN_CONV - 1):CONV_HIST + length, :]
    gate = _causal_conv(extf_s, fw_ref, slice(0, D_FF), length)
    up = _dot(h16, wu_ref[...])
    act = (_silu(gate).reshape(rows, D_FF) * up).astype(BF16)
    y = x + _dot(act, wd_ref[...])
    if final:
        y = _rms_scale(y) * fnorm_ref[...]
    xo_ref[...] = y.reshape(nb, length, D_MODEL)


def _layer_spec(arr, layer, cols=None):
    tail = arr.shape[1:] if cols is None else arr.shape[1:-1] + (cols,)
    zeros = (0,) * len(tail)
    return pl.BlockSpec((None,) + tail, lambda i, j: (layer,) + zeros, pipeline_mode=pl.Buffered(1))


def _state_spec(arr, layer, nb):
    tail = arr.shape[2:]
    zeros = (0,) * len(tail)
    return pl.BlockSpec((None, nb) + tail, lambda i, j: (layer, i) + zeros)


def _state_out_spec(shape, layer, nb, whole):
    tail = shape[2:]
    zeros = (0,) * len(tail)
    if whole:
        return pl.BlockSpec((shape[0], nb) + tail, lambda i, j: (0, i) + zeros)
    return pl.BlockSpec((None, nb) + tail, lambda i, j: (layer, i) + zeros)


def _alias_args(prev, n_in, first_out):
    if prev is None:
        return [], [], {}
    specs = [pl.BlockSpec(memory_space=pl.ANY)] * len(prev)
    return list(prev), specs, {n_in + k: first_out + k for k in range(len(prev))}


def _mixer_call(x, states, state_layer, layer, depth, w, prev, *, nb, length, chunk, pos0, groups):
    batch, seq, _ = x.shape
    sd, sdc, sp, ssc = states
    n_tiles = seq // length
    pipelined = n_tiles > 1
    if pipelined:
        last = n_tiles - 1
        grid = (batch // nb, n_tiles + 1)
        x_in_spec = pl.BlockSpec((nb, length, D_MODEL), lambda i, j: (i, jnp.minimum(j, last), 0))
        x_out_spec = pl.BlockSpec((nb, length, D_MODEL), lambda i, j: (i, jnp.maximum(j - 1, 0), 0))
    else:
        grid = (batch // nb, n_tiles)
        x_in_spec = x_out_spec = pl.BlockSpec((nb, length, D_MODEL), lambda i, j: (i, j, 0))
    weights = tuple(w[name] for name in _MIXER_IN[5:])
    out_shape = (
        jax.ShapeDtypeStruct(x.shape, F32),
        jax.ShapeDtypeStruct((depth, batch) + sd.shape[2:], F32),
        jax.ShapeDtypeStruct((depth, batch) + sdc.shape[2:], F32),
        jax.ShapeDtypeStruct((depth, batch) + sp.shape[2:], F32),
        jax.ShapeDtypeStruct((depth, batch) + ssc.shape[2:], F32),
    )
    operands = (x, sd, sdc, sp, ssc) + weights
    alias_ops, alias_specs, aliases = _alias_args(prev, len(operands), 1)
    whole = prev is None
    kern = functools.partial(_mixer_kernel, nb=nb, length=length, chunk=chunk, pos0=pos0, n_tiles=n_tiles,
                             n_alias=len(alias_ops), layer=layer, whole=whole, groups=groups)
    return pl.pallas_call(
        kern,
        grid=grid,
        in_specs=[x_in_spec, _state_spec(sd, state_layer, nb), _state_spec(sdc, state_layer, nb),
                  _state_spec(sp, state_layer, nb), _state_spec(ssc, state_layer, nb)]
                 + [_layer_spec(a, layer, W_Z + DN_WIDTH if name == "wqkvz" else None)
                    for name, a in zip(_MIXER_IN[5:], weights)] + alias_specs,
        out_specs=(x_out_spec,) + tuple(_state_out_spec(s.shape, layer, nb, whole) for s in out_shape[1:]),
        out_shape=out_shape,
        input_output_aliases=aliases,
        scratch_shapes=_mixer_scratch(nb, length, chunk, 2 if pipelined else 1, groups),
        compiler_params=pltpu.CompilerParams(
            dimension_semantics=("parallel", "arbitrary"), vmem_limit_bytes=VMEM_LIMIT_BYTES),
        name="mixer",
    )(*operands, *alias_ops)


def _ffn_call(x, sf, state_layer, layer, depth, w, prev, *, nb, length, final):
    batch, seq, _ = x.shape
    grid = (batch // nb, seq // length)
    x_spec = pl.BlockSpec((nb, length, D_MODEL), lambda i, j: (i, j, 0))
    weights = (w["nffn"], w["wg"], w["fw"], w["wu"], w["wd"])
    fnorm = w["fnorm"]
    out_shape = (jax.ShapeDtypeStruct(x.shape, F32),
                 jax.ShapeDtypeStruct((depth, batch) + sf.shape[2:], F32))
    operands = (x, sf) + weights + (fnorm,)
    alias_ops, alias_specs, aliases = _alias_args(prev, len(operands), 1)
    whole = prev is None
    kern = functools.partial(_ffn_kernel, nb=nb, length=length, final=final, layer=layer, whole=whole)
    return pl.pallas_call(
        kern,
        grid=grid,
        in_specs=[x_spec, _state_spec(sf, state_layer, nb)] + [_layer_spec(a, layer) for a in weights]
                 + [pl.BlockSpec(fnorm.shape, lambda i, j: (0, 0))] + alias_specs,
        out_specs=(x_spec, _state_out_spec(out_shape[1].shape, layer, nb, whole)),
        out_shape=out_shape,
        input_output_aliases=aliases,
        scratch_shapes=[pltpu.VMEM((nb, CONV_HIST + length, D_FF), F32)],
        compiler_params=pltpu.CompilerParams(
            dimension_semantics=("parallel", "arbitrary"), vmem_limit_bytes=VMEM_LIMIT_BYTES),
        name="ffn",
    )(*operands, *alias_ops)


def _prepare_weights(norm_mix, w_in, dn_conv_w, dn_a_log, dn_dt_bias, dn_out_norm, pool_w, pool_scale,
                     sconv_w, w_out, norm_ffn, w_ffn_gate, ffn_conv_w, w_ffn_up, w_ffn_down, final_norm):
    depth = w_in.shape[0]
    assert N_GATE == SUBLANES
    g0 = W_Z + DN_WIDTH
    rest0 = g0 + N_GATE
    w16 = w_in.astype(BF16)
    wgt = jnp.swapaxes(w16[:, :, g0:g0 + N_GATE], 1, 2)
    pad = jnp.zeros((depth, SUBLANES - DN_HEADS, 1), F32)
    eye = jnp.eye(len(POOL_WINDOWS), dtype=F32)
    pw = (pool_w[:, :, :, None, :] * eye[None, :, None, :, None]).reshape(depth, POOL_WIDTH, POOL_WIDTH)
    return {
        "nmix": norm_mix[:, None, :],
        "wgt": wgt,
        "wqkvz": w16,
        "wrest": w16[:, :, rest0:rest0 + REST_WIDTH],
        "dnw": dn_conv_w,
        "alog": jnp.concatenate([dn_a_log[:, :, None], pad], axis=1),
        "dtb": jnp.concatenate([dn_dt_bias[:, :, None], pad], axis=1),
        "onorm": dn_out_norm[:, None, :],
        "pw": pw.astype(BF16),
        "pscale": pool_scale[:, None, :],
        "scw": sconv_w,
        "wout": w_out.astype(BF16),
        "nffn": norm_ffn[:, None, :],
        "wg": w_ffn_gate.astype(BF16),
        "fw": ffn_conv_w,
        "wu": w_ffn_up.astype(BF16),
        "wd": w_ffn_down.astype(BF16),
        "fnorm": final_norm[None, :],
    }


def _trunk(x, states, state_layers, weights, depth, *, nb_mix, groups, nb_ffn, length, ffn_length, chunk,
           pos0):
    sd, sdc, sp, ssc, sf = states
    mix_out, ffn_out = None, None
    for l in range(depth):
        sl = state_layers[l]
        x, *mix_out = _mixer_call(x, (sd, sdc, sp, ssc), sl, l, depth, weights, mix_out,
                                  nb=nb_mix, length=length, chunk=chunk, pos0=pos0, groups=groups)
        x, *ffn_out = _ffn_call(x, sf, sl, l, depth, weights, ffn_out,
                                nb=nb_ffn, length=ffn_length, final=(l == depth - 1))
    return x, tuple(mix_out) + tuple(ffn_out)


PROMPT_TILE = 256
PROMPT_MIX_ROWS = 2
PROMPT_FFN_TILE = 512
SAMPLE_MIX_ROWS = 16
SAMPLE_FFN_ROWS = 64


def kernel(x_prompt, x_sample, state_delta, state_delta_conv, state_pool, state_sconv, state_ffn_conv,
           norm_mix, w_in, dn_conv_w, dn_a_log, dn_dt_bias, dn_out_norm, pool_w, pool_scale,
           sconv_w, w_out, norm_ffn, w_ffn_gate, ffn_conv_w, w_ffn_up, w_ffn_down, final_norm):
    depth = w_in.shape[0]
    weights = _prepare_weights(norm_mix, w_in, dn_conv_w, dn_a_log, dn_dt_bias, dn_out_norm, pool_w,
                               pool_scale, sconv_w, w_out, norm_ffn, w_ffn_gate, ffn_conv_w, w_ffn_up,
                               w_ffn_down, final_norm)

    batch, seq, _ = x_prompt.shape
    dec_batch, dec_seq, _ = x_sample.shape
    sample_states = (state_delta, state_delta_conv, state_pool, state_sconv, state_ffn_conv)
    zero_states = tuple(jnp.zeros((1, batch) + s.shape[2:], F32) for s in sample_states)

    y_prompt, p_st = _trunk(x_prompt, zero_states, (0,) * depth, weights, depth,
                            nb_mix=PROMPT_MIX_ROWS, groups=PROMPT_MIX_ROWS, nb_ffn=1,
                            length=min(PROMPT_TILE, seq), ffn_length=min(PROMPT_FFN_TILE, seq),
                            chunk=min(PROMPT_TILE, seq), pos0=0)
    y_sample, s_st = _trunk(x_sample, sample_states, tuple(range(depth)), weights, depth,
                            nb_mix=SAMPLE_MIX_ROWS, groups=1, nb_ffn=SAMPLE_FFN_ROWS,
                            length=dec_seq, ffn_length=dec_seq,
                            chunk=min(DN_CHUNK, dec_seq), pos0=PAST_LEN)
    return (y_prompt, y_sample) + p_st + s_st
```

```python
import functools
import math
import types

import jax
import jax.numpy as jnp
from jax import lax
from jax.experimental import pallas as pl
from jax.experimental.pallas import tpu as pltpu

F32 = jnp.float32
BF16 = jnp.bfloat16

LANES = 128
SUBLANES = 8
BF16_ROWS = 16
VMEM_LIMIT_BYTES = 56 * 1024 * 1024

D_MODEL = 1024
DN_HEADS = 4
DN_HEAD_DIM = 128
DN_WIDTH = DN_HEADS * DN_HEAD_DIM
DN_CONV = 4
DN_CHUNK = 64
POOL_WINDOWS = (2, 4, 8, 16)
POOL_WIDTH = 256
POOL_GROUP_DIM = POOL_WIDTH // len(POOL_WINDOWS)
POOL_BUF = max(POOL_WINDOWS) - 1
SC_WIDTH = 256
SC_CONV = 3
D_FF = 2816
FFN_CONV = 3
EPS = 1e-6
PAST_LEN = 16384

N_GATE = 2 * DN_HEADS
W_Z = 3 * DN_WIDTH
REST_WIDTH = POOL_WIDTH + 3 * SC_WIDTH
D_REST = POOL_WIDTH + SC_WIDTH

CONV_HIST = SUBLANES
POOL_HIST = 24

L_GC, L_BETA, L_EGC, L_KDF, L_EGL = 0, DN_HEADS, SUBLANES, 2 * SUBLANES, 3 * SUBLANES


def _silu(t):
    return t / (1.0 + jnp.exp(-t))


def _softplus(t):
    return jnp.maximum(t, 0.0) + jnp.log1p(jnp.exp(-jnp.abs(t)))


def _rms_scale(t):
    return t * lax.rsqrt(jnp.mean(t * t, axis=-1, keepdims=True) + EPS)


def _dot(a, b):
    return jnp.dot(a, b, preferred_element_type=F32)


def _dot_nt(a, b):
    return lax.dot_general(a, b, (((1,), (1,)), ((), ())), preferred_element_type=F32)


def _head_slice(hh):
    return slice(hh * DN_HEAD_DIM, (hh + 1) * DN_HEAD_DIM)


B_AHEAD = (4.0, 1.0)


def _interleave(*stages, speeds=None):
    speeds = speeds or [1.0] * len(stages)
    clock = [0.0] * len(stages)
    live = list(range(len(stages)))
    while live:
        i = min(live, key=lambda s: clock[s])
        try:
            clock[i] += next(stages[i]) / speeds[i]
        except StopIteration:
            live.remove(i)


def _carry_history(j, length, items, first_tile=None):
    @pl.when(j == 0)
    def _():
        if first_tile is not None:
            first_tile()
        for ext, state_ref, hist, nbuf in items:
            ext[:, 0:hist - nbuf, :] = jnp.zeros((ext.shape[0], hist - nbuf, ext.shape[2]), F32)
            ext[:, hist - nbuf:hist, :] = state_ref[...]

    @pl.when(j != 0)
    def _():
        for ext, state_ref, hist, nbuf in items:
            ext[:, hist - nbuf:hist, :] = ext[:, hist + length - nbuf:hist + length, :]


def _causal_conv(ext, w_ref, cs, length, bs=slice(None)):
    taps = w_ref.shape[0]
    tiles = length // SUBLANES
    e4 = ext[bs, 0:CONV_HIST + length, cs]
    nb, _, width = e4.shape
    e4 = e4.reshape(nb, tiles + 1, SUBLANES, width)
    sub4 = lax.broadcasted_iota(jnp.int32, (nb, tiles, SUBLANES, width), 2)
    acc = None
    for i in range(taps):
        s = taps - 1 - i
        if s:
            rot = pltpu.roll(e4, s, axis=2)
            term = jnp.where(sub4 >= s, rot[:, 1:], rot[:, :-1])
        else:
            term = e4[:, 1:]
        term = term * w_ref[i:i + 1, cs][None, None]
        acc = term if acc is None else acc + term
    return acc.reshape(nb, length, width)


def _unit_lower_inverses(na16, log2c, out):
    heads = range(len(na16))
    n = na16[0].shape[0]

    def spread(fn):
        vals = []
        for i in heads:
            vals.append(fn(i))
            if (i + 1) % DN_HEADS == 0:
                yield 0.3
        return vals

    row = lax.broadcasted_iota(jnp.int32, (n, n), 0)
    col = lax.broadcasted_iota(jnp.int32, (n, n), 1)
    pair = ((row >> 1) == (col >> 1)) & ((row & 1) == 1) & ((col & 1) == 0)
    eye = jnp.where(row == col, 1.0, 0.0).astype(F32)
    t16 = yield from spread(lambda i: jnp.where(pair, na16[i].astype(F32), eye).astype(BF16))
    for lvl in range(1, log2c):
        blk = 1 << lvl
        if blk < BF16_ROWS:
            off = (((row >> (lvl + 1)) == (col >> (lvl + 1)))
                   & (((row >> lvl) & 1) == 1) & (((col >> lvl) & 1) == 0))
            p16 = yield from spread(lambda i: _dot(na16[i], t16[i]).astype(BF16))
            t16 = yield from spread(
                lambda i: t16[i] + jnp.where(off, _dot(t16[i], p16[i]), 0.0).astype(BF16))
        else:
            pairs = n // (2 * blk)
            rowh = lax.broadcasted_iota(jnp.int32, (n // 2, n), 0)
            colh = lax.broadcasted_iota(jnp.int32, (n // 2, n), 1)
            offh = ((colh >> (lvl + 1)) == (rowh >> lvl)) & (((colh >> lvl) & 1) == 0)
            t4 = [t16[i].reshape(pairs, 2, blk, n) for i in heads]
            na_odd = [na16[i].reshape(pairs, 2, blk, n)[:, 1].reshape(n // 2, n) for i in heads]
            t_odd = [t4[i][:, 1].reshape(n // 2, n) for i in heads]
            p16 = yield from spread(
                lambda i: jnp.broadcast_to(_dot(na_odd[i], t16[i]).reshape(pairs, 1, blk, n),
                                           (pairs, 2, blk, n)).reshape(n, n).astype(BF16))
            x_odd = yield from spread(
                lambda i: jnp.where(offh, _dot(t_odd[i], p16[i]), 0.0).astype(BF16))
            t16 = [jnp.concatenate([t4[i][:, 0:1], (t_odd[i] + x_odd[i]).reshape(pairs, 1, blk, n)],
                                   axis=1).reshape(n, n) for i in heads]
    out.extend(t16)


def _stage_a(j, r, st, *, nb, length, chunk, pos0, groups):
    rows = nb * length
    grows = rows // groups
    log2c = int(math.log2(chunk))

    x = r.x[...].reshape(rows, D_MODEL)
    h16 = (_rms_scale(x) * r.nmix[...]).astype(BF16)

    gates_t = _dot_nt(r.wgt[...], h16)
    g_t = -jnp.exp(r.alog[...]) * _softplus(gates_t + r.dtb[...])
    beta_t = 1.0 / (1.0 + jnp.exp(-gates_t))
    pos_in_chunk = lax.broadcasted_iota(jnp.int32, (SUBLANES, rows), 1) & (chunk - 1)
    gc_t = g_t
    sfx_t = g_t
    step = 1
    while step < chunk:
        gc_t = gc_t + jnp.where(pos_in_chunk >= step, pltpu.roll(gc_t, step, axis=1), 0.0)
        sfx_t = sfx_t + jnp.where(pos_in_chunk + step < chunk,
                                  pltpu.roll(sfx_t, rows - step, axis=1), 0.0)
        step *= 2
    rest_t = sfx_t - g_t
    sub = lax.broadcasted_iota(jnp.int32, (SUBLANES, rows), 0)
    cols = jnp.concatenate(
        [jnp.where(sub < DN_HEADS, gc_t, -beta_t), jnp.exp(gc_t), jnp.exp(rest_t), jnp.exp(gc_t + rest_t),
         jnp.zeros((LANES - 4 * SUBLANES, rows), F32)], axis=0).T

    def colb(lane, g, width=LANES):
        return jnp.broadcast_to(cols[g * grows:(g + 1) * grows, lane:lane + 1], (grows, width))

    yield 1.0
    for part in range(3):
        cs = slice(part * DN_WIDTH, (part + 1) * DN_WIDTH)
        r.extd[:, CONV_HIST:CONV_HIST + length, cs] = _dot(h16, r.wqkvz[:, cs]).reshape(nb, length, DN_WIDTH)
        yield 0.3
    r.odc[...] = r.extd[:, CONV_HIST + length - (DN_CONV - 1):CONV_HIST + length, :]

    units = nb if grows == chunk else 1
    unb, urows = nb // units, rows // units
    per_unit = groups // units

    def conv_block(ci, u):
        cs = slice(ci * LANES, (ci + 1) * LANES)
        bs = slice(u * unb, (u + 1) * unb)
        return _silu(_causal_conv(r.extd, r.dnw, cs, length, bs)).reshape(urows, LANES)

    row = lax.broadcasted_iota(jnp.int32, (grows, grows), 0)
    col = lax.broadcasted_iota(jnp.int32, (grows, grows), 1)
    same_chunk = (row >> log2c) == (col >> log2c)
    m_incl = same_chunk & (row >= col)
    m_strict = same_chunk & (row > col)

    for u in range(units):
        for hh in range(DN_HEADS):
            qc = conv_block(hh, u)
            yield 0.2 * per_unit
            kc = conv_block(DN_HEADS + hh, u)
            yield 0.2 * per_unit
            vc_u = conv_block(2 * DN_HEADS + hh, u)
            q_u = qc * (lax.rsqrt(jnp.sum(qc * qc, axis=-1, keepdims=True) + EPS) * (DN_HEAD_DIM ** -0.5))
            k_u = kc * lax.rsqrt(jnp.sum(kc * kc, axis=-1, keepdims=True) + EPS)
            yield 0.2 * per_unit
            for t in range(per_unit):
                g = u * per_unit + t
                c = g * DN_HEADS + hh
                rs = slice(t * grows, (t + 1) * grows)
                q, k, vc = q_u[rs], k_u[rs], vc_u[rs]
                nbeta_b = colb(L_BETA + hh, g)
                egc_b = colb(L_EGC + hh, g)
                nkb = k * nbeta_b
                k16 = k.astype(BF16)
                diff = colb(L_GC + hh, g, grows) - gc_t[hh:hh + 1, g * grows:(g + 1) * grows]
                decay = jnp.where(m_incl, jnp.exp(diff), 0.0)
                na = jnp.where(m_strict, _dot_nt(nkb.astype(BF16), k16) * decay, 0.0)
                st.na[c] = na.astype(BF16)
                yield 0.25
                st.qk[c] = (_dot_nt(q.astype(BF16), k16) * decay).astype(BF16)
                st.rhs[c] = jnp.concatenate([vc * nbeta_b, nkb * egc_b], axis=1).astype(BF16)
                st.qg[c] = q * egc_b
                st.kdt[c] = (k * colb(L_KDF + hh, g)).T
                st.egl[c] = colb(L_EGL + hh, g)[0:st.egl.shape[1]]
                yield 0.25

    prest = _dot(h16, r.wrest[...])
    st.z[...] = _dot(h16, r.wqkvz[:, W_Z:W_Z + DN_WIDTH])
    p_in = prest[:, 0:POOL_WIDTH]
    sc_x = prest[:, POOL_WIDTH:POOL_WIDTH + SC_WIDTH]
    sc_b = prest[:, POOL_WIDTH + SC_WIDTH:POOL_WIDTH + 2 * SC_WIDTH]
    sc_c = prest[:, POOL_WIDTH + 2 * SC_WIDTH:]
    yield 0.5

    ph = POOL_HIST
    r.extp[:, ph:ph + length, :] = p_in.reshape(nb, length, POOL_WIDTH)
    r.op[...] = r.extp[:, ph + length - POOL_BUF:ph + length, :]
    zero8 = jnp.zeros((nb, SUBLANES, POOL_WIDTH), F32)
    end = ph + length
    r.s2[:, 0:8, :] = zero8
    r.s4[:, 0:8, :] = zero8
    r.s8[:, 0:8, :] = zero8
    r.s2[:, 8:end, :] = r.extp[:, 8:end, :] + r.extp[:, 7:end - 1, :]
    r.s4[:, 8:end, :] = r.s2[:, 8:end, :] + r.s2[:, 6:end - 2, :]
    r.s8[:, 8:end, :] = r.s4[:, 8:end, :] + r.s4[:, 4:end - 4, :]
    s16 = r.s8[:, ph:end, :] + r.s8[:, ph - 8:end - 8, :]
    lane3 = lax.broadcasted_iota(jnp.int32, (nb, length, POOL_WIDTH), 2)
    grp = lane3 // POOL_GROUP_DIM
    wsum = jnp.where(grp == 0, r.s2[:, ph:end, :],
                     jnp.where(grp == 1, r.s4[:, ph:end, :],
                               jnp.where(grp == 2, r.s8[:, ph:end, :], s16)))
    win = jnp.where(grp == 0, POOL_WINDOWS[0],
                    jnp.where(grp == 1, POOL_WINDOWS[1],
                              jnp.where(grp == 2, POOL_WINDOWS[2], POOL_WINDOWS[3])))
    tpos = lax.broadcasted_iota(jnp.int32, (nb, length, POOL_WIDTH), 1) + (pos0 + j * length)
    cnt = jnp.minimum(tpos + 1, win).astype(F32)
    dpool = (wsum / cnt - r.extp[:, ph:end, :]).reshape(rows, POOL_WIDTH)
    r.cat_a[:, 0:POOL_WIDTH] = (_dot(dpool.astype(BF16), r.pw[...]) * r.pscale[...]).astype(BF16)
    yield 0.4

    r.exts[:, CONV_HIST:CONV_HIST + length, :] = (sc_c * sc_x).reshape(nb, length, SC_WIDTH)
    r.osc[...] = r.exts[:, CONV_HIST + length - (SC_CONV - 1):CONV_HIST + length, :]
    cconv = _causal_conv(r.exts, r.scw, slice(0, SC_WIDTH), length)
    r.cat_a[:, POOL_WIDTH:] = (sc_b * cconv.reshape(rows, SC_WIDTH)).astype(BF16)

    st.xy[...] = x + _dot(r.cat_a[...], r.wout[DN_WIDTH:, :])


def _stage_b(r, st, *, nb, length, chunk, groups):
    rows = nb * length
    grows = rows // groups
    chains = groups * DN_HEADS
    log2c = int(math.log2(chunk))

    tinv = []
    yield from _unit_lower_inverses([st.na[c] for c in range(chains)], log2c, tinv)
    for c in range(chains):
        nsol = _dot(tinv[c], st.rhs[c])
        r.nu[c] = nsol[:, :DN_HEAD_DIM]
        r.nw[c] = nsol[:, DN_HEAD_DIM:]
        if (c + 1) % DN_HEADS == 0:
            yield 0.5

    if grows == chunk:
        per_seq = groups // nb
        assert groups == nb * per_seq and DN_HEADS % 2 == 0
        zero = jnp.zeros((DN_HEAD_DIM, DN_HEAD_DIM), BF16)
        for c0 in range(0, chains, 2):
            g, h0 = divmod(c0, DN_HEADS)
            seq = g // per_seq
            pair = ((c0, h0, slice(0, DN_HEAD_DIM)), (c0 + 1, h0 + 1, slice(DN_HEAD_DIM, 2 * DN_HEAD_DIM)))
            s_pair = [r.od[seq, hh] for _, hh, _ in pair]
            s2 = jnp.concatenate([jnp.concatenate([s_pair[0].astype(BF16), zero], axis=1),
                                  jnp.concatenate([zero, s_pair[1].astype(BF16)], axis=1)], axis=0)
            nwq = jnp.concatenate([jnp.concatenate([r.nw[c] for c, _, _ in pair], axis=1),
                                   jnp.concatenate([st.qg[c] for c, _, _ in pair], axis=1)],
                                  axis=0).astype(BF16)
            res = _dot(nwq, s2)
            for (c, hh, lanes), s in zip(pair, s_pair):
                vn16 = (res[:grows, lanes] - r.nu[c]).astype(BF16)
                r.o[g * grows:(g + 1) * grows, _head_slice(hh)] = res[grows:, lanes] + _dot(st.qk[c], vn16)
                r.od[seq, hh] = s * st.egl[c, 0:1, :] + _dot(st.kdt[c].astype(BF16), vn16)
            yield 0.6
    else:
        assert groups == 1 and rows // chunk == nb
        seg = lax.broadcasted_iota(jnp.int32, (nb, 1, rows), 2) >> log2c
        own = seg == lax.broadcasted_iota(jnp.int32, (nb, 1, rows), 0)
        for hh in range(DN_HEADS):
            vn_parts, qs_parts = [], []
            for b in range(nb):
                rs = slice(b * chunk, (b + 1) * chunk)
                nwq = jnp.concatenate([r.nw[hh, rs, :], st.qg[hh, rs, :]], axis=0).astype(BF16)
                res = _dot(nwq, r.od[b, hh].astype(BF16))
                vn_parts.append(res[:chunk] - r.nu[hh, rs, :])
                qs_parts.append(res[chunk:])
            vn16 = jnp.concatenate(vn_parts, axis=0).astype(BF16)
            r.o[:, _head_slice(hh)] = jnp.concatenate(qs_parts, axis=0) + _dot(st.qk[hh], vn16)
            kd_own = jnp.where(own, st.kdt[hh][None], 0.0).astype(BF16).reshape(nb * DN_HEAD_DIM, rows)
            upd = _dot(kd_own, vn16).reshape(nb, DN_HEAD_DIM, DN_HEAD_DIM)
            egl = st.egl[hh].reshape(nb, chunk, DN_HEAD_DIM)[:, 0:1, :]
            r.od[:, hh, :, :] = r.od[:, hh, :, :] * egl + upd
            yield 0.3

    for hh in range(DN_HEADS):
        hs = _head_slice(hh)
        r.cat_b[:, hs] = (_rms_scale(r.o[:, hs]) * r.onorm[...] * _silu(st.z[:, hs])).astype(BF16)
    y = st.xy[...] + _dot(r.cat_b[...], r.wout[:DN_WIDTH, :])
    r.xo[...] = y.reshape(nb, length, D_MODEL)


_MIXER_IN = ("x", "sd", "sdc", "sp", "ssc", "nmix", "wgt", "wqkvz", "wrest", "dnw", "alog", "dtb", "onorm", "pw",
             "pscale", "scw", "wout")
_MIXER_OUT = ("xo", "od", "odc", "op", "osc")
_MIXER_PRIVATE = ("extd", "extp", "s2", "s4", "s8", "exts", "cat_a", "nu", "nw", "o", "cat_b")
_MIXER_SET = ("na", "qk", "rhs", "qg", "kdt", "egl", "z", "xy")


def _mixer_scratch(nb, length, chunk, n_sets, groups):
    rows = nb * length
    grows = rows // groups
    chains = groups * DN_HEADS
    one_chunk = grows == chunk
    private = [
        pltpu.VMEM((nb, CONV_HIST + length, 3 * DN_WIDTH), F32),
        pltpu.VMEM((nb, POOL_HIST + length, POOL_WIDTH), F32),
        pltpu.VMEM((nb, POOL_HIST + length, POOL_WIDTH), F32),
        pltpu.VMEM((nb, POOL_HIST + length, POOL_WIDTH), F32),
        pltpu.VMEM((nb, POOL_HIST + length, POOL_WIDTH), F32),
        pltpu.VMEM((nb, CONV_HIST + length, SC_WIDTH), F32),
        pltpu.VMEM((rows, D_REST), BF16),
        pltpu.VMEM((chains, grows, DN_HEAD_DIM), F32),
        pltpu.VMEM((chains, grows, DN_HEAD_DIM), F32),
        pltpu.VMEM((rows, DN_WIDTH), F32),
        pltpu.VMEM((rows, DN_WIDTH), BF16),
    ]
    one_set = [
        pltpu.VMEM((chains, grows, grows), BF16),
        pltpu.VMEM((chains, grows, grows), BF16),
        pltpu.VMEM((chains, grows, 2 * DN_HEAD_DIM), BF16),
        pltpu.VMEM((chains, grows, DN_HEAD_DIM), F32),
        pltpu.VMEM((chains, DN_HEAD_DIM, grows), F32),
        pltpu.VMEM((chains, SUBLANES if one_chunk else grows, DN_HEAD_DIM), F32),
        pltpu.VMEM((rows, DN_WIDTH), F32),
        pltpu.VMEM((rows, D_MODEL), F32),
    ]
    return private + one_set * n_sets


def _zero_other_layers(ref, layer):
    for l in range(ref.shape[0]):
        if l != layer:
            ref[l] = jnp.zeros(ref.shape[1:], F32)


def _mixer_kernel(*refs, nb, length, chunk, pos0, n_tiles, n_alias, layer, whole, groups):
    n_in, n_set = len(_MIXER_IN), len(_MIXER_SET)
    refs = refs[:n_in] + refs[n_in + n_alias:]
    names = _MIXER_IN + _MIXER_OUT + _MIXER_PRIVATE
    r = types.SimpleNamespace(**dict(zip(names, refs[:len(names)])))
    set_refs = refs[len(names):]
    sets = [types.SimpleNamespace(**dict(zip(_MIXER_SET, set_refs[i * n_set:(i + 1) * n_set])))
            for i in range(len(set_refs) // n_set)]
    j = pl.program_id(1)
    whole_refs = (r.od, r.odc, r.op, r.osc)
    if whole:
        r.od, r.odc, r.op, r.osc = (f.at[layer] for f in whole_refs)

    def init_state():
        if whole:
            for f in whole_refs:
                _zero_other_layers(f, layer)
        r.od[...] = r.sd[...]

    _carry_history(j, length,
                   [(r.extd, r.sdc, CONV_HIST, DN_CONV - 1), (r.extp, r.sp, POOL_HIST, POOL_BUF),
                    (r.exts, r.ssc, CONV_HIST, SC_CONV - 1)], first_tile=init_state)

    stage_a = functools.partial(_stage_a, j, r, nb=nb, length=length, chunk=chunk, pos0=pos0, groups=groups)
    stage_b = functools.partial(_stage_b, r, nb=nb, length=length, chunk=chunk, groups=groups)

    if len(sets) == 1:
        _interleave(stage_a(sets[0]))
        _interleave(stage_b(sets[0]))
        return

    middle = (j > 0) & (j < n_tiles)

    @pl.when(j == 0)
    def _():
        _interleave(stage_a(sets[0]))

    @pl.when(middle & (j % 2 == 1))
    def _():
        _interleave(stage_b(sets[0]), stage_a(sets[1]), speeds=B_AHEAD)

    @pl.when(middle & (j % 2 == 0))
    def _():
        _interleave(stage_b(sets[1]), stage_a(sets[0]), speeds=B_AHEAD)

    @pl.when(j == n_tiles)
    def _():
        _interleave(stage_b(sets[(n_tiles - 1) % 2]))


def _ffn_kernel(x_ref, sf_ref, nffn_ref, wg_ref, fw_ref, wu_ref, wd_ref, fnorm_ref, *rest,
                nb, length, final, layer, whole):
    xo_ref, of_full, extf_s = rest[-3:]
    of_ref = of_full.at[layer] if whole else of_full
    j = pl.program_id(1)
    rows = nb * length
    _carry_history(j, length, [(extf_s, sf_ref, CONV_HIST, FFN_CONV - 1)],
                   first_tile=functools.partial(_zero_other_layers, of_full, layer) if whole else None)
    x = x_ref[...].reshape(rows, D_MODEL)
    h16 = (_rms_scale(x) * nffn_ref[...]).astype(BF16)
    extf_s[:, CONV_HIST:CONV_HIST + length, :] = _dot(h16, wg_ref[...]).reshape(nb, length, D_FF)
    of_ref[...] = extf_s[:, CONV_HIST + length - (FFN_CONV - 1):CONV_HIST + length, :]
    gate = _causal_conv(extf_s, fw_ref, slice(0, D_FF), length)
    up = _dot(h16, wu_ref[...])
    act = (_silu(gate).reshape(rows, D_FF) * up).astype(BF16)
    y = x + _dot(act, wd_ref[...])
    if final:
        y = _rms_scale(y) * fnorm_ref[...]
    xo_ref[...] = y.reshape(nb, length, D_MODEL)


def _layer_spec(arr, layer, cols=None):
    tail = arr.shape[1:] if cols is None else arr.shape[1:-1] + (cols,)
    zeros = (0,) * len(tail)
    return pl.BlockSpec((None,) + tail, lambda i, j: (layer,) + zeros, pipeline_mode=pl.Buffered(1))


def _state_spec(arr, layer, nb):
    tail = arr.shape[2:]
    zeros = (0,) * len(tail)
    return pl.BlockSpec((None, nb) + tail, lambda i, j: (layer, i) + zeros)


def _state_out_spec(shape, layer, nb, whole):
    tail = shape[2:]
    zeros = (0,) * len(tail)
    if whole:
        return pl.BlockSpec((shape[0], nb) + tail, lambda i, j: (0, i) + zeros)
    return pl.BlockSpec((None, nb) + tail, lambda i, j: (layer, i) + zeros)


def _alias_args(prev, n_in, first_out):
    if prev is None:
        return [], [], {}
    specs = [pl.BlockSpec(memory_space=pl.ANY)] * len(prev)
    return list(prev), specs, {n_in + k: first_out + k for k in range(len(prev))}


def _mixer_call(x, states, state_layer, layer, depth, w, prev, *, nb, length, chunk, pos0, groups):
    batch, seq, _ = x.shape
    sd, sdc, sp, ssc = states
    n_tiles = seq // length
    pipelined = n_tiles > 1
    if pipelined:
        last = n_tiles - 1
        grid = (batch // nb, n_tiles + 1)
        x_in_spec = pl.BlockSpec((nb, length, D_MODEL), lambda i, j: (i, jnp.minimum(j, last), 0))
        x_out_spec = pl.BlockSpec((nb, length, D_MODEL), lambda i, j: (i, jnp.maximum(j - 1, 0), 0))
    else:
        grid = (batch // nb, n_tiles)
        x_in_spec = x_out_spec = pl.BlockSpec((nb, length, D_MODEL), lambda i, j: (i, j, 0))
    weights = tuple(w[name] for name in _MIXER_IN[5:])
    out_shape = (
        jax.ShapeDtypeStruct(x.shape, F32),
        jax.ShapeDtypeStruct((depth, batch) + sd.shape[2:], F32),
        jax.ShapeDtypeStruct((depth, batch) + sdc.shape[2:], F32),
        jax.ShapeDtypeStruct((depth, batch) + sp.shape[2:], F32),
        jax.ShapeDtypeStruct((depth, batch) + ssc.shape[2:], F32),
    )
    operands = (x, sd, sdc, sp, ssc) + weights
    alias_ops, alias_specs, aliases = _alias_args(prev, len(operands), 1)
    whole = prev is None
    kern = functools.partial(_mixer_kernel, nb=nb, length=length, chunk=chunk, pos0=pos0, n_tiles=n_tiles,
                             n_alias=len(alias_ops), layer=layer, whole=whole, groups=groups)
    return pl.pallas_call(
        kern,
        grid=grid,
        in_specs=[x_in_spec, _state_spec(sd, state_layer, nb), _state_spec(sdc, state_layer, nb),
                  _state_spec(sp, state_layer, nb), _state_spec(ssc, state_layer, nb)]
                 + [_layer_spec(a, layer, W_Z + DN_WIDTH if name == "wqkvz" else None)
                    for name, a in zip(_MIXER_IN[5:], weights)] + alias_specs,
        out_specs=(x_out_spec,) + tuple(_state_out_spec(s.shape, layer, nb, whole) for s in out_shape[1:]),
        out_shape=out_shape,
        input_output_aliases=aliases,
        scratch_shapes=_mixer_scratch(nb, length, chunk, 2 if pipelined else 1, groups),
        compiler_params=pltpu.CompilerParams(
            dimension_semantics=("parallel", "arbitrary"), vmem_limit_bytes=VMEM_LIMIT_BYTES),
        name="mixer",
    )(*operands, *alias_ops)


def _ffn_call(x, sf, state_layer, layer, depth, w, prev, *, nb, length, final):
    batch, seq, _ = x.shape
    grid = (batch // nb, seq // length)
    x_spec = pl.BlockSpec((nb, length, D_MODEL), lambda i, j: (i, j, 0))
    weights = (w["nffn"], w["wg"], w["fw"], w["wu"], w["wd"])
    fnorm = w["fnorm"]
    out_shape = (jax.ShapeDtypeStruct(x.shape, F32),
                 jax.ShapeDtypeStruct((depth, batch) + sf.shape[2:], F32))
    operands = (x, sf) + weights + (fnorm,)
    alias_ops, alias_specs, aliases = _alias_args(prev, len(operands), 1)
    whole = prev is None
    kern = functools.partial(_ffn_kernel, nb=nb, length=length, final=final, layer=layer, whole=whole)
    return pl.pallas_call(
        kern,
        grid=grid,
        in_specs=[x_spec, _state_spec(sf, state_layer, nb)] + [_layer_spec(a, layer) for a in weights]
                 + [pl.BlockSpec(fnorm.shape, lambda i, j: (0, 0))] + alias_specs,
        out_specs=(x_spec, _state_out_spec(out_shape[1].shape, layer, nb, whole)),
        out_shape=out_shape,
        input_output_aliases=aliases,
        scratch_shapes=[pltpu.VMEM((nb, CONV_HIST + length, D_FF), F32)],
        compiler_params=pltpu.CompilerParams(
            dimension_semantics=("parallel", "arbitrary"), vmem_limit_bytes=VMEM_LIMIT_BYTES),
        name="ffn",
    )(*operands, *alias_ops)


def _prepare_weights(norm_mix, w_in, dn_conv_w, dn_a_log, dn_dt_bias, dn_out_norm, pool_w, pool_scale,
                     sconv_w, w_out, norm_ffn, w_ffn_gate, ffn_conv_w, w_ffn_up, w_ffn_down, final_norm):
    depth = w_in.shape[0]
    assert N_GATE == SUBLANES
    g0 = W_Z + DN_WIDTH
    rest0 = g0 + N_GATE
    w16 = w_in.astype(BF16)
    wgt = jnp.swapaxes(w16[:, :, g0:g0 + N_GATE], 1, 2)
    pad = jnp.zeros((depth, SUBLANES - DN_HEADS, 1), F32)
    eye = jnp.eye(len(POOL_WINDOWS), dtype=F32)
    pw = (pool_w[:, :, :, None, :] * eye[None, :, None, :, None]).reshape(depth, POOL_WIDTH, POOL_WIDTH)
    return {
        "nmix": norm_mix[:, None, :],
        "wgt": wgt,
        "wqkvz": w16,
        "wrest": w16[:, :, rest0:rest0 + REST_WIDTH],
        "dnw": dn_conv_w,
        "alog": jnp.concatenate([dn_a_log[:, :, None], pad], axis=1),
        "dtb": jnp.concatenate([dn_dt_bias[:, :, None], pad], axis=1),
        "onorm": dn_out_norm[:, None, :],
        "pw": pw.astype(BF16),
        "pscale": pool_scale[:, None, :],
        "scw": sconv_w,
        "wout": w_out.astype(BF16),
        "nffn": norm_ffn[:, None, :],
        "wg": w_ffn_gate.astype(BF16),
        "fw": ffn_conv_w,
        "wu": w_ffn_up.astype(BF16),
        "wd": w_ffn_down.astype(BF16),
        "fnorm": final_norm[None, :],
    }


def _trunk(x, states, state_layers, weights, depth, *, nb_mix, groups, nb_ffn, length, ffn_length, chunk,
           pos0):
    sd, sdc, sp, ssc, sf = states
    mix_out, ffn_out = None, None
    for l in range(depth):
        sl = state_layers[l]
        x, *mix_out = _mixer_call(x, (sd, sdc, sp, ssc), sl, l, depth, weights, mix_out,
                                  nb=nb_mix, length=length, chunk=chunk, pos0=pos0, groups=groups)
        x, *ffn_out = _ffn_call(x, sf, sl, l, depth, weights, ffn_out,
                                nb=nb_ffn, length=ffn_length, final=(l == depth - 1))
    return x, tuple(mix_out) + tuple(ffn_out)


PROMPT_TILE = 256
PROMPT_MIX_ROWS = 2
PROMPT_CHUNK = 128
PROMPT_FFN_TILE = 512
SAMPLE_MIX_ROWS = 16
SAMPLE_FFN_ROWS = 64


def kernel(x_prompt, x_sample, state_delta, state_delta_conv, state_pool, state_sconv, state_ffn_conv,
           norm_mix, w_in, dn_conv_w, dn_a_log, dn_dt_bias, dn_out_norm, pool_w, pool_scale,
           sconv_w, w_out, norm_ffn, w_ffn_gate, ffn_conv_w, w_ffn_up, w_ffn_down, final_norm):
    depth = w_in.shape[0]
    weights = _prepare_weights(norm_mix, w_in, dn_conv_w, dn_a_log, dn_dt_bias, dn_out_norm, pool_w,
                               pool_scale, sconv_w, w_out, norm_ffn, w_ffn_gate, ffn_conv_w, w_ffn_up,
                               w_ffn_down, final_norm)

    batch, seq, _ = x_prompt.shape
    dec_batch, dec_seq, _ = x_sample.shape
    sample_states = (state_delta, state_delta_conv, state_pool, state_sconv, state_ffn_conv)
    zero_states = tuple(jnp.zeros((1, batch) + s.shape[2:], F32) for s in sample_states)

    tile, chunk = min(PROMPT_TILE, seq), min(PROMPT_CHUNK, seq)
    y_prompt, p_st = _trunk(x_prompt, zero_states, (0,) * depth, weights, depth,
                            nb_mix=PROMPT_MIX_ROWS, groups=PROMPT_MIX_ROWS * (tile // chunk), nb_ffn=1,
                            length=tile, ffn_length=min(PROMPT_FFN_TILE, seq), chunk=chunk, pos0=0)
    y_sample, s_st = _trunk(x_sample, sample_states, tuple(range(depth)), weights, depth,
                            nb_mix=SAMPLE_MIX_ROWS, groups=1, nb_ffn=SAMPLE_FFN_ROWS,
                            length=dec_seq, ffn_length=dec_seq,
                            chunk=min(DN_CHUNK, dec_seq), pos0=PAST_LEN)
    return (y_prompt, y_sample) + p_st + s_st
```

```python
import functools
import math
import types

import jax
import jax.numpy as jnp
from jax import lax
from jax.experimental import pallas as pl
from jax.experimental.pallas import tpu as pltpu

F32 = jnp.float32
BF16 = jnp.bfloat16

LANES = 128
SUBLANES = 8
BF16_ROWS = 16
VMEM_LIMIT_BYTES = 56 * 1024 * 1024

D_MODEL = 1024
DN_HEADS = 4
DN_HEAD_DIM = 128
DN_WIDTH = DN_HEADS * DN_HEAD_DIM
DN_CONV = 4
DN_CHUNK = 64
POOL_WINDOWS = (2, 4, 8, 16)
POOL_WIDTH = 256
POOL_GROUP_DIM = POOL_WIDTH // len(POOL_WINDOWS)
POOL_BUF = max(POOL_WINDOWS) - 1
SC_WIDTH = 256
SC_CONV = 3
D_FF = 2816
FFN_CONV = 3
EPS = 1e-6
PAST_LEN = 16384

N_GATE = 2 * DN_HEADS
W_Z = 3 * DN_WIDTH
REST_WIDTH = POOL_WIDTH + 3 * SC_WIDTH
D_REST = POOL_WIDTH + SC_WIDTH

CONV_HIST = SUBLANES
POOL_HIST = 24

L_GC, L_BETA, L_EGC, L_KDF, L_EGL = 0, DN_HEADS, SUBLANES, 2 * SUBLANES, 3 * SUBLANES


def _silu(t):
    return t / (1.0 + jnp.exp(-t))


def _softplus(t):
    return jnp.maximum(t, 0.0) + jnp.log1p(jnp.exp(-jnp.abs(t)))


def _rms_scale(t):
    return t * lax.rsqrt(jnp.mean(t * t, axis=-1, keepdims=True) + EPS)


def _dot(a, b):
    return jnp.dot(a, b, preferred_element_type=F32)


def _dot_nt(a, b):
    return lax.dot_general(a, b, (((1,), (1,)), ((), ())), preferred_element_type=F32)


def _head_slice(hh):
    return slice(hh * DN_HEAD_DIM, (hh + 1) * DN_HEAD_DIM)


B_AHEAD = (4.0, 1.0)


def _interleave(*stages, speeds=None):
    speeds = speeds or [1.0] * len(stages)
    clock = [0.0] * len(stages)
    live = list(range(len(stages)))
    while live:
        i = min(live, key=lambda s: clock[s])
        try:
            clock[i] += next(stages[i]) / speeds[i]
        except StopIteration:
            live.remove(i)


def _carry_history(j, length, items, first_tile=None):
    @pl.when(j == 0)
    def _():
        if first_tile is not None:
            first_tile()
        for ext, state_ref, hist, nbuf in items:
            ext[:, 0:hist - nbuf, :] = jnp.zeros((ext.shape[0], hist - nbuf, ext.shape[2]), F32)
            ext[:, hist - nbuf:hist, :] = state_ref[...]

    @pl.when(j != 0)
    def _():
        for ext, state_ref, hist, nbuf in items:
            ext[:, hist - nbuf:hist, :] = ext[:, hist + length - nbuf:hist + length, :]


def _causal_conv(ext, w_ref, cs, length, bs=slice(None)):
    taps = w_ref.shape[0]
    tiles = length // SUBLANES
    e4 = ext[bs, 0:CONV_HIST + length, cs]
    nb, _, width = e4.shape
    e4 = e4.reshape(nb, tiles + 1, SUBLANES, width)
    sub4 = lax.broadcasted_iota(jnp.int32, (nb, tiles, SUBLANES, width), 2)
    acc = None
    for i in range(taps):
        s = taps - 1 - i
        if s:
            rot = pltpu.roll(e4, s, axis=2)
            term = jnp.where(sub4 >= s, rot[:, 1:], rot[:, :-1])
        else:
            term = e4[:, 1:]
        term = term * w_ref[i:i + 1, cs][None, None]
        acc = term if acc is None else acc + term
    return acc.reshape(nb, length, width)


def _unit_lower_inverses(na16, log2c, out):
    heads = range(len(na16))
    n = na16[0].shape[0]

    def spread(fn):
        vals = []
        for i in heads:
            vals.append(fn(i))
            if (i + 1) % DN_HEADS == 0:
                yield 0.3
        return vals

    row = lax.broadcasted_iota(jnp.int32, (n, n), 0)
    col = lax.broadcasted_iota(jnp.int32, (n, n), 1)
    pair = ((row >> 1) == (col >> 1)) & ((row & 1) == 1) & ((col & 1) == 0)
    eye = jnp.where(row == col, 1.0, 0.0).astype(F32)
    t16 = yield from spread(lambda i: jnp.where(pair, na16[i].astype(F32), eye).astype(BF16))
    for lvl in range(1, log2c):
        blk = 1 << lvl
        if blk < BF16_ROWS:
            off = (((row >> (lvl + 1)) == (col >> (lvl + 1)))
                   & (((row >> lvl) & 1) == 1) & (((col >> lvl) & 1) == 0))
            p16 = yield from spread(lambda i: _dot(na16[i], t16[i]).astype(BF16))
            t16 = yield from spread(
                lambda i: t16[i] + jnp.where(off, _dot(t16[i], p16[i]), 0.0).astype(BF16))
        else:
            pairs = n // (2 * blk)
            rowh = lax.broadcasted_iota(jnp.int32, (n // 2, n), 0)
            colh = lax.broadcasted_iota(jnp.int32, (n // 2, n), 1)
            offh = ((colh >> (lvl + 1)) == (rowh >> lvl)) & (((colh >> lvl) & 1) == 0)
            t4 = [t16[i].reshape(pairs, 2, blk, n) for i in heads]
            na_odd = [na16[i].reshape(pairs, 2, blk, n)[:, 1].reshape(n // 2, n) for i in heads]
            t_odd = [t4[i][:, 1].reshape(n // 2, n) for i in heads]
            p16 = yield from spread(
                lambda i: jnp.broadcast_to(_dot(na_odd[i], t16[i]).reshape(pairs, 1, blk, n),
                                           (pairs, 2, blk, n)).reshape(n, n).astype(BF16))
            x_odd = yield from spread(
                lambda i: jnp.where(offh, _dot(t_odd[i], p16[i]), 0.0).astype(BF16))
            t16 = [jnp.concatenate([t4[i][:, 0:1], (t_odd[i] + x_odd[i]).reshape(pairs, 1, blk, n)],
                                   axis=1).reshape(n, n) for i in heads]
    out.extend(t16)


def _stage_a(j, r, st, *, nb, length, chunk, pos0, groups):
    rows = nb * length
    grows = rows // groups
    log2c = int(math.log2(chunk))

    x = r.x[...].reshape(rows, D_MODEL)
    h16 = (_rms_scale(x) * r.nmix[...]).astype(BF16)

    gates_t = _dot_nt(r.wgt[...], h16)
    g_t = -jnp.exp(r.alog[...]) * _softplus(gates_t + r.dtb[...])
    beta_t = 1.0 / (1.0 + jnp.exp(-gates_t))
    pos_in_chunk = lax.broadcasted_iota(jnp.int32, (SUBLANES, rows), 1) & (chunk - 1)
    gc_t = g_t
    sfx_t = g_t
    step = 1
    while step < chunk:
        gc_t = gc_t + jnp.where(pos_in_chunk >= step, pltpu.roll(gc_t, step, axis=1), 0.0)
        sfx_t = sfx_t + jnp.where(pos_in_chunk + step < chunk,
                                  pltpu.roll(sfx_t, rows - step, axis=1), 0.0)
        step *= 2
    rest_t = sfx_t - g_t
    sub = lax.broadcasted_iota(jnp.int32, (SUBLANES, rows), 0)
    cols = jnp.concatenate(
        [jnp.where(sub < DN_HEADS, gc_t, -beta_t), jnp.exp(gc_t), jnp.exp(rest_t), jnp.exp(gc_t + rest_t),
         jnp.zeros((LANES - 4 * SUBLANES, rows), F32)], axis=0).T

    def colb(lane, g, width=LANES):
        return jnp.broadcast_to(cols[g * grows:(g + 1) * grows, lane:lane + 1], (grows, width))

    yield 1.0
    for part in range(3):
        cs = slice(part * DN_WIDTH, (part + 1) * DN_WIDTH)
        r.extd[:, CONV_HIST:CONV_HIST + length, cs] = _dot(h16, r.wqkvz[:, cs]).reshape(nb, length, DN_WIDTH)
        yield 0.3
    r.odc[...] = r.extd[:, CONV_HIST + length - (DN_CONV - 1):CONV_HIST + length, :]

    units = nb if grows == chunk else 1
    unb, urows = nb // units, rows // units
    per_unit = groups // units

    def conv_block(ci, u):
        cs = slice(ci * LANES, (ci + 1) * LANES)
        bs = slice(u * unb, (u + 1) * unb)
        return _silu(_causal_conv(r.extd, r.dnw, cs, length, bs)).reshape(urows, LANES)

    row = lax.broadcasted_iota(jnp.int32, (grows, grows), 0)
    col = lax.broadcasted_iota(jnp.int32, (grows, grows), 1)
    same_chunk = (row >> log2c) == (col >> log2c)
    m_incl = same_chunk & (row >= col)
    m_strict = same_chunk & (row > col)

    for u in range(units):
        for hh in range(DN_HEADS):
            qc = conv_block(hh, u)
            yield 0.2 * per_unit
            kc = conv_block(DN_HEADS + hh, u)
            yield 0.2 * per_unit
            vc_u = conv_block(2 * DN_HEADS + hh, u)
            q_u = qc * (lax.rsqrt(jnp.sum(qc * qc, axis=-1, keepdims=True) + EPS) * (DN_HEAD_DIM ** -0.5))
            k_u = kc * lax.rsqrt(jnp.sum(kc * kc, axis=-1, keepdims=True) + EPS)
            yield 0.2 * per_unit
            for t in range(per_unit):
                g = u * per_unit + t
                c = g * DN_HEADS + hh
                rs = slice(t * grows, (t + 1) * grows)
                q, k, vc = q_u[rs], k_u[rs], vc_u[rs]
                nbeta_b = colb(L_BETA + hh, g)
                egc_b = colb(L_EGC + hh, g)
                nkb = k * nbeta_b
                k16 = k.astype(BF16)
                diff = colb(L_GC + hh, g, grows) - gc_t[hh:hh + 1, g * grows:(g + 1) * grows]
                decay = jnp.where(m_incl, jnp.exp(diff), 0.0)
                na = jnp.where(m_strict, _dot_nt(nkb.astype(BF16), k16) * decay, 0.0)
                st.na[c] = na.astype(BF16)
                yield 0.25
                st.qk[c] = (_dot_nt(q.astype(BF16), k16) * decay).astype(BF16)
                st.rhs[c] = jnp.concatenate([vc * nbeta_b, nkb * egc_b], axis=1).astype(BF16)
                st.qg[c] = q * egc_b
                st.kdt[c] = (k * colb(L_KDF + hh, g)).T
                st.egl[c] = colb(L_EGL + hh, g)[0:st.egl.shape[1]]
                yield 0.25

    prest = _dot(h16, r.wrest[...])
    st.z[...] = _dot(h16, r.wqkvz[:, W_Z:W_Z + DN_WIDTH])
    p_in = prest[:, 0:POOL_WIDTH]
    sc_x = prest[:, POOL_WIDTH:POOL_WIDTH + SC_WIDTH]
    sc_b = prest[:, POOL_WIDTH + SC_WIDTH:POOL_WIDTH + 2 * SC_WIDTH]
    sc_c = prest[:, POOL_WIDTH + 2 * SC_WIDTH:]
    yield 0.5

    ph = POOL_HIST
    r.extp[:, ph:ph + length, :] = p_in.reshape(nb, length, POOL_WIDTH)
    r.op[...] = r.extp[:, ph + length - POOL_BUF:ph + length, :]
    zero8 = jnp.zeros((nb, SUBLANES, POOL_WIDTH), F32)
    end = ph + length
    r.s2[:, 0:8, :] = zero8
    r.s4[:, 0:8, :] = zero8
    r.s8[:, 0:8, :] = zero8
    r.s2[:, 8:end, :] = r.extp[:, 8:end, :] + r.extp[:, 7:end - 1, :]
    r.s4[:, 8:end, :] = r.s2[:, 8:end, :] + r.s2[:, 6:end - 2, :]
    r.s8[:, 8:end, :] = r.s4[:, 8:end, :] + r.s4[:, 4:end - 4, :]
    s16 = r.s8[:, ph:end, :] + r.s8[:, ph - 8:end - 8, :]
    lane3 = lax.broadcasted_iota(jnp.int32, (nb, length, POOL_WIDTH), 2)
    grp = lane3 // POOL_GROUP_DIM
    wsum = jnp.where(grp == 0, r.s2[:, ph:end, :],
                     jnp.where(grp == 1, r.s4[:, ph:end, :],
                               jnp.where(grp == 2, r.s8[:, ph:end, :], s16)))
    win = jnp.where(grp == 0, POOL_WINDOWS[0],
                    jnp.where(grp == 1, POOL_WINDOWS[1],
                              jnp.where(grp == 2, POOL_WINDOWS[2], POOL_WINDOWS[3])))
    tpos = lax.broadcasted_iota(jnp.int32, (nb, length, POOL_WIDTH), 1) + (pos0 + j * length)
    cnt = jnp.minimum(tpos + 1, win).astype(F32)
    dpool = (wsum / cnt - r.extp[:, ph:end, :]).reshape(rows, POOL_WIDTH)
    r.cat_a[:, 0:POOL_WIDTH] = (_dot(dpool.astype(BF16), r.pw[...]) * r.pscale[...]).astype(BF16)
    yield 0.4

    r.exts[:, CONV_HIST:CONV_HIST + length, :] = (sc_c * sc_x).reshape(nb, length, SC_WIDTH)
    r.osc[...] = r.exts[:, CONV_HIST + length - (SC_CONV - 1):CONV_HIST + length, :]
    cconv = _causal_conv(r.exts, r.scw, slice(0, SC_WIDTH), length)
    r.cat_a[:, POOL_WIDTH:] = (sc_b * cconv.reshape(rows, SC_WIDTH)).astype(BF16)

    st.xy[...] = x + _dot(r.cat_a[...], r.wout[DN_WIDTH:, :])


def _stage_b(r, st, *, nb, length, chunk, groups):
    rows = nb * length
    grows = rows // groups
    chains = groups * DN_HEADS
    log2c = int(math.log2(chunk))

    tinv = []
    yield from _unit_lower_inverses([st.na[c] for c in range(chains)], log2c, tinv)
    for c in range(chains):
        nsol = _dot(tinv[c], st.rhs[c])
        r.nu[c] = nsol[:, :DN_HEAD_DIM]
        r.nw[c] = nsol[:, DN_HEAD_DIM:]
        if (c + 1) % DN_HEADS == 0:
            yield 0.5

    if grows == chunk:
        per_seq = groups // nb
        assert groups == nb * per_seq and DN_HEADS % 2 == 0
        zero = jnp.zeros((DN_HEAD_DIM, DN_HEAD_DIM), BF16)
        order = [(s_ * per_seq + t_, h_) for t_ in range(per_seq) for s_ in range(nb)
                 for h_ in range(0, DN_HEADS, 2)]
        for g, h0 in order:
            c0 = g * DN_HEADS + h0
            seq = g // per_seq
            pair = ((c0, h0, slice(0, DN_HEAD_DIM)), (c0 + 1, h0 + 1, slice(DN_HEAD_DIM, 2 * DN_HEAD_DIM)))
            s_pair = [r.od[seq, hh] for _, hh, _ in pair]
            s2 = jnp.concatenate([jnp.concatenate([s_pair[0].astype(BF16), zero], axis=1),
                                  jnp.concatenate([zero, s_pair[1].astype(BF16)], axis=1)], axis=0)
            nwq = jnp.concatenate([jnp.concatenate([r.nw[c] for c, _, _ in pair], axis=1),
                                   jnp.concatenate([st.qg[c] for c, _, _ in pair], axis=1)],
                                  axis=0).astype(BF16)
            res = _dot(nwq, s2)
            for (c, hh, lanes), s in zip(pair, s_pair):
                vn16 = (res[:grows, lanes] - r.nu[c]).astype(BF16)
                r.o[g * grows:(g + 1) * grows, _head_slice(hh)] = res[grows:, lanes] + _dot(st.qk[c], vn16)
                r.od[seq, hh] = s * st.egl[c, 0:1, :] + _dot(st.kdt[c].astype(BF16), vn16)
            yield 0.6
    else:
        assert groups == 1 and rows // chunk == nb
        seg = lax.broadcasted_iota(jnp.int32, (nb, 1, rows), 2) >> log2c
        own = seg == lax.broadcasted_iota(jnp.int32, (nb, 1, rows), 0)
        for hh in range(DN_HEADS):
            vn_parts, qs_parts = [], []
            for b in range(nb):
                rs = slice(b * chunk, (b + 1) * chunk)
                nwq = jnp.concatenate([r.nw[hh, rs, :], st.qg[hh, rs, :]], axis=0).astype(BF16)
                res = _dot(nwq, r.od[b, hh].astype(BF16))
                vn_parts.append(res[:chunk] - r.nu[hh, rs, :])
                qs_parts.append(res[chunk:])
            vn16 = jnp.concatenate(vn_parts, axis=0).astype(BF16)
            r.o[:, _head_slice(hh)] = jnp.concatenate(qs_parts, axis=0) + _dot(st.qk[hh], vn16)
            kd_own = jnp.where(own, st.kdt[hh][None], 0.0).astype(BF16).reshape(nb * DN_HEAD_DIM, rows)
            upd = _dot(kd_own, vn16).reshape(nb, DN_HEAD_DIM, DN_HEAD_DIM)
            egl = st.egl[hh].reshape(nb, chunk, DN_HEAD_DIM)[:, 0:1, :]
            r.od[:, hh, :, :] = r.od[:, hh, :, :] * egl + upd
            yield 0.3

    for hh in range(DN_HEADS):
        hs = _head_slice(hh)
        r.cat_b[:, hs] = (_rms_scale(r.o[:, hs]) * r.onorm[...] * _silu(st.z[:, hs])).astype(BF16)
    y = st.xy[...] + _dot(r.cat_b[...], r.wout[:DN_WIDTH, :])
    r.xo[...] = y.reshape(nb, length, D_MODEL)


_MIXER_IN = ("x", "sd", "sdc", "sp", "ssc", "nmix", "wgt", "wqkvz", "wrest", "dnw", "alog", "dtb", "onorm", "pw",
             "pscale", "scw", "wout")
_MIXER_OUT = ("xo", "od", "odc", "op", "osc")
_MIXER_PRIVATE = ("extd", "extp", "s2", "s4", "s8", "exts", "cat_a", "nu", "nw", "o", "cat_b")
_MIXER_SET = ("na", "qk", "rhs", "qg", "kdt", "egl", "z", "xy")


def _mixer_scratch(nb, length, chunk, n_sets, groups):
    rows = nb * length
    grows = rows // groups
    chains = groups * DN_HEADS
    one_chunk = grows == chunk
    private = [
        pltpu.VMEM((nb, CONV_HIST + length, 3 * DN_WIDTH), F32),
        pltpu.VMEM((nb, POOL_HIST + length, POOL_WIDTH), F32),
        pltpu.VMEM((nb, POOL_HIST + length, POOL_WIDTH), F32),
        pltpu.VMEM((nb, POOL_HIST + length, POOL_WIDTH), F32),
        pltpu.VMEM((nb, POOL_HIST + length, POOL_WIDTH), F32),
        pltpu.VMEM((nb, CONV_HIST + length, SC_WIDTH), F32),
        pltpu.VMEM((rows, D_REST), BF16),
        pltpu.VMEM((chains, grows, DN_HEAD_DIM), F32),
        pltpu.VMEM((chains, grows, DN_HEAD_DIM), F32),
        pltpu.VMEM((rows, DN_WIDTH), F32),
        pltpu.VMEM((rows, DN_WIDTH), BF16),
    ]
    one_set = [
        pltpu.VMEM((chains, grows, grows), BF16),
        pltpu.VMEM((chains, grows, grows), BF16),
        pltpu.VMEM((chains, grows, 2 * DN_HEAD_DIM), BF16),
        pltpu.VMEM((chains, grows, DN_HEAD_DIM), F32),
        pltpu.VMEM((chains, DN_HEAD_DIM, grows), F32),
        pltpu.VMEM((chains, SUBLANES if one_chunk else grows, DN_HEAD_DIM), F32),
        pltpu.VMEM((rows, DN_WIDTH), F32),
        pltpu.VMEM((rows, D_MODEL), F32),
    ]
    return private + one_set * n_sets


def _zero_other_layers(ref, layer):
    for l in range(ref.shape[0]):
        if l != layer:
            ref[l] = jnp.zeros(ref.shape[1:], F32)


def _mixer_kernel(*refs, nb, length, chunk, pos0, n_tiles, n_alias, layer, whole, groups):
    n_in, n_set = len(_MIXER_IN), len(_MIXER_SET)
    refs = refs[:n_in] + refs[n_in + n_alias:]
    names = _MIXER_IN + _MIXER_OUT + _MIXER_PRIVATE
    r = types.SimpleNamespace(**dict(zip(names, refs[:len(names)])))
    set_refs = refs[len(names):]
    sets = [types.SimpleNamespace(**dict(zip(_MIXER_SET, set_refs[i * n_set:(i + 1) * n_set])))
            for i in range(len(set_refs) // n_set)]
    j = pl.program_id(1)
    whole_refs = (r.od, r.odc, r.op, r.osc)
    if whole:
        r.od, r.odc, r.op, r.osc = (f.at[layer] for f in whole_refs)

    def init_state():
        if whole:
            for f in whole_refs:
                _zero_other_layers(f, layer)
        r.od[...] = r.sd[...]

    _carry_history(j, length,
                   [(r.extd, r.sdc, CONV_HIST, DN_CONV - 1), (r.extp, r.sp, POOL_HIST, POOL_BUF),
                    (r.exts, r.ssc, CONV_HIST, SC_CONV - 1)], first_tile=init_state)

    stage_a = functools.partial(_stage_a, j, r, nb=nb, length=length, chunk=chunk, pos0=pos0, groups=groups)
    stage_b = functools.partial(_stage_b, r, nb=nb, length=length, chunk=chunk, groups=groups)

    if len(sets) == 1:
        _interleave(stage_a(sets[0]))
        _interleave(stage_b(sets[0]))
        return

    middle = (j > 0) & (j < n_tiles)

    @pl.when(j == 0)
    def _():
        _interleave(stage_a(sets[0]))

    @pl.when(middle & (j % 2 == 1))
    def _():
        _interleave(stage_b(sets[0]), stage_a(sets[1]), speeds=B_AHEAD)

    @pl.when(middle & (j % 2 == 0))
    def _():
        _interleave(stage_b(sets[1]), stage_a(sets[0]), speeds=B_AHEAD)

    @pl.when(j == n_tiles)
    def _():
        _interleave(stage_b(sets[(n_tiles - 1) % 2]))


def _ffn_kernel(x_ref, sf_ref, nffn_ref, wg_ref, fw_ref, wu_ref, wd_ref, fnorm_ref, *rest,
                nb, length, final, layer, whole):
    xo_ref, of_full, extf_s = rest[-3:]
    of_ref = of_full.at[layer] if whole else of_full
    j = pl.program_id(1)
    rows = nb * length
    _carry_history(j, length, [(extf_s, sf_ref, CONV_HIST, FFN_CONV - 1)],
                   first_tile=functools.partial(_zero_other_layers, of_full, layer) if whole else None)
    x = x_ref[...].reshape(rows, D_MODEL)
    h16 = (_rms_scale(x) * nffn_ref[...]).astype(BF16)
    extf_s[:, CONV_HIST:CONV_HIST + length, :] = _dot(h16, wg_ref[...]).reshape(nb, length, D_FF)
    of_ref[...] = extf_s[:, CONV_HIST + length - (FFN_CONV - 1):CONV_HIST + length, :]
    gate = _causal_conv(extf_s, fw_ref, slice(0, D_FF), length)
    up = _dot(h16, wu_ref[...])
    act = (_silu(gate).reshape(rows, D_FF) * up).astype(BF16)
    y = x + _dot(act, wd_ref[...])
    if final:
        y = _rms_scale(y) * fnorm_ref[...]
    xo_ref[...] = y.reshape(nb, length, D_MODEL)


def _layer_spec(arr, layer, cols=None):
    tail = arr.shape[1:] if cols is None else arr.shape[1:-1] + (cols,)
    zeros = (0,) * len(tail)
    return pl.BlockSpec((None,) + tail, lambda i, j: (layer,) + zeros, pipeline_mode=pl.Buffered(1))


def _state_spec(arr, layer, nb):
    tail = arr.shape[2:]
    zeros = (0,) * len(tail)
    return pl.BlockSpec((None, nb) + tail, lambda i, j: (layer, i) + zeros)


def _state_out_spec(shape, layer, nb, whole):
    tail = shape[2:]
    zeros = (0,) * len(tail)
    if whole:
        return pl.BlockSpec((shape[0], nb) + tail, lambda i, j: (0, i) + zeros)
    return pl.BlockSpec((None, nb) + tail, lambda i, j: (layer, i) + zeros)


def _alias_args(prev, n_in, first_out):
    if prev is None:
        return [], [], {}
    specs = [pl.BlockSpec(memory_space=pl.ANY)] * len(prev)
    return list(prev), specs, {n_in + k: first_out + k for k in range(len(prev))}


def _mixer_call(x, states, state_layer, layer, depth, w, prev, *, nb, length, chunk, pos0, groups):
    batch, seq, _ = x.shape
    sd, sdc, sp, ssc = states
    n_tiles = seq // length
    pipelined = n_tiles > 1
    if pipelined:
        last = n_tiles - 1
        grid = (batch // nb, n_tiles + 1)
        x_in_spec = pl.BlockSpec((nb, length, D_MODEL), lambda i, j: (i, jnp.minimum(j, last), 0))
        x_out_spec = pl.BlockSpec((nb, length, D_MODEL), lambda i, j: (i, jnp.maximum(j - 1, 0), 0))
    else:
        grid = (batch // nb, n_tiles)
        x_in_spec = x_out_spec = pl.BlockSpec((nb, length, D_MODEL), lambda i, j: (i, j, 0))
    weights = tuple(w[name] for name in _MIXER_IN[5:])
    out_shape = (
        jax.ShapeDtypeStruct(x.shape, F32),
        jax.ShapeDtypeStruct((depth, batch) + sd.shape[2:], F32),
        jax.ShapeDtypeStruct((depth, batch) + sdc.shape[2:], F32),
        jax.ShapeDtypeStruct((depth, batch) + sp.shape[2:], F32),
        jax.ShapeDtypeStruct((depth, batch) + ssc.shape[2:], F32),
    )
    operands = (x, sd, sdc, sp, ssc) + weights
    alias_ops, alias_specs, aliases = _alias_args(prev, len(operands), 1)
    whole = prev is None
    kern = functools.partial(_mixer_kernel, nb=nb, length=length, chunk=chunk, pos0=pos0, n_tiles=n_tiles,
                             n_alias=len(alias_ops), layer=layer, whole=whole, groups=groups)
    return pl.pallas_call(
        kern,
        grid=grid,
        in_specs=[x_in_spec, _state_spec(sd, state_layer, nb), _state_spec(sdc, state_layer, nb),
                  _state_spec(sp, state_layer, nb), _state_spec(ssc, state_layer, nb)]
                 + [_layer_spec(a, layer, W_Z + DN_WIDTH if name == "wqkvz" else None)
                    for name, a in zip(_MIXER_IN[5:], weights)] + alias_specs,
        out_specs=(x_out_spec,) + tuple(_state_out_spec(s.shape, layer, nb, whole) for s in out_shape[1:]),
        out_shape=out_shape,
        input_output_aliases=aliases,
        scratch_shapes=_mixer_scratch(nb, length, chunk, 2 if pipelined else 1, groups),
        compiler_params=pltpu.CompilerParams(
            dimension_semantics=("parallel", "arbitrary"), vmem_limit_bytes=VMEM_LIMIT_BYTES),
        name="mixer",
    )(*operands, *alias_ops)


def _ffn_call(x, sf, state_layer, layer, depth, w, prev, *, nb, length, final):
    batch, seq, _ = x.shape
    grid = (batch // nb, seq // length)
    x_spec = pl.BlockSpec((nb, length, D_MODEL), lambda i, j: (i, j, 0))
    weights = (w["nffn"], w["wg"], w["fw"], w["wu"], w["wd"])
    fnorm = w["fnorm"]
    out_shape = (jax.ShapeDtypeStruct(x.shape, F32),
                 jax.ShapeDtypeStruct((depth, batch) + sf.shape[2:], F32))
    operands = (x, sf) + weights + (fnorm,)
    alias_ops, alias_specs, aliases = _alias_args(prev, len(operands), 1)
    whole = prev is None
    kern = functools.partial(_ffn_kernel, nb=nb, length=length, final=final, layer=layer, whole=whole)
    return pl.pallas_call(
        kern,
        grid=grid,
        in_specs=[x_spec, _state_spec(sf, state_layer, nb)] + [_layer_spec(a, layer) for a in weights]
                 + [pl.BlockSpec(fnorm.shape, lambda i, j: (0, 0))] + alias_specs,
        out_specs=(x_spec, _state_out_spec(out_shape[1].shape, layer, nb, whole)),
        out_shape=out_shape,
        input_output_aliases=aliases,
        scratch_shapes=[pltpu.VMEM((nb, CONV_HIST + length, D_FF), F32)],
        compiler_params=pltpu.CompilerParams(
            dimension_semantics=("parallel", "arbitrary"), vmem_limit_bytes=VMEM_LIMIT_BYTES),
        name="ffn",
    )(*operands, *alias_ops)


def _prepare_weights(norm_mix, w_in, dn_conv_w, dn_a_log, dn_dt_bias, dn_out_norm, pool_w, pool_scale,
                     sconv_w, w_out, norm_ffn, w_ffn_gate, ffn_conv_w, w_ffn_up, w_ffn_down, final_norm):
    depth = w_in.shape[0]
    assert N_GATE == SUBLANES
    g0 = W_Z + DN_WIDTH
    rest0 = g0 + N_GATE
    w16 = w_in.astype(BF16)
    wgt = jnp.swapaxes(w16[:, :, g0:g0 + N_GATE], 1, 2)
    pad = jnp.zeros((depth, SUBLANES - DN_HEADS, 1), F32)
    eye = jnp.eye(len(POOL_WINDOWS), dtype=F32)
    pw = (pool_w[:, :, :, None, :] * eye[None, :, None, :, None]).reshape(depth, POOL_WIDTH, POOL_WIDTH)
    return {
        "nmix": norm_mix[:, None, :],
        "wgt": wgt,
        "wqkvz": w16,
        "wrest": w16[:, :, rest0:rest0 + REST_WIDTH],
        "dnw": dn_conv_w,
        "alog": jnp.concatenate([dn_a_log[:, :, None], pad], axis=1),
        "dtb": jnp.concatenate([dn_dt_bias[:, :, None], pad], axis=1),
        "onorm": dn_out_norm[:, None, :],
        "pw": pw.astype(BF16),
        "pscale": pool_scale[:, None, :],
        "scw": sconv_w,
        "wout": w_out.astype(BF16),
        "nffn": norm_ffn[:, None, :],
        "wg": w_ffn_gate.astype(BF16),
        "fw": ffn_conv_w,
        "wu": w_ffn_up.astype(BF16),
        "wd": w_ffn_down.astype(BF16),
        "fnorm": final_norm[None, :],
    }


def _trunk(x, states, state_layers, weights, depth, *, nb_mix, groups, nb_ffn, length, ffn_length, chunk,
           pos0):
    sd, sdc, sp, ssc, sf = states
    mix_out, ffn_out = None, None
    for l in range(depth):
        sl = state_layers[l]
        x, *mix_out = _mixer_call(x, (sd, sdc, sp, ssc), sl, l, depth, weights, mix_out,
                                  nb=nb_mix, length=length, chunk=chunk, pos0=pos0, groups=groups)
        x, *ffn_out = _ffn_call(x, sf, sl, l, depth, weights, ffn_out,
                                nb=nb_ffn, length=ffn_length, final=(l == depth - 1))
    return x, tuple(mix_out) + tuple(ffn_out)


PROMPT_TILE = 256
PROMPT_MIX_ROWS = 2
PROMPT_CHUNK = 128
PROMPT_FFN_TILE = 512
SAMPLE_MIX_ROWS = 16
SAMPLE_FFN_ROWS = 64


def kernel(x_prompt, x_sample, state_delta, state_delta_conv, state_pool, state_sconv, state_ffn_conv,
           norm_mix, w_in, dn_conv_w, dn_a_log, dn_dt_bias, dn_out_norm, pool_w, pool_scale,
           sconv_w, w_out, norm_ffn, w_ffn_gate, ffn_conv_w, w_ffn_up, w_ffn_down, final_norm):
    depth = w_in.shape[0]
    weights = _prepare_weights(norm_mix, w_in, dn_conv_w, dn_a_log, dn_dt_bias, dn_out_norm, pool_w,
                               pool_scale, sconv_w, w_out, norm_ffn, w_ffn_gate, ffn_conv_w, w_ffn_up,
                               w_ffn_down, final_norm)

    batch, seq, _ = x_prompt.shape
    dec_batch, dec_seq, _ = x_sample.shape
    sample_states = (state_delta, state_delta_conv, state_pool, state_sconv, state_ffn_conv)
    zero_states = tuple(jnp.zeros((1, batch) + s.shape[2:], F32) for s in sample_states)

    tile, chunk = min(PROMPT_TILE, seq), min(PROMPT_CHUNK, seq)
    y_prompt, p_st = _trunk(x_prompt, zero_states, (0,) * depth, weights, depth,
                            nb_mix=PROMPT_MIX_ROWS, groups=PROMPT_MIX_ROWS * (tile // chunk), nb_ffn=1,
                            length=tile, ffn_length=min(PROMPT_FFN_TILE, seq), chunk=chunk, pos0=0)
    y_sample, s_st = _trunk(x_sample, sample_states, tuple(range(depth)), weights, depth,
                            nb_mix=SAMPLE_MIX_ROWS, groups=1, nb_ffn=SAMPLE_FFN_ROWS,
                            length=dec_seq, ffn_length=dec_seq,
                            chunk=min(DN_CHUNK, dec_seq), pos0=PAST_LEN)
    return (y_prompt, y_sample) + p_st + s_st
```

```python
import functools
import math
import types

import jax
import jax.numpy as jnp
from jax import lax
from jax.experimental import pallas as pl
from jax.experimental.pallas import tpu as pltpu

F32 = jnp.float32
BF16 = jnp.bfloat16

LANES = 128
SUBLANES = 8
BF16_ROWS = 16
VMEM_LIMIT_BYTES = 56 * 1024 * 1024

D_MODEL = 1024
DN_HEADS = 4
DN_HEAD_DIM = 128
DN_WIDTH = DN_HEADS * DN_HEAD_DIM
DN_CONV = 4
DN_CHUNK = 64
POOL_WINDOWS = (2, 4, 8, 16)
POOL_WIDTH = 256
POOL_GROUP_DIM = POOL_WIDTH // len(POOL_WINDOWS)
POOL_BUF = max(POOL_WINDOWS) - 1
SC_WIDTH = 256
SC_CONV = 3
D_FF = 2816
FFN_CONV = 3
EPS = 1e-6
PAST_LEN = 16384

N_GATE = 2 * DN_HEADS
W_Z = 3 * DN_WIDTH
REST_WIDTH = POOL_WIDTH + 3 * SC_WIDTH
D_REST = POOL_WIDTH + SC_WIDTH

FFN_COLS = 768
CONV_HIST = SUBLANES
POOL_HIST = 24

L_GC, L_BETA, L_EGC, L_KDF, L_EGL = 0, DN_HEADS, SUBLANES, 2 * SUBLANES, 3 * SUBLANES


def _silu(t):
    return t / (1.0 + jnp.exp(-t))


def _softplus(t):
    return jnp.maximum(t, 0.0) + jnp.log1p(jnp.exp(-jnp.abs(t)))


def _rms_scale(t):
    return t * lax.rsqrt(jnp.mean(t * t, axis=-1, keepdims=True) + EPS)


def _dot(a, b):
    return jnp.dot(a, b, preferred_element_type=F32)


def _dot_nt(a, b):
    return lax.dot_general(a, b, (((1,), (1,)), ((), ())), preferred_element_type=F32)


def _head_slice(hh):
    return slice(hh * DN_HEAD_DIM, (hh + 1) * DN_HEAD_DIM)


B_AHEAD = (4.0, 1.0)


def _interleave(*stages, speeds=None):
    speeds = speeds or [1.0] * len(stages)
    clock = [0.0] * len(stages)
    live = list(range(len(stages)))
    while live:
        i = min(live, key=lambda s: clock[s])
        try:
            clock[i] += next(stages[i]) / speeds[i]
        except StopIteration:
            live.remove(i)


def _carry_history(j, length, items, first_tile=None):
    @pl.when(j == 0)
    def _():
        if first_tile is not None:
            first_tile()
        for ext, state_ref, hist, nbuf in items:
            ext[:, 0:hist - nbuf, :] = jnp.zeros((ext.shape[0], hist - nbuf, ext.shape[2]), F32)
            ext[:, hist - nbuf:hist, :] = state_ref[...]

    @pl.when(j != 0)
    def _():
        for ext, state_ref, hist, nbuf in items:
            ext[:, hist - nbuf:hist, :] = ext[:, hist + length - nbuf:hist + length, :]


def _causal_conv(ext, w_ref, cs, length, bs=slice(None)):
    taps = w_ref.shape[0]
    tiles = length // SUBLANES
    e4 = ext[bs, 0:CONV_HIST + length, cs]
    nb, _, width = e4.shape
    e4 = e4.reshape(nb, tiles + 1, SUBLANES, width)
    sub4 = lax.broadcasted_iota(jnp.int32, (nb, tiles, SUBLANES, width), 2)
    acc = None
    for i in range(taps):
        s = taps - 1 - i
        if s:
            rot = pltpu.roll(e4, s, axis=2)
            term = jnp.where(sub4 >= s, rot[:, 1:], rot[:, :-1])
        else:
            term = e4[:, 1:]
        term = term * w_ref[i:i + 1, cs][None, None]
        acc = term if acc is None else acc + term
    return acc.reshape(nb, length, width)


def _unit_lower_inverses(na16, log2c, out):
    heads = range(len(na16))
    n = na16[0].shape[0]

    def spread(fn):
        vals = []
        for i in heads:
            vals.append(fn(i))
            if (i + 1) % DN_HEADS == 0:
                yield 0.3
        return vals

    row = lax.broadcasted_iota(jnp.int32, (n, n), 0)
    col = lax.broadcasted_iota(jnp.int32, (n, n), 1)
    pair = ((row >> 1) == (col >> 1)) & ((row & 1) == 1) & ((col & 1) == 0)
    eye = jnp.where(row == col, 1.0, 0.0).astype(F32)
    t16 = yield from spread(lambda i: jnp.where(pair, na16[i].astype(F32), eye).astype(BF16))
    for lvl in range(1, log2c):
        blk = 1 << lvl
        if blk < BF16_ROWS:
            off = (((row >> (lvl + 1)) == (col >> (lvl + 1)))
                   & (((row >> lvl) & 1) == 1) & (((col >> lvl) & 1) == 0))
            p16 = yield from spread(lambda i: _dot(na16[i], t16[i]).astype(BF16))
            t16 = yield from spread(
                lambda i: t16[i] + jnp.where(off, _dot(t16[i], p16[i]), 0.0).astype(BF16))
        else:
            pairs = n // (2 * blk)
            rowh = lax.broadcasted_iota(jnp.int32, (n // 2, n), 0)
            colh = lax.broadcasted_iota(jnp.int32, (n // 2, n), 1)
            offh = ((colh >> (lvl + 1)) == (rowh >> lvl)) & (((colh >> lvl) & 1) == 0)
            t4 = [t16[i].reshape(pairs, 2, blk, n) for i in heads]
            na_odd = [na16[i].reshape(pairs, 2, blk, n)[:, 1].reshape(n // 2, n) for i in heads]
            t_odd = [t4[i][:, 1].reshape(n // 2, n) for i in heads]
            p16 = yield from spread(
                lambda i: jnp.broadcast_to(_dot(na_odd[i], t16[i]).reshape(pairs, 1, blk, n),
                                           (pairs, 2, blk, n)).reshape(n, n).astype(BF16))
            x_odd = yield from spread(
                lambda i: jnp.where(offh, _dot(t_odd[i], p16[i]), 0.0).astype(BF16))
            t16 = [jnp.concatenate([t4[i][:, 0:1], (t_odd[i] + x_odd[i]).reshape(pairs, 1, blk, n)],
                                   axis=1).reshape(n, n) for i in heads]
    out.extend(t16)


def _stage_a(j, r, st, *, nb, length, chunk, pos0, groups):
    rows = nb * length
    grows = rows // groups
    log2c = int(math.log2(chunk))

    x = r.x[...].reshape(rows, D_MODEL)
    h16 = (_rms_scale(x) * r.nmix[...]).astype(BF16)

    gates_t = _dot_nt(r.wgt[...], h16)
    g_t = -jnp.exp(r.alog[...]) * _softplus(gates_t + r.dtb[...])
    beta_t = 1.0 / (1.0 + jnp.exp(-gates_t))
    pos_in_chunk = lax.broadcasted_iota(jnp.int32, (SUBLANES, rows), 1) & (chunk - 1)
    gc_t = g_t
    sfx_t = g_t
    step = 1
    while step < chunk:
        gc_t = gc_t + jnp.where(pos_in_chunk >= step, pltpu.roll(gc_t, step, axis=1), 0.0)
        sfx_t = sfx_t + jnp.where(pos_in_chunk + step < chunk,
                                  pltpu.roll(sfx_t, rows - step, axis=1), 0.0)
        step *= 2
    rest_t = sfx_t - g_t
    sub = lax.broadcasted_iota(jnp.int32, (SUBLANES, rows), 0)
    cols = jnp.concatenate(
        [jnp.where(sub < DN_HEADS, gc_t, -beta_t), jnp.exp(gc_t), jnp.exp(rest_t), jnp.exp(gc_t + rest_t),
         jnp.zeros((LANES - 4 * SUBLANES, rows), F32)], axis=0).T

    def colb(lane, g, width=LANES):
        return jnp.broadcast_to(cols[g * grows:(g + 1) * grows, lane:lane + 1], (grows, width))

    yield 1.0
    for part in range(3):
        cs = slice(part * DN_WIDTH, (part + 1) * DN_WIDTH)
        r.extd[:, CONV_HIST:CONV_HIST + length, cs] = _dot(h16, r.wqkvz[:, cs]).reshape(nb, length, DN_WIDTH)
        yield 0.3
    r.odc[...] = r.extd[:, CONV_HIST + length - (DN_CONV - 1):CONV_HIST + length, :]

    units = nb if grows == chunk else 1
    unb, urows = nb // units, rows // units
    per_unit = groups // units

    def conv_block(ci, u):
        cs = slice(ci * LANES, (ci + 1) * LANES)
        bs = slice(u * unb, (u + 1) * unb)
        return _silu(_causal_conv(r.extd, r.dnw, cs, length, bs)).reshape(urows, LANES)

    row = lax.broadcasted_iota(jnp.int32, (grows, grows), 0)
    col = lax.broadcasted_iota(jnp.int32, (grows, grows), 1)
    same_chunk = (row >> log2c) == (col >> log2c)
    m_incl = same_chunk & (row >= col)
    m_strict = same_chunk & (row > col)

    for u in range(units):
        for hh in range(DN_HEADS):
            qc = conv_block(hh, u)
            yield 0.2 * per_unit
            kc = conv_block(DN_HEADS + hh, u)
            yield 0.2 * per_unit
            vc_u = conv_block(2 * DN_HEADS + hh, u)
            q_u = qc * (lax.rsqrt(jnp.sum(qc * qc, axis=-1, keepdims=True) + EPS) * (DN_HEAD_DIM ** -0.5))
            k_u = kc * lax.rsqrt(jnp.sum(kc * kc, axis=-1, keepdims=True) + EPS)
            yield 0.2 * per_unit
            for t in range(per_unit):
                g = u * per_unit + t
                c = g * DN_HEADS + hh
                rs = slice(t * grows, (t + 1) * grows)
                q, k, vc = q_u[rs], k_u[rs], vc_u[rs]
                nbeta_b = colb(L_BETA + hh, g)
                egc_b = colb(L_EGC + hh, g)
                nkb = k * nbeta_b
                k16 = k.astype(BF16)
                diff = colb(L_GC + hh, g, grows) - gc_t[hh:hh + 1, g * grows:(g + 1) * grows]
                decay = jnp.where(m_incl, jnp.exp(diff), 0.0)
                na = jnp.where(m_strict, _dot_nt(nkb.astype(BF16), k16) * decay, 0.0)
                st.na[c] = na.astype(BF16)
                yield 0.25
                st.qk[c] = (_dot_nt(q.astype(BF16), k16) * decay).astype(BF16)
                st.rhs[c] = jnp.concatenate([vc * nbeta_b, nkb * egc_b], axis=1).astype(BF16)
                st.qg[c] = q * egc_b
                st.kdt[c] = (k * colb(L_KDF + hh, g)).T
                st.egl[c] = colb(L_EGL + hh, g)[0:st.egl.shape[1]]
                yield 0.25

    prest = _dot(h16, r.wrest[...])
    st.z[...] = _dot(h16, r.wqkvz[:, W_Z:W_Z + DN_WIDTH])
    p_in = prest[:, 0:POOL_WIDTH]
    sc_x = prest[:, POOL_WIDTH:POOL_WIDTH + SC_WIDTH]
    sc_b = prest[:, POOL_WIDTH + SC_WIDTH:POOL_WIDTH + 2 * SC_WIDTH]
    sc_c = prest[:, POOL_WIDTH + 2 * SC_WIDTH:]
    yield 0.5

    ph = POOL_HIST
    r.extp[:, ph:ph + length, :] = p_in.reshape(nb, length, POOL_WIDTH)
    r.op[...] = r.extp[:, ph + length - POOL_BUF:ph + length, :]
    zero8 = jnp.zeros((nb, SUBLANES, POOL_WIDTH), F32)
    end = ph + length
    r.s2[:, 0:8, :] = zero8
    r.s4[:, 0:8, :] = zero8
    r.s8[:, 0:8, :] = zero8
    r.s2[:, 8:end, :] = r.extp[:, 8:end, :] + r.extp[:, 7:end - 1, :]
    r.s4[:, 8:end, :] = r.s2[:, 8:end, :] + r.s2[:, 6:end - 2, :]
    r.s8[:, 8:end, :] = r.s4[:, 8:end, :] + r.s4[:, 4:end - 4, :]
    s16 = r.s8[:, ph:end, :] + r.s8[:, ph - 8:end - 8, :]
    lane3 = lax.broadcasted_iota(jnp.int32, (nb, length, POOL_WIDTH), 2)
    grp = lane3 // POOL_GROUP_DIM
    wsum = jnp.where(grp == 0, r.s2[:, ph:end, :],
                     jnp.where(grp == 1, r.s4[:, ph:end, :],
                               jnp.where(grp == 2, r.s8[:, ph:end, :], s16)))
    win = jnp.where(grp == 0, POOL_WINDOWS[0],
                    jnp.where(grp == 1, POOL_WINDOWS[1],
                              jnp.where(grp == 2, POOL_WINDOWS[2], POOL_WINDOWS[3])))
    tpos = lax.broadcasted_iota(jnp.int32, (nb, length, POOL_WIDTH), 1) + (pos0 + j * length)
    cnt = jnp.minimum(tpos + 1, win).astype(F32)
    dpool = (wsum / cnt - r.extp[:, ph:end, :]).reshape(rows, POOL_WIDTH)
    r.cat_a[:, 0:POOL_WIDTH] = (_dot(dpool.astype(BF16), r.pw[...]) * r.pscale[...]).astype(BF16)
    yield 0.4

    r.exts[:, CONV_HIST:CONV_HIST + length, :] = (sc_c * sc_x).reshape(nb, length, SC_WIDTH)
    r.osc[...] = r.exts[:, CONV_HIST + length - (SC_CONV - 1):CONV_HIST + length, :]
    cconv = _causal_conv(r.exts, r.scw, slice(0, SC_WIDTH), length)
    r.cat_a[:, POOL_WIDTH:] = (sc_b * cconv.reshape(rows, SC_WIDTH)).astype(BF16)

    st.xy[...] = x + _dot(r.cat_a[...], r.wout[DN_WIDTH:, :])


def _stage_b(r, st, *, nb, length, chunk, groups):
    rows = nb * length
    grows = rows // groups
    chains = groups * DN_HEADS
    log2c = int(math.log2(chunk))

    tinv = []
    yield from _unit_lower_inverses([st.na[c] for c in range(chains)], log2c, tinv)
    for c in range(chains):
        nsol = _dot(tinv[c], st.rhs[c])
        r.nu[c] = nsol[:, :DN_HEAD_DIM]
        r.nw[c] = nsol[:, DN_HEAD_DIM:]
        if (c + 1) % DN_HEADS == 0:
            yield 0.5

    if grows == chunk:
        per_seq = groups // nb
        assert groups == nb * per_seq and DN_HEADS % 2 == 0
        zero = jnp.zeros((DN_HEAD_DIM, DN_HEAD_DIM), BF16)
        order = [(s_ * per_seq + t_, h_) for t_ in range(per_seq) for s_ in range(nb)
                 for h_ in range(0, DN_HEADS, 2)]
        for g, h0 in order:
            c0 = g * DN_HEADS + h0
            seq = g // per_seq
            pair = ((c0, h0, slice(0, DN_HEAD_DIM)), (c0 + 1, h0 + 1, slice(DN_HEAD_DIM, 2 * DN_HEAD_DIM)))
            s_pair = [r.od[seq, hh] for _, hh, _ in pair]
            s2 = jnp.concatenate([jnp.concatenate([s_pair[0].astype(BF16), zero], axis=1),
                                  jnp.concatenate([zero, s_pair[1].astype(BF16)], axis=1)], axis=0)
            nwq = jnp.concatenate([jnp.concatenate([r.nw[c] for c, _, _ in pair], axis=1),
                                   jnp.concatenate([st.qg[c] for c, _, _ in pair], axis=1)],
                                  axis=0).astype(BF16)
            res = _dot(nwq, s2)
            for (c, hh, lanes), s in zip(pair, s_pair):
                vn16 = (res[:grows, lanes] - r.nu[c]).astype(BF16)
                r.o[g * grows:(g + 1) * grows, _head_slice(hh)] = res[grows:, lanes] + _dot(st.qk[c], vn16)
                r.od[seq, hh] = s * st.egl[c, 0:1, :] + _dot(st.kdt[c].astype(BF16), vn16)
            yield 0.6
    else:
        assert groups == 1 and rows // chunk == nb
        seg = lax.broadcasted_iota(jnp.int32, (nb, 1, rows), 2) >> log2c
        own = seg == lax.broadcasted_iota(jnp.int32, (nb, 1, rows), 0)
        for hh in range(DN_HEADS):
            vn_parts, qs_parts = [], []
            for b in range(nb):
                rs = slice(b * chunk, (b + 1) * chunk)
                nwq = jnp.concatenate([r.nw[hh, rs, :], st.qg[hh, rs, :]], axis=0).astype(BF16)
                res = _dot(nwq, r.od[b, hh].astype(BF16))
                vn_parts.append(res[:chunk] - r.nu[hh, rs, :])
                qs_parts.append(res[chunk:])
            vn16 = jnp.concatenate(vn_parts, axis=0).astype(BF16)
            r.o[:, _head_slice(hh)] = jnp.concatenate(qs_parts, axis=0) + _dot(st.qk[hh], vn16)
            kd_own = jnp.where(own, st.kdt[hh][None], 0.0).astype(BF16).reshape(nb * DN_HEAD_DIM, rows)
            upd = _dot(kd_own, vn16).reshape(nb, DN_HEAD_DIM, DN_HEAD_DIM)
            egl = st.egl[hh].reshape(nb, chunk, DN_HEAD_DIM)[:, 0:1, :]
            r.od[:, hh, :, :] = r.od[:, hh, :, :] * egl + upd
            yield 0.3

    for hh in range(DN_HEADS):
        hs = _head_slice(hh)
        r.cat_b[:, hs] = (_rms_scale(r.o[:, hs]) * r.onorm[...] * _silu(st.z[:, hs])).astype(BF16)
    y = st.xy[...] + _dot(r.cat_b[...], r.wout[:DN_WIDTH, :])
    r.xo[...] = y.reshape(nb, length, D_MODEL)


_MIXER_IN = ("x", "sd", "sdc", "sp", "ssc", "nmix", "wgt", "wqkvz", "wrest", "dnw", "alog", "dtb", "onorm", "pw",
             "pscale", "scw", "wout")
_MIXER_OUT = ("xo", "od", "odc", "op", "osc")
_MIXER_PRIVATE = ("extd", "extp", "s2", "s4", "s8", "exts", "cat_a", "nu", "nw", "o", "cat_b")
_MIXER_SET = ("na", "qk", "rhs", "qg", "kdt", "egl", "z", "xy")


def _mixer_scratch(nb, length, chunk, n_sets, groups):
    rows = nb * length
    grows = rows // groups
    chains = groups * DN_HEADS
    one_chunk = grows == chunk
    private = [
        pltpu.VMEM((nb, CONV_HIST + length, 3 * DN_WIDTH), F32),
        pltpu.VMEM((nb, POOL_HIST + length, POOL_WIDTH), F32),
        pltpu.VMEM((nb, POOL_HIST + length, POOL_WIDTH), F32),
        pltpu.VMEM((nb, POOL_HIST + length, POOL_WIDTH), F32),
        pltpu.VMEM((nb, POOL_HIST + length, POOL_WIDTH), F32),
        pltpu.VMEM((nb, CONV_HIST + length, SC_WIDTH), F32),
        pltpu.VMEM((rows, D_REST), BF16),
        pltpu.VMEM((chains, grows, DN_HEAD_DIM), F32),
        pltpu.VMEM((chains, grows, DN_HEAD_DIM), F32),
        pltpu.VMEM((rows, DN_WIDTH), F32),
        pltpu.VMEM((rows, DN_WIDTH), BF16),
    ]
    one_set = [
        pltpu.VMEM((chains, grows, grows), BF16),
        pltpu.VMEM((chains, grows, grows), BF16),
        pltpu.VMEM((chains, grows, 2 * DN_HEAD_DIM), BF16),
        pltpu.VMEM((chains, grows, DN_HEAD_DIM), F32),
        pltpu.VMEM((chains, DN_HEAD_DIM, grows), F32),
        pltpu.VMEM((chains, SUBLANES if one_chunk else grows, DN_HEAD_DIM), F32),
        pltpu.VMEM((rows, DN_WIDTH), F32),
        pltpu.VMEM((rows, D_MODEL), F32),
    ]
    return private + one_set * n_sets


def _zero_other_layers(ref, layer):
    for l in range(ref.shape[0]):
        if l != layer:
            ref[l] = jnp.zeros(ref.shape[1:], F32)


def _mixer_kernel(*refs, nb, length, chunk, pos0, n_tiles, n_alias, layer, whole, groups):
    n_in, n_set = len(_MIXER_IN), len(_MIXER_SET)
    refs = refs[:n_in] + refs[n_in + n_alias:]
    names = _MIXER_IN + _MIXER_OUT + _MIXER_PRIVATE
    r = types.SimpleNamespace(**dict(zip(names, refs[:len(names)])))
    set_refs = refs[len(names):]
    sets = [types.SimpleNamespace(**dict(zip(_MIXER_SET, set_refs[i * n_set:(i + 1) * n_set])))
            for i in range(len(set_refs) // n_set)]
    j = pl.program_id(1)
    whole_refs = (r.od, r.odc, r.op, r.osc)
    if whole:
        r.od, r.odc, r.op, r.osc = (f.at[layer] for f in whole_refs)

    def init_state():
        if whole:
            for f in whole_refs:
                _zero_other_layers(f, layer)
        r.od[...] = r.sd[...]

    _carry_history(j, length,
                   [(r.extd, r.sdc, CONV_HIST, DN_CONV - 1), (r.extp, r.sp, POOL_HIST, POOL_BUF),
                    (r.exts, r.ssc, CONV_HIST, SC_CONV - 1)], first_tile=init_state)

    stage_a = functools.partial(_stage_a, j, r, nb=nb, length=length, chunk=chunk, pos0=pos0, groups=groups)
    stage_b = functools.partial(_stage_b, r, nb=nb, length=length, chunk=chunk, groups=groups)

    if len(sets) == 1:
        _interleave(stage_a(sets[0]))
        _interleave(stage_b(sets[0]))
        return

    middle = (j > 0) & (j < n_tiles)

    @pl.when(j == 0)
    def _():
        _interleave(stage_a(sets[0]))

    @pl.when(middle & (j % 2 == 1))
    def _():
        _interleave(stage_b(sets[0]), stage_a(sets[1]), speeds=B_AHEAD)

    @pl.when(middle & (j % 2 == 0))
    def _():
        _interleave(stage_b(sets[1]), stage_a(sets[0]), speeds=B_AHEAD)

    @pl.when(j == n_tiles)
    def _():
        _interleave(stage_b(sets[(n_tiles - 1) % 2]))


def _ffn_kernel(x_ref, sf_ref, nffn_ref, wg_ref, fw_ref, wu_ref, wd_ref, fnorm_ref, *rest,
                nb, length, final, layer, whole):
    xo_ref, of_full, extf_s = rest[-3:]
    of_ref = of_full.at[layer] if whole else of_full
    j = pl.program_id(1)
    rows = nb * length
    _carry_history(j, length, [(extf_s, sf_ref, CONV_HIST, FFN_CONV - 1)],
                   first_tile=functools.partial(_zero_other_layers, of_full, layer) if whole else None)
    x = x_ref[...].reshape(rows, D_MODEL)
    h16 = (_rms_scale(x) * nffn_ref[...]).astype(BF16)
    y = x
    for c0 in range(0, D_FF, FFN_COLS):
        cs = slice(c0, min(c0 + FFN_COLS, D_FF))
        width = cs.stop - cs.start
        extf_s[:, CONV_HIST:CONV_HIST + length, cs] = _dot(h16, wg_ref[:, cs]).reshape(nb, length, width)
        gate = _causal_conv(extf_s, fw_ref, cs, length)
        up = _dot(h16, wu_ref[:, cs])
        act = (_silu(gate).reshape(rows, width) * up).astype(BF16)
        y = y + _dot(act, wd_ref[cs, :])
    of_ref[...] = extf_s[:, CONV_HIST + length - (FFN_CONV - 1):CONV_HIST + length, :]
    if final:
        y = _rms_scale(y) * fnorm_ref[...]
    xo_ref[...] = y.reshape(nb, length, D_MODEL)


def _layer_spec(arr, layer, cols=None):
    tail = arr.shape[1:] if cols is None else arr.shape[1:-1] + (cols,)
    zeros = (0,) * len(tail)
    return pl.BlockSpec((None,) + tail, lambda i, j: (layer,) + zeros, pipeline_mode=pl.Buffered(1))


def _state_spec(arr, layer, nb):
    tail = arr.shape[2:]
    zeros = (0,) * len(tail)
    return pl.BlockSpec((None, nb) + tail, lambda i, j: (layer, i) + zeros)


def _state_out_spec(shape, layer, nb, whole):
    tail = shape[2:]
    zeros = (0,) * len(tail)
    if whole:
        return pl.BlockSpec((shape[0], nb) + tail, lambda i, j: (0, i) + zeros)
    return pl.BlockSpec((None, nb) + tail, lambda i, j: (layer, i) + zeros)


def _alias_args(prev, n_in, first_out):
    if prev is None:
        return [], [], {}
    specs = [pl.BlockSpec(memory_space=pl.ANY)] * len(prev)
    return list(prev), specs, {n_in + k: first_out + k for k in range(len(prev))}


def _mixer_call(x, states, state_layer, layer, depth, w, prev, *, nb, length, chunk, pos0, groups):
    batch, seq, _ = x.shape
    sd, sdc, sp, ssc = states
    n_tiles = seq // length
    pipelined = n_tiles > 1
    if pipelined:
        last = n_tiles - 1
        grid = (batch // nb, n_tiles + 1)
        x_in_spec = pl.BlockSpec((nb, length, D_MODEL), lambda i, j: (i, jnp.minimum(j, last), 0))
        x_out_spec = pl.BlockSpec((nb, length, D_MODEL), lambda i, j: (i, jnp.maximum(j - 1, 0), 0))
    else:
        grid = (batch // nb, n_tiles)
        x_in_spec = x_out_spec = pl.BlockSpec((nb, length, D_MODEL), lambda i, j: (i, j, 0))
    weights = tuple(w[name] for name in _MIXER_IN[5:])
    out_shape = (
        jax.ShapeDtypeStruct(x.shape, F32),
        jax.ShapeDtypeStruct((depth, batch) + sd.shape[2:], F32),
        jax.ShapeDtypeStruct((depth, batch) + sdc.shape[2:], F32),
        jax.ShapeDtypeStruct((depth, batch) + sp.shape[2:], F32),
        jax.ShapeDtypeStruct((depth, batch) + ssc.shape[2:], F32),
    )
    operands = (x, sd, sdc, sp, ssc) + weights
    alias_ops, alias_specs, aliases = _alias_args(prev, len(operands), 1)
    whole = prev is None
    kern = functools.partial(_mixer_kernel, nb=nb, length=length, chunk=chunk, pos0=pos0, n_tiles=n_tiles,
                             n_alias=len(alias_ops), layer=layer, whole=whole, groups=groups)
    return pl.pallas_call(
        kern,
        grid=grid,
        in_specs=[x_in_spec, _state_spec(sd, state_layer, nb), _state_spec(sdc, state_layer, nb),
                  _state_spec(sp, state_layer, nb), _state_spec(ssc, state_layer, nb)]
                 + [_layer_spec(a, layer, W_Z + DN_WIDTH if name == "wqkvz" else None)
                    for name, a in zip(_MIXER_IN[5:], weights)] + alias_specs,
        out_specs=(x_out_spec,) + tuple(_state_out_spec(s.shape, layer, nb, whole) for s in out_shape[1:]),
        out_shape=out_shape,
        input_output_aliases=aliases,
        scratch_shapes=_mixer_scratch(nb, length, chunk, 2 if pipelined else 1, groups),
        compiler_params=pltpu.CompilerParams(
            dimension_semantics=("parallel", "arbitrary"), vmem_limit_bytes=VMEM_LIMIT_BYTES),
        name="mixer",
    )(*operands, *alias_ops)


def _ffn_call(x, sf, state_layer, layer, depth, w, prev, *, nb, length, final):
    batch, seq, _ = x.shape
    grid = (batch // nb, seq // length)
    x_spec = pl.BlockSpec((nb, length, D_MODEL), lambda i, j: (i, j, 0))
    weights = (w["nffn"], w["wg"], w["fw"], w["wu"], w["wd"])
    fnorm = w["fnorm"]
    out_shape = (jax.ShapeDtypeStruct(x.shape, F32),
                 jax.ShapeDtypeStruct((depth, batch) + sf.shape[2:], F32))
    operands = (x, sf) + weights + (fnorm,)
    alias_ops, alias_specs, aliases = _alias_args(prev, len(operands), 1)
    whole = prev is None
    kern = functools.partial(_ffn_kernel, nb=nb, length=length, final=final, layer=layer, whole=whole)
    return pl.pallas_call(
        kern,
        grid=grid,
        in_specs=[x_spec, _state_spec(sf, state_layer, nb)] + [_layer_spec(a, layer) for a in weights]
                 + [pl.BlockSpec(fnorm.shape, lambda i, j: (0, 0))] + alias_specs,
        out_specs=(x_spec, _state_out_spec(out_shape[1].shape, layer, nb, whole)),
        out_shape=out_shape,
        input_output_aliases=aliases,
        scratch_shapes=[pltpu.VMEM((nb, CONV_HIST + length, D_FF), F32)],
        compiler_params=pltpu.CompilerParams(
            dimension_semantics=("parallel", "arbitrary"), vmem_limit_bytes=VMEM_LIMIT_BYTES),
        name="ffn",
    )(*operands, *alias_ops)


def _prepare_weights(norm_mix, w_in, dn_conv_w, dn_a_log, dn_dt_bias, dn_out_norm, pool_w, pool_scale,
                     sconv_w, w_out, norm_ffn, w_ffn_gate, ffn_conv_w, w_ffn_up, w_ffn_down, final_norm):
    depth = w_in.shape[0]
    assert N_GATE == SUBLANES
    g0 = W_Z + DN_WIDTH
    rest0 = g0 + N_GATE
    w16 = w_in.astype(BF16)
    wgt = jnp.swapaxes(w16[:, :, g0:g0 + N_GATE], 1, 2)
    pad = jnp.zeros((depth, SUBLANES - DN_HEADS, 1), F32)
    eye = jnp.eye(len(POOL_WINDOWS), dtype=F32)
    pw = (pool_w[:, :, :, None, :] * eye[None, :, None, :, None]).reshape(depth, POOL_WIDTH, POOL_WIDTH)
    return {
        "nmix": norm_mix[:, None, :],
        "wgt": wgt,
        "wqkvz": w16,
        "wrest": w16[:, :, rest0:rest0 + REST_WIDTH],
        "dnw": dn_conv_w,
        "alog": jnp.concatenate([dn_a_log[:, :, None], pad], axis=1),
        "dtb": jnp.concatenate([dn_dt_bias[:, :, None], pad], axis=1),
        "onorm": dn_out_norm[:, None, :],
        "pw": pw.astype(BF16),
        "pscale": pool_scale[:, None, :],
        "scw": sconv_w,
        "wout": w_out.astype(BF16),
        "nffn": norm_ffn[:, None, :],
        "wg": w_ffn_gate.astype(BF16),
        "fw": ffn_conv_w,
        "wu": w_ffn_up.astype(BF16),
        "wd": w_ffn_down.astype(BF16),
        "fnorm": final_norm[None, :],
    }


def _trunk(x, states, state_layers, weights, depth, *, nb_mix, groups, nb_ffn, length, ffn_length, chunk,
           pos0):
    sd, sdc, sp, ssc, sf = states
    mix_out, ffn_out = None, None
    for l in range(depth):
        sl = state_layers[l]
        x, *mix_out = _mixer_call(x, (sd, sdc, sp, ssc), sl, l, depth, weights, mix_out,
                                  nb=nb_mix, length=length, chunk=chunk, pos0=pos0, groups=groups)
        x, *ffn_out = _ffn_call(x, sf, sl, l, depth, weights, ffn_out,
                                nb=nb_ffn, length=ffn_length, final=(l == depth - 1))
    return x, tuple(mix_out) + tuple(ffn_out)


PROMPT_TILE = 256
PROMPT_MIX_ROWS = 2
PROMPT_CHUNK = 128
PROMPT_FFN_TILE = 512
SAMPLE_MIX_ROWS = 16
SAMPLE_FFN_ROWS = 64


def kernel(x_prompt, x_sample, state_delta, state_delta_conv, state_pool, state_sconv, state_ffn_conv,
           norm_mix, w_in, dn_conv_w, dn_a_log, dn_dt_bias, dn_out_norm, pool_w, pool_scale,
           sconv_w, w_out, norm_ffn, w_ffn_gate, ffn_conv_w, w_ffn_up, w_ffn_down, final_norm):
    depth = w_in.shape[0]
    weights = _prepare_weights(norm_mix, w_in, dn_conv_w, dn_a_log, dn_dt_bias, dn_out_norm, pool_w,
                               pool_scale, sconv_w, w_out, norm_ffn, w_ffn_gate, ffn_conv_w, w_ffn_up,
                               w_ffn_down, final_norm)

    batch, seq, _ = x_prompt.shape
    dec_batch, dec_seq, _ = x_sample.shape
    sample_states = (state_delta, state_delta_conv, state_pool, state_sconv, state_ffn_conv)
    zero_states = tuple(jnp.zeros((1, batch) + s.shape[2:], F32) for s in sample_states)

    tile, chunk = min(PROMPT_TILE, seq), min(PROMPT_CHUNK, seq)
    y_prompt, p_st = _trunk(x_prompt, zero_states, (0,) * depth, weights, depth,
                            nb_mix=PROMPT_MIX_ROWS, groups=PROMPT_MIX_ROWS * (tile // chunk), nb_ffn=1,
                            length=tile, ffn_length=min(PROMPT_FFN_TILE, seq), chunk=chunk, pos0=0)
    y_sample, s_st = _trunk(x_sample, sample_states, tuple(range(depth)), weights, depth,
                            nb_mix=SAMPLE_MIX_ROWS, groups=1, nb_ffn=SAMPLE_FFN_ROWS,
                            length=dec_seq, ffn_length=dec_seq,
                            chunk=min(DN_CHUNK, dec_seq), pos0=PAST_LEN)
    return (y_prompt, y_sample) + p_st + s_st
```
